```python
import jax, jax.numpy as jnp
from jax import lax
import numpy as np

D_MODEL = 2048
BATCH = 8
SEQ = 8192
DEPTH = 2

D_MIX = D_MODEL
D_ATTN = D_MIX // 2
D_CONV = D_MIX // 4
D_POOL = D_MIX // 4
HEAD_DIM = 64
ATTN_HEADS = D_ATTN // HEAD_DIM
CONV_GROUPS = 8
CONV_WIDTH = 3
POOL_WINDOWS = (2, 4, 8, 16)
POOL_GROUP = D_POOL // len(POOL_WINDOWS)
DILATED_PATTERNS = ((128, 1), (512, 4), (2048, 16))
BLK = 128
D_IN = 3 * D_ATTN + 3 * D_CONV + D_POOL
D_FF = 5632
FFN_RESIDUAL = 0.5
RMS_EPS = 1e-6
NEG_INF = -1e30

kernel_name = "hybrid_dilated_conv_pool_macaron"


def rmsnorm(x, g):
    xf = x.astype(jnp.float32)
    y = xf * lax.rsqrt(jnp.mean(xf * xf, axis=-1, keepdims=True) + RMS_EPS)
    return (y * g.astype(jnp.float32)).astype(x.dtype)


def swiglu(h, w_gate, w_up, w_down):
    return (jax.nn.silu(h @ w_gate) * (h @ w_up)) @ w_down


def dilated_band_attention(q, k, v, window, dilation):
    B, S, H, Dh = q.shape
    d = dilation
    L = S // d
    span = window // d
    assert span <= BLK
    nb = -(-L // BLK)
    Lp = nb * BLK

    def to_sub(t):
        t = t.reshape(B, L, d, H, Dh).transpose(0, 2, 1, 3, 4).reshape(B * d, L, H, Dh)
        t = jnp.pad(t, ((0, 0), (0, Lp - L), (0, 0), (0, 0)))
        return t.reshape(B * d, nb, BLK, H, Dh)

    def with_prev(t):
        prev = jnp.pad(t, ((0, 0), (1, 0), (0, 0), (0, 0), (0, 0)))[:, :-1]
        return jnp.concatenate([prev, t], axis=2)

    qb = to_sub(q)
    kc = with_prev(to_sub(k))
    vc = with_prev(to_sub(v))
    s = jnp.einsum('znqhd,znkhd->znhqk', qb, kc).astype(jnp.float32) * (Dh ** -0.5)
    qi = jnp.arange(BLK)[:, None]
    kj = jnp.arange(2 * BLK)[None, :]
    dist = qi + BLK - kj
    blk = jnp.arange(nb)[:, None, None]
    valid = (dist >= 0) & (dist <= span) & (blk * BLK + kj - BLK >= 0)
    s = jnp.where(valid[None, :, None], s, NEG_INF)
    m = jnp.max(s, axis=-1, keepdims=True)
    p = jnp.exp(s - m)
    l = jnp.sum(p, axis=-1)
    num = jnp.einsum('znhqk,znkhd->znqhd', p.astype(v.dtype), vc).astype(jnp.float32)
    l_q = l.transpose(0, 1, 3, 2)
    o = num / l_q[..., None]
    lse = m[..., 0].transpose(0, 1, 3, 2) + jnp.log(l_q)

    def from_sub(t):
        rest = t.shape[3:]
        t = t.reshape((B * d, Lp) + rest)[:, :L]
        t = t.reshape((B, d, L) + rest)
        t = jnp.moveaxis(t, 1, 2)
        return t.reshape((B, S) + rest)

    return from_sub(o), from_sub(lse)


def dilated_attention(q, k, v):
    outs, lses = [], []
    for window, dilation in DILATED_PATTERNS:
        o, lse = dilated_band_attention(q, k, v, window, dilation)
        outs.append(o)
        lses.append(lse)
    w = jax.nn.softmax(jnp.stack(lses, axis=0), axis=0)
    o = jnp.sum(w[..., None] * jnp.stack(outs, axis=0), axis=0)
    return o.astype(q.dtype)


def causal_short_conv(u, conv_w):
    S = u.shape[1]
    up = jnp.pad(u, ((0, 0), (CONV_WIDTH - 1, 0), (0, 0)))
    return conv_w[0] * up[:, 0:S] + conv_w[1] * up[:, 1:S + 1] + conv_w[2] * up[:, 2:S + 2]


def multiscale_pool(u, pool_w, pool_scale):
    B, S, C = u.shape
    uf = u.astype(jnp.float32)
    csz = jnp.pad(jnp.cumsum(uf, axis=1), ((0, 0), (1, 0), (0, 0)))
    pos = jnp.arange(S)
    outs = []
    for g, w in enumerate(POOL_WINDOWS):
        sl = slice(g * POOL_GROUP, (g + 1) * POOL_GROUP)
        P = jnp.pad(csz[..., sl], ((0, 0), (w - 1, 0), (0, 0)))
        win_sum = P[:, w:w + S] - P[:, 0:S]
        count = jnp.minimum(pos + 1, w).astype(jnp.float32)[None, :, None]
        outs.append(win_sum / count - uf[..., sl])
    pooled = jnp.stack(outs, axis=2).astype(u.dtype)
    y = jnp.einsum('bsgc,gcd->bsgd', pooled, pool_w).reshape(B, S, C)
    return y * pool_scale


def hybrid_mixer(h, w_in, conv_w, pool_w, pool_scale, w_out):
    B, S, _ = h.shape
    z = h @ w_in
    cuts = np.cumsum([D_ATTN, D_ATTN, D_ATTN, D_CONV, D_CONV, D_CONV])
    q, k, v, gate_b, gate_c, conv_in, pool_in = jnp.split(z, cuts, axis=-1)
    heads = lambda t: t.reshape(B, S, ATTN_HEADS, HEAD_DIM)
    y_attn = dilated_attention(heads(q), heads(k), heads(v)).reshape(B, S, D_ATTN)
    y_conv = gate_b * causal_short_conv(gate_c * conv_in, conv_w)
    y_pool = multiscale_pool(pool_in, pool_w, pool_scale)
    return jnp.concatenate([y_attn, y_conv, y_pool], axis=-1) @ w_out


def _fwd_setup_inputs(seed: int = 0) -> dict:
    key = jax.random.key(seed)
    ks = jax.random.split(key, 16)
    f32 = jnp.float32

    def lin(k, shape, fan_in):
        return jax.random.normal(k, shape, f32) * (fan_in ** -0.5)

    def gain(k, shape, noise=0.02):
        return 1.0 + noise * jax.random.normal(k, shape, f32)

    return {
        "x": jax.random.normal(ks[0], (BATCH, SEQ, D_MODEL), f32),
        "ffn1_norm": gain(ks[1], (DEPTH, D_MODEL)),
        "ffn1_w_gate": lin(ks[2], (DEPTH, D_MODEL, D_FF), D_MODEL),
        "ffn1_w_up": lin(ks[3], (DEPTH, D_MODEL, D_FF), D_MODEL),
        "ffn1_w_down": lin(ks[4], (DEPTH, D_FF, D_MODEL), D_FF),
        "mix_norm": gain(ks[5], (DEPTH, D_MODEL)),
        "w_in": lin(ks[6], (DEPTH, D_MODEL, D_IN), D_MODEL),
        "conv_w": lin(ks[7], (DEPTH, CONV_WIDTH, D_CONV), CONV_WIDTH),
        "pool_w": lin(ks[8], (DEPTH, len(POOL_WINDOWS), POOL_GROUP, POOL_GROUP), POOL_GROUP),
        "pool_scale": gain(ks[9], (DEPTH, D_POOL), 0.1),
        "w_out": lin(ks[10], (DEPTH, D_MIX, D_MODEL), D_MIX),
        "ffn2_norm": gain(ks[11], (DEPTH, D_MODEL)),
        "ffn2_w_gate": lin(ks[12], (DEPTH, D_MODEL, D_FF), D_MODEL),
        "ffn2_w_up": lin(ks[13], (DEPTH, D_MODEL, D_FF), D_MODEL),
        "ffn2_w_down": lin(ks[14], (DEPTH, D_FF, D_MODEL), D_FF),
        "final_norm": gain(ks[15], (D_MODEL,)),
    }


def _fwd_reference(x, ffn1_norm, ffn1_w_gate, ffn1_w_up, ffn1_w_down, mix_norm, w_in, conv_w,
              pool_w, pool_scale, w_out, ffn2_norm, ffn2_w_gate, ffn2_w_up, ffn2_w_down,
              final_norm):
    for l in range(DEPTH):
        x = x + FFN_RESIDUAL * swiglu(rmsnorm(x, ffn1_norm[l]), ffn1_w_gate[l], ffn1_w_up[l], ffn1_w_down[l])
        x = x + hybrid_mixer(rmsnorm(x, mix_norm[l]), w_in[l], conv_w[l], pool_w[l], pool_scale[l], w_out[l])
        x = x + FFN_RESIDUAL * swiglu(rmsnorm(x, ffn2_norm[l]), ffn2_w_gate[l], ffn2_w_up[l], ffn2_w_down[l])
    return rmsnorm(x, final_norm)


import jax as _jax
import jax.numpy as _jnp

TWIN_FORMAT = 'train_step'
FWD_PARAMS = ['x', 'ffn1_norm', 'ffn1_w_gate', 'ffn1_w_up', 'ffn1_w_down', 'mix_norm', 'w_in', 'conv_w', 'pool_w', 'pool_scale', 'w_out', 'ffn2_norm', 'ffn2_w_gate', 'ffn2_w_up', 'ffn2_w_down', 'final_norm']
TWIN_WEIGHTS = ['ffn1_norm', 'ffn1_w_gate', 'ffn1_w_up', 'ffn1_w_down', 'mix_norm', 'w_in', 'conv_w', 'pool_w', 'pool_scale', 'w_out', 'ffn2_norm', 'ffn2_w_gate', 'ffn2_w_up', 'ffn2_w_down', 'final_norm']
TWIN_DIFF_INPUT = 'x'
TWIN_INPUTS = ['x', 'ffn1_norm', 'ffn1_w_gate', 'ffn1_w_up', 'ffn1_w_down', 'mix_norm', 'w_in', 'conv_w', 'pool_w', 'pool_scale', 'w_out', 'ffn2_norm', 'ffn2_w_gate', 'ffn2_w_up', 'ffn2_w_down', 'final_norm', 'loss_target', 'm_ffn1_norm', 'm_ffn1_w_gate', 'm_ffn1_w_up', 'm_ffn1_w_down', 'm_mix_norm', 'm_w_in', 'm_conv_w', 'm_pool_w', 'm_pool_scale', 'm_w_out', 'm_ffn2_norm', 'm_ffn2_w_gate', 'm_ffn2_w_up', 'm_ffn2_w_down', 'm_final_norm', 'v_ffn1_norm', 'v_ffn1_w_gate', 'v_ffn1_w_up', 'v_ffn1_w_down', 'v_mix_norm', 'v_w_in', 'v_conv_w', 'v_pool_w', 'v_pool_scale', 'v_w_out', 'v_ffn2_norm', 'v_ffn2_w_gate', 'v_ffn2_w_up', 'v_ffn2_w_down', 'v_final_norm']
TWIN_OUTPUTS = ['loss', 'grad_x', 'grad_ffn1_norm', 'grad_ffn1_w_gate', 'grad_ffn1_w_up', 'grad_ffn1_w_down', 'grad_mix_norm', 'grad_w_in', 'grad_conv_w', 'grad_pool_w', 'grad_pool_scale', 'grad_w_out', 'grad_ffn2_norm', 'grad_ffn2_w_gate', 'grad_ffn2_w_up', 'grad_ffn2_w_down', 'grad_final_norm', 'delta_ffn1_norm', 'delta_ffn1_w_gate', 'delta_ffn1_w_up', 'delta_ffn1_w_down', 'delta_mix_norm', 'delta_w_in', 'delta_conv_w', 'delta_pool_w', 'delta_pool_scale', 'delta_w_out', 'delta_ffn2_norm', 'delta_ffn2_w_gate', 'delta_ffn2_w_up', 'delta_ffn2_w_down', 'delta_final_norm', 'new_m_ffn1_norm', 'new_m_ffn1_w_gate', 'new_m_ffn1_w_up', 'new_m_ffn1_w_down', 'new_m_mix_norm', 'new_m_w_in', 'new_m_conv_w', 'new_m_pool_w', 'new_m_pool_scale', 'new_m_w_out', 'new_m_ffn2_norm', 'new_m_ffn2_w_gate', 'new_m_ffn2_w_up', 'new_m_ffn2_w_down', 'new_m_final_norm', 'new_v_ffn1_norm', 'new_v_ffn1_w_gate', 'new_v_ffn1_w_up', 'new_v_ffn1_w_down', 'new_v_mix_norm', 'new_v_w_in', 'new_v_conv_w', 'new_v_pool_w', 'new_v_pool_scale', 'new_v_w_out', 'new_v_ffn2_norm', 'new_v_ffn2_w_gate', 'new_v_ffn2_w_up', 'new_v_ffn2_w_down', 'new_v_final_norm']
TWIN_LEAF_KINDS = {'loss': 'loss', 'grad_x': 'grad_x', 'grad_ffn1_norm': 'grad_w', 'grad_ffn1_w_gate': 'grad_w', 'grad_ffn1_w_up': 'grad_w', 'grad_ffn1_w_down': 'grad_w', 'grad_mix_norm': 'grad_w', 'grad_w_in': 'grad_w', 'grad_conv_w': 'grad_w', 'grad_pool_w': 'grad_w', 'grad_pool_scale': 'grad_w', 'grad_w_out': 'grad_w', 'grad_ffn2_norm': 'grad_w', 'grad_ffn2_w_gate': 'grad_w', 'grad_ffn2_w_up': 'grad_w', 'grad_ffn2_w_down': 'grad_w', 'grad_final_norm': 'grad_w', 'delta_ffn1_norm': 'delta_w', 'delta_ffn1_w_gate': 'delta_w', 'delta_ffn1_w_up': 'delta_w', 'delta_ffn1_w_down': 'delta_w', 'delta_mix_norm': 'delta_w', 'delta_w_in': 'delta_w', 'delta_conv_w': 'delta_w', 'delta_pool_w': 'delta_w', 'delta_pool_scale': 'delta_w', 'delta_w_out': 'delta_w', 'delta_ffn2_norm': 'delta_w', 'delta_ffn2_w_gate': 'delta_w', 'delta_ffn2_w_up': 'delta_w', 'delta_ffn2_w_down': 'delta_w', 'delta_final_norm': 'delta_w', 'new_m_ffn1_norm': 'new_m', 'new_m_ffn1_w_gate': 'new_m', 'new_m_ffn1_w_up': 'new_m', 'new_m_ffn1_w_down': 'new_m', 'new_m_mix_norm': 'new_m', 'new_m_w_in': 'new_m', 'new_m_conv_w': 'new_m', 'new_m_pool_w': 'new_m', 'new_m_pool_scale': 'new_m', 'new_m_w_out': 'new_m', 'new_m_ffn2_norm': 'new_m', 'new_m_ffn2_w_gate': 'new_m', 'new_m_ffn2_w_up': 'new_m', 'new_m_ffn2_w_down': 'new_m', 'new_m_final_norm': 'new_m', 'new_v_ffn1_norm': 'new_v', 'new_v_ffn1_w_gate': 'new_v', 'new_v_ffn1_w_up': 'new_v', 'new_v_ffn1_w_down': 'new_v', 'new_v_mix_norm': 'new_v', 'new_v_w_in': 'new_v', 'new_v_conv_w': 'new_v', 'new_v_pool_w': 'new_v', 'new_v_pool_scale': 'new_v', 'new_v_w_out': 'new_v', 'new_v_ffn2_norm': 'new_v', 'new_v_ffn2_w_gate': 'new_v', 'new_v_ffn2_w_up': 'new_v', 'new_v_ffn2_w_down': 'new_v', 'new_v_final_norm': 'new_v'}


def _forward(args):
    return _fwd_reference(*[args[k] for k in FWD_PARAMS])


def _output_shape():
    def fwd():
        inp = _fwd_setup_inputs(0)
        return _fwd_reference(*[inp[k] for k in FWD_PARAMS])
    out = _jax.eval_shape(fwd)
    return out.shape, out.dtype

N_MICROBATCH = 1
ADAM_LR = 0.001
ADAM_B1 = 0.9
ADAM_B2 = 0.999
ADAM_EPS = 1e-08
ADAM_WD = 0.01
ADAM_STEP = 10
PER_EXAMPLE_BATCH_AXIS = {'x': 0, 'loss_target': 0}
SHARED_INPUTS = []
_WEIGHT_DTYPES = {'ffn1_norm': _jnp.float32, 'ffn1_w_gate': _jnp.float32, 'ffn1_w_up': _jnp.float32, 'ffn1_w_down': _jnp.float32, 'mix_norm': _jnp.float32, 'w_in': _jnp.float32, 'conv_w': _jnp.float32, 'pool_w': _jnp.float32, 'pool_scale': _jnp.float32, 'w_out': _jnp.float32, 'ffn2_norm': _jnp.float32, 'ffn2_w_gate': _jnp.float32, 'ffn2_w_up': _jnp.float32, 'ffn2_w_down': _jnp.float32, 'final_norm': _jnp.float32}
MOMENT_SCALE = {'ffn1_norm': 5.999718e-02, 'ffn1_w_gate': 2.606944e-02, 'ffn1_w_up': 2.525278e-02, 'ffn1_w_down': 4.190476e-02, 'mix_norm': 1.043273e-01, 'w_in': 6.601633e-02, 'conv_w': 1.064517e-01, 'pool_w': 9.353297e-02, 'pool_scale': 9.606974e-02, 'w_out': 7.139933e-02, 'ffn2_norm': 4.440439e-02, 'ffn2_w_gate': 1.873327e-02, 'ffn2_w_up': 1.813902e-02, 'ffn2_w_down': 3.011508e-02, 'final_norm': 3.197153e+01}


def _to_microbatches(a, axis):
    t = _jnp.moveaxis(a, axis, 0)
    t = t.reshape((N_MICROBATCH, t.shape[0] // N_MICROBATCH) + t.shape[1:])
    return _jnp.moveaxis(t, 1, axis + 1)


def setup_inputs(seed: int = 0) -> dict:
    inp = _fwd_setup_inputs(seed)
    key = _jax.random.fold_in(_jax.random.key(seed), 7919)
    shape, _ = _output_shape()
    out = dict(inp)
    out["loss_target"] = _jax.random.normal(_jax.random.fold_in(key, 0), shape, _jnp.float32)
    for i, name in enumerate(TWIN_WEIGHTS):
        w = inp[name].astype(_jnp.float32)
        if MOMENT_SCALE is None:
            s = _jnp.sqrt(_jnp.mean(_jnp.square(w)) + 1e-30)
        else:
            s = MOMENT_SCALE[name]
        km, kv = _jax.random.split(_jax.random.fold_in(key, i + 1))
        out[name] = w
        out["m_" + name] = s * _jax.random.normal(km, w.shape, _jnp.float32)
        out["v_" + name] = (s * s) * _jax.random.uniform(kv, w.shape, _jnp.float32, 0.5, 1.5)
    if N_MICROBATCH > 1:
        for name, axis in PER_EXAMPLE_BATCH_AXIS.items():
            out[name] = _to_microbatches(out[name], axis)
    return {'x': out['x'], 'ffn1_norm': out['ffn1_norm'], 'ffn1_w_gate': out['ffn1_w_gate'], 'ffn1_w_up': out['ffn1_w_up'], 'ffn1_w_down': out['ffn1_w_down'], 'mix_norm': out['mix_norm'], 'w_in': out['w_in'], 'conv_w': out['conv_w'], 'pool_w': out['pool_w'], 'pool_scale': out['pool_scale'], 'w_out': out['w_out'], 'ffn2_norm': out['ffn2_norm'], 'ffn2_w_gate': out['ffn2_w_gate'], 'ffn2_w_up': out['ffn2_w_up'], 'ffn2_w_down': out['ffn2_w_down'], 'final_norm': out['final_norm'], 'loss_target': out['loss_target'], 'm_ffn1_norm': out['m_ffn1_norm'], 'm_ffn1_w_gate': out['m_ffn1_w_gate'], 'm_ffn1_w_up': out['m_ffn1_w_up'], 'm_ffn1_w_down': out['m_ffn1_w_down'], 'm_mix_norm': out['m_mix_norm'], 'm_w_in': out['m_w_in'], 'm_conv_w': out['m_conv_w'], 'm_pool_w': out['m_pool_w'], 'm_pool_scale': out['m_pool_scale'], 'm_w_out': out['m_w_out'], 'm_ffn2_norm': out['m_ffn2_norm'], 'm_ffn2_w_gate': out['m_ffn2_w_gate'], 'm_ffn2_w_up': out['m_ffn2_w_up'], 'm_ffn2_w_down': out['m_ffn2_w_down'], 'm_final_norm': out['m_final_norm'], 'v_ffn1_norm': out['v_ffn1_norm'], 'v_ffn1_w_gate': out['v_ffn1_w_gate'], 'v_ffn1_w_up': out['v_ffn1_w_up'], 'v_ffn1_w_down': out['v_ffn1_w_down'], 'v_mix_norm': out['v_mix_norm'], 'v_w_in': out['v_w_in'], 'v_conv_w': out['v_conv_w'], 'v_pool_w': out['v_pool_w'], 'v_pool_scale': out['v_pool_scale'], 'v_w_out': out['v_w_out'], 'v_ffn2_norm': out['v_ffn2_norm'], 'v_ffn2_w_gate': out['v_ffn2_w_gate'], 'v_ffn2_w_up': out['v_ffn2_w_up'], 'v_ffn2_w_down': out['v_ffn2_w_down'], 'v_final_norm': out['v_final_norm']}


def _loss(weights, diff, rest, loss_target):
    with _jax.named_scope("forward"):
        args = {**rest, TWIN_DIFF_INPUT: diff, **{k: w.astype(_WEIGHT_DTYPES[k]) for k, w in weights.items()}}
        y = _forward(args)
    with _jax.named_scope("loss_head"):
        err = _jnp.square(y.astype(_jnp.float32) - loss_target)
        return 0.5 * _jnp.sum(_jnp.mean(err, axis=-1)) if err.ndim else 0.5 * err


def _adamw(w, g, m, v):
    m = ADAM_B1 * m + (1.0 - ADAM_B1) * g
    v = ADAM_B2 * v + (1.0 - ADAM_B2) * _jnp.square(g)
    m_hat = m / (1.0 - ADAM_B1 ** ADAM_STEP)
    v_hat = v / (1.0 - ADAM_B2 ** ADAM_STEP)
    delta = -ADAM_LR * (m_hat / (_jnp.sqrt(v_hat) + ADAM_EPS) + ADAM_WD * w)
    return delta, m, v


def reference(x, ffn1_norm, ffn1_w_gate, ffn1_w_up, ffn1_w_down, mix_norm, w_in, conv_w, pool_w, pool_scale, w_out, ffn2_norm, ffn2_w_gate, ffn2_w_up, ffn2_w_down, final_norm, loss_target, m_ffn1_norm, m_ffn1_w_gate, m_ffn1_w_up, m_ffn1_w_down, m_mix_norm, m_w_in, m_conv_w, m_pool_w, m_pool_scale, m_w_out, m_ffn2_norm, m_ffn2_w_gate, m_ffn2_w_up, m_ffn2_w_down, m_final_norm, v_ffn1_norm, v_ffn1_w_gate, v_ffn1_w_up, v_ffn1_w_down, v_mix_norm, v_w_in, v_conv_w, v_pool_w, v_pool_scale, v_w_out, v_ffn2_norm, v_ffn2_w_gate, v_ffn2_w_up, v_ffn2_w_down, v_final_norm):
    given = dict(x=x, ffn1_norm=ffn1_norm, ffn1_w_gate=ffn1_w_gate, ffn1_w_up=ffn1_w_up, ffn1_w_down=ffn1_w_down, mix_norm=mix_norm, w_in=w_in, conv_w=conv_w, pool_w=pool_w, pool_scale=pool_scale, w_out=w_out, ffn2_norm=ffn2_norm, ffn2_w_gate=ffn2_w_gate, ffn2_w_up=ffn2_w_up, ffn2_w_down=ffn2_w_down, final_norm=final_norm, loss_target=loss_target, m_ffn1_norm=m_ffn1_norm, m_ffn1_w_gate=m_ffn1_w_gate, m_ffn1_w_up=m_ffn1_w_up, m_ffn1_w_down=m_ffn1_w_down, m_mix_norm=m_mix_norm, m_w_in=m_w_in, m_conv_w=m_conv_w, m_pool_w=m_pool_w, m_pool_scale=m_pool_scale, m_w_out=m_w_out, m_ffn2_norm=m_ffn2_norm, m_ffn2_w_gate=m_ffn2_w_gate, m_ffn2_w_up=m_ffn2_w_up, m_ffn2_w_down=m_ffn2_w_down, m_final_norm=m_final_norm, v_ffn1_norm=v_ffn1_norm, v_ffn1_w_gate=v_ffn1_w_gate, v_ffn1_w_up=v_ffn1_w_up, v_ffn1_w_down=v_ffn1_w_down, v_mix_norm=v_mix_norm, v_w_in=v_w_in, v_conv_w=v_conv_w, v_pool_w=v_pool_w, v_pool_scale=v_pool_scale, v_w_out=v_w_out, v_ffn2_norm=v_ffn2_norm, v_ffn2_w_gate=v_ffn2_w_gate, v_ffn2_w_up=v_ffn2_w_up, v_ffn2_w_down=v_ffn2_w_down, v_final_norm=v_final_norm)
    weights = {n: given[n] for n in TWIN_WEIGHTS}
    shared = {n: given[n] for n in SHARED_INPUTS}
    per_example = {n: given[n] for n in ['x']}
    grad_fn = _jax.value_and_grad(_loss, argnums=(0, 1))

    def one_microbatch(ex, loss_target):
        ex = dict(ex)
        diff = ex.pop(TWIN_DIFF_INPUT)
        return grad_fn(weights, diff, {**shared, **ex}, loss_target)

    if N_MICROBATCH == 1:
        loss, (grad_w, grad_x) = one_microbatch(per_example, given["loss_target"])
    else:
        def body(carry, xs):
            loss_sum, grad_sum = carry
            l_k, (gw_k, gx_k) = one_microbatch(xs[0], xs[1])
            with _jax.named_scope("update"):
                return (loss_sum + l_k, _jax.tree.map(_jnp.add, grad_sum, gw_k)), gx_k

        init = (_jnp.zeros((), _jnp.float32), _jax.tree.map(_jnp.zeros_like, weights))
        (loss, grad_w), grad_x = _jax.lax.scan(body, init, (per_example, given["loss_target"]))
    with _jax.named_scope("update"):
        delta_w, new_m, new_v = {}, {}, {}
        for n in TWIN_WEIGHTS:
            delta_w[n], new_m[n], new_v[n] = _adamw(weights[n], grad_w[n], given["m_" + n], given["v_" + n])
    return (loss, grad_x, *[grad_w[n] for n in TWIN_WEIGHTS], *[delta_w[n] for n in TWIN_WEIGHTS],
            *[new_m[n] for n in TWIN_WEIGHTS], *[new_v[n] for n in TWIN_WEIGHTS])
```

```python
import functools

import jax
import jax.numpy as jnp
from jax import lax
from jax.experimental import pallas as pl
from jax.experimental.pallas import tpu as pltpu

F32 = jnp.float32
BF16 = jnp.bfloat16
MESH = pl.DeviceIdType.MESH

RMS_EPS = 1e-6
NEG_INF = -1e30
HEAD_DIM = 64
BLK = 128
SPAN = 128
DILATIONS = (1, 4, 16)
D_ATTN = 1024
D_CONV = 512
D_POOL = 512
POOL_WINDOWS = (2, 4, 8, 16)
POOL_GROUP = 128
D_IN = 3 * D_ATTN + 3 * D_CONV + D_POOL
D_MIX = D_ATTN + D_CONV + D_POOL
N_PAIR = D_ATTN // 128
ATTN_SCALE = HEAD_DIM ** -0.5
NSH = 4
ADAM_LR, ADAM_B1, ADAM_B2, ADAM_EPS, ADAM_WD, ADAM_STEP = 0.001, 0.9, 0.999, 1e-08, 0.01, 10

VMEM_LIMIT = 56 * 2 ** 20


def _cp(n_axes):
    return pltpu.CompilerParams(dimension_semantics=("arbitrary",) * n_axes, vmem_limit_bytes=VMEM_LIMIT)


def _sds(shape, dtype):
    return jax.ShapeDtypeStruct(shape, dtype)


def _dot(a, b):
    return jnp.dot(a, b, preferred_element_type=F32)


def _dot_nt(a, b):
    return lax.dot_general(a, b, (((1,), (1,)), ((), ())), preferred_element_type=F32)


def _dot_tn(a, b):
    return lax.dot_general(a, b, (((0,), (0,)), ((), ())), preferred_element_type=F32)


def _row_tile(s):
    return min(512, s)


def rms_fwd(x, g, name):
    S, D = x.shape
    tm = _row_tile(S)

    def body(x_ref, g_ref, h_ref):
        xv = x_ref[...]
        r = lax.rsqrt(jnp.mean(xv * xv, axis=-1, keepdims=True) + RMS_EPS)
        h_ref[...] = (xv * r * g_ref[...]).astype(BF16)

    return pl.pallas_call(
        body, grid=(S // tm,),
        in_specs=[pl.BlockSpec((tm, D), lambda i: (i, 0)), pl.BlockSpec((1, D), lambda i: (0, 0))],
        out_specs=pl.BlockSpec((tm, D), lambda i: (i, 0)),
        out_shape=_sds((S, D), BF16), name=name, compiler_params=_cp(1),
    )(x, g.reshape(1, D))


def _rms_bwd_tile(xv, gv, dh):
    r = lax.rsqrt(jnp.mean(xv * xv, axis=-1, keepdims=True) + RMS_EPS)
    xhat = xv * r
    dg = jnp.sum(dh * xhat, axis=0, keepdims=True)
    dxhat = dh * gv
    dx = r * (dxhat - xhat * jnp.mean(dxhat * xhat, axis=-1, keepdims=True))
    return dx, dg


def final_loss(x, g, target, name):
    S, D = x.shape
    tm = _row_tile(S)

    def body(x_ref, g_ref, t_ref, loss_ref, dx_ref, dxb_ref, dg_ref):
        i = pl.program_id(0)
        xv, gv = x_ref[...], g_ref[...]
        r = lax.rsqrt(jnp.mean(xv * xv, axis=-1, keepdims=True) + RMS_EPS)
        err = xv * r * gv - t_ref[...]
        part = 0.5 * jnp.sum(jnp.mean(err * err, axis=-1, keepdims=True), axis=0, keepdims=True)
        dx, dg = _rms_bwd_tile(xv, gv, err * (1.0 / D))

        @pl.when(i == 0)
        def _():
            loss_ref[...] = jnp.zeros_like(loss_ref)
            dg_ref[...] = jnp.zeros_like(dg_ref)

        loss_ref[...] += part
        dg_ref[...] += dg
        dx_ref[...] = dx
        dxb_ref[...] = dx.astype(BF16)

    row = pl.BlockSpec((tm, D), lambda i: (i, 0))
    vec = pl.BlockSpec((1, D), lambda i: (0, 0))
    return pl.pallas_call(
        body, grid=(S // tm,), in_specs=[row, vec, row],
        out_specs=[pl.BlockSpec((1, 1), lambda i: (0, 0)), row, row, vec],
        out_shape=[_sds((1, 1), F32), _sds((S, D), F32), _sds((S, D), BF16), _sds((1, D), F32)],
        name=name, compiler_params=_cp(1),
    )(x, g.reshape(1, D), target)


def _wspec(w, l, imap):
    _, _, a, b = w.shape
    return pl.BlockSpec((None, None, a, b), lambda *ids: (imap(*ids), l, 0, 0))


def ffn_up(h, wg, wu, l, name):
    S, D = h.shape
    FS = wg.shape[3]
    tm = _row_tile(S)

    def body(h_ref, wg_ref, wu_ref, g_ref, u_ref, a_ref):
        hv = h_ref[...]
        g = _dot(hv, wg_ref[...])
        u = _dot(hv, wu_ref[...])
        g_ref[...] = g.astype(BF16)
        u_ref[...] = u.astype(BF16)
        a_ref[...] = (g * jax.nn.sigmoid(g) * u).astype(BF16)

    out = pl.BlockSpec((tm, FS), lambda k, i: (i, k))
    shard = lambda k, i: k
    return pl.pallas_call(
        body, grid=(NSH, S // tm),
        in_specs=[pl.BlockSpec((tm, D), lambda k, i: (i, 0)), _wspec(wg, l, shard), _wspec(wu, l, shard)],
        out_specs=[out] * 3, out_shape=[_sds((S, NSH * FS), BF16)] * 3,
        name=name, compiler_params=_cp(2),
    )(h, wg, wu)


def col_mm(h, w, l, name):
    S, D = h.shape
    NS = w.shape[3]
    tm = _row_tile(S)

    def body(h_ref, w_ref, z_ref):
        z_ref[...] = _dot(h_ref[...], w_ref[...])

    return pl.pallas_call(
        body, grid=(NSH, S // tm),
        in_specs=[pl.BlockSpec((tm, D), lambda k, i: (i, 0)), _wspec(w, l, lambda k, i: k)],
        out_specs=pl.BlockSpec((tm, NS), lambda k, i: (i, k)), out_shape=_sds((S, NSH * NS), F32),
        name=name, compiler_params=_cp(2),
    )(h, w)


def resid_mm(a, w, l, x, scale, name):
    S = a.shape[0]
    KS, D = w.shape[2], w.shape[3]
    tm = _row_tile(S)

    def body(a_ref, w_ref, x_ref, o_ref, acc):
        k = pl.program_id(1)

        @pl.when(k == 0)
        def _():
            acc[...] = jnp.zeros_like(acc)

        acc[...] += _dot(a_ref[...], w_ref[...])

        @pl.when(k == NSH - 1)
        def _():
            o_ref[...] = x_ref[...] + scale * acc[...]

    row = pl.BlockSpec((tm, D), lambda i, k: (i, 0))
    return pl.pallas_call(
        body, grid=(S // tm, NSH),
        in_specs=[pl.BlockSpec((tm, KS), lambda i, k: (i, k)), _wspec(w, l, lambda i, k: k), row],
        out_specs=row, out_shape=_sds((S, D), F32),
        scratch_shapes=[pltpu.VMEM((tm, D), F32)], name=name, compiler_params=_cp(2),
    )(a, w, x)


def ffn_bwd_act(dxb, wd, l, g, u, name):
    S, D = dxb.shape
    FS = wd.shape[2]
    tm = _row_tile(S)

    def body(dx_ref, w_ref, g_ref, u_ref, dg_ref, du_ref):
        da = 0.5 * _dot_nt(dx_ref[...], w_ref[...])
        gv = g_ref[...].astype(F32)
        uv = u_ref[...].astype(F32)
        s = jax.nn.sigmoid(gv)
        du_ref[...] = (da * gv * s).astype(BF16)
        dg_ref[...] = (da * uv * s * (1.0 + gv * (1.0 - s))).astype(BF16)

    act = pl.BlockSpec((tm, FS), lambda k, i: (i, k))
    return pl.pallas_call(
        body, grid=(NSH, S // tm),
        in_specs=[pl.BlockSpec((tm, D), lambda k, i: (i, 0)), _wspec(wd, l, lambda k, i: k), act, act],
        out_specs=[act, act], out_shape=[_sds((S, NSH * FS), BF16)] * 2,
        name=name, compiler_params=_cp(2),
    )(dxb, wd, g, u)


def nt_col_mm(dxb, w, l, name):
    S, D = dxb.shape
    KS = w.shape[2]
    tm = _row_tile(S)

    def body(dx_ref, w_ref, o_ref):
        o_ref[...] = _dot_nt(dx_ref[...], w_ref[...])

    return pl.pallas_call(
        body, grid=(NSH, S // tm),
        in_specs=[pl.BlockSpec((tm, D), lambda k, i: (i, 0)), _wspec(w, l, lambda k, i: k)],
        out_specs=pl.BlockSpec((tm, KS), lambda k, i: (i, k)), out_shape=_sds((S, NSH * KS), F32),
        name=name, compiler_params=_cp(2),
    )(dxb, w)


def wgrad(lhs, rhs, lhs_sharded, rhs_sharded, scale, tr, tc, name):
    S = lhs.shape[0]
    R = lhs.shape[1] // (NSH if lhs_sharded else 1)
    C = rhs.shape[1] // (NSH if rhs_sharded else 1)
    ts = _row_tile(S)
    nr, nc = R // tr, C // tc

    def body(l_ref, r_ref, o_ref):
        @pl.when(pl.program_id(3) == 0)
        def _():
            o_ref[...] = jnp.zeros_like(o_ref)

        o_ref[...] += scale * _dot_tn(l_ref[...], r_ref[...])

    lmap = (lambda k, a, b, s: (s, k * nr + a)) if lhs_sharded else (lambda k, a, b, s: (s, a))
    rmap = (lambda k, a, b, s: (s, k * nc + b)) if rhs_sharded else (lambda k, a, b, s: (s, b))
    return pl.pallas_call(
        body, grid=(NSH, nr, nc, S // ts),
        in_specs=[pl.BlockSpec((ts, tr), lmap), pl.BlockSpec((ts, tc), rmap)],
        out_specs=pl.BlockSpec((None, tr, tc), lambda k, a, b, s: (k, a, b)),
        out_shape=_sds((NSH, R, C), F32), name=name, compiler_params=_cp(4),
    )(lhs, rhs)


def bwd_dh(dys, ws, l, x, g, dxin, name):
    S, D = x.shape
    NS = ws[0].shape[3]
    tm = _row_tile(S)
    nj = len(dys)

    def mm_body(*refs):
        dy_refs, w_refs, dh_ref = refs[:nj], refs[nj:2 * nj], refs[2 * nj]

        @pl.when(pl.program_id(1) == 0)
        def _():
            dh_ref[...] = jnp.zeros_like(dh_ref)

        for dy_ref, w_ref in zip(dy_refs, w_refs):
            dh_ref[...] += _dot_nt(dy_ref[...], w_ref[...])

    dh = pl.pallas_call(
        mm_body, grid=(S // tm, NSH),
        in_specs=[pl.BlockSpec((tm, NS), lambda i, k: (i, k))] * nj + [_wspec(w, l, lambda i, k: k) for w in ws],
        out_specs=pl.BlockSpec((tm, D), lambda i, k: (i, 0)), out_shape=_sds((S, D), F32),
        name=name + "_mm", compiler_params=_cp(2),
    )(*dys, *ws)

    def norm_body(dh_ref, x_ref, g_ref, dxin_ref, dx_ref, dxb_ref, dg_ref):
        dx, dg = _rms_bwd_tile(x_ref[...], g_ref[...], dh_ref[...])
        tot = dxin_ref[...] + dx
        dx_ref[...] = tot
        dxb_ref[...] = tot.astype(BF16)

        @pl.when(pl.program_id(0) == 0)
        def _():
            dg_ref[...] = jnp.zeros_like(dg_ref)

        dg_ref[...] += dg

    row = pl.BlockSpec((tm, D), lambda i: (i, 0))
    vec = pl.BlockSpec((1, D), lambda i: (0, 0))
    return pl.pallas_call(
        norm_body, grid=(S // tm,), in_specs=[row, row, vec, row], out_specs=[row, row, vec],
        out_shape=[_sds((S, D), F32), _sds((S, D), BF16), _sds((1, D), F32)],
        name=name + "_norm", compiler_params=_cp(1),
    )(dh, x, g.reshape(1, D), dxin)


def _band_valid(n):
    qi = lax.broadcasted_iota(jnp.int32, (BLK, 2 * BLK), 0)
    kj = lax.broadcasted_iota(jnp.int32, (BLK, 2 * BLK), 1)
    dist = qi + BLK - kj
    return (dist >= 0) & (dist <= SPAN) & ((kj >= BLK) | (n > 0))


def attn_fwd(z, d, name):
    S = z.shape[0]
    L = S // d
    nb = L // BLK
    zc = D_IN // D_ATTN
    zv = z.reshape(L, d * D_IN)

    def body(q_ref, kc_ref, kp_ref, vc_ref, vp_ref, o_ref, lse_ref):
        valid = _band_valid(pl.program_id(1))
        lane = lax.broadcasted_iota(jnp.int32, (1, 128), 1)
        stat = jnp.zeros((BLK, 128), F32)
        for hp in range(N_PAIR):
            sl = slice(hp * 128, (hp + 1) * 128)
            q2 = q_ref[:, sl]
            k2 = jnp.concatenate([kp_ref[:, sl], kc_ref[:, sl]], axis=0).astype(BF16)
            v2 = jnp.concatenate([vp_ref[:, sl], vc_ref[:, sl]], axis=0)
            o2 = jnp.zeros((BLK, 128), F32)
            for h in range(2):
                hm = (lane < HEAD_DIM) if h == 0 else (lane >= HEAD_DIM)
                qm = jnp.where(hm, q2, 0.0).astype(BF16)
                vm = jnp.where(hm, v2, 0.0).astype(BF16)
                s = jnp.where(valid, _dot_nt(qm, k2) * ATTN_SCALE, NEG_INF)
                m = jnp.max(s, axis=-1, keepdims=True)
                p = jnp.exp(s - m)
                lsum = jnp.sum(p, axis=-1, keepdims=True)
                o2 = o2 + _dot(p.astype(BF16), vm) / lsum
                stat = jnp.where(lane == 2 * hp + h, m + jnp.log(lsum), stat)
            o_ref[:, sl] = o2
        lse_ref[...] = stat

    blk = lambda c, prev: pl.BlockSpec(
        (BLK, D_ATTN), (lambda r, n: (jnp.maximum(n - 1, 0), r * zc + c)) if prev else (lambda r, n: (n, r * zc + c)))
    o, lse = pl.pallas_call(
        body, grid=(d, nb),
        in_specs=[blk(0, False), blk(1, False), blk(1, True), blk(2, False), blk(2, True)],
        out_specs=[pl.BlockSpec((BLK, D_ATTN), lambda r, n: (n, r)), pl.BlockSpec((BLK, 128), lambda r, n: (n, r))],
        out_shape=[_sds((L, d * D_ATTN), F32), _sds((L, d * 128), F32)],
        name=name, compiler_params=_cp(2),
    )(zv, zv, zv, zv, zv)
    return o.reshape(S, D_ATTN), lse.reshape(S, 128)


def _pair_weights(w, lane):
    return [jnp.where(lane < HEAD_DIM, w[:, 2 * hp:2 * hp + 1], w[:, 2 * hp + 1:2 * hp + 2]) for hp in range(N_PAIR)]


def attn_merge(os_, lses, name):
    S = os_[0].shape[0]
    tm = _row_tile(S)
    npat = len(os_)

    def body(*refs):
        o_refs, l_refs = refs[:npat], refs[npat:2 * npat]
        y_ref, yb_ref, lt_ref = refs[2 * npat:]
        lane = lax.broadcasted_iota(jnp.int32, (1, 128), 1)
        ls = [r[...] for r in l_refs]
        mx = functools.reduce(jnp.maximum, ls)
        es = [jnp.exp(v - mx) for v in ls]
        den = functools.reduce(jnp.add, es)
        lt_ref[...] = mx + jnp.log(den)
        ws = [_pair_weights(e / den, lane) for e in es]
        for hp in range(N_PAIR):
            sl = slice(hp * 128, (hp + 1) * 128)
            y = ws[0][hp] * o_refs[0][:, sl]
            for p in range(1, npat):
                y = y + ws[p][hp] * o_refs[p][:, sl]
            y_ref[:, sl] = y
            yb_ref[:, sl] = y.astype(BF16)

    big = pl.BlockSpec((tm, D_ATTN), lambda i: (i, 0))
    st = pl.BlockSpec((tm, 128), lambda i: (i, 0))
    return pl.pallas_call(
        body, grid=(S // tm,), in_specs=[big] * npat + [st] * npat, out_specs=[big, big, st],
        out_shape=[_sds((S, D_ATTN), F32), _sds((S, D_ATTN), BF16), _sds((S, 128), F32)],
        name=name, compiler_params=_cp(1),
    )(*os_, *lses)


def attn_bwd(z, dycat, y, lse_tot, d, name):
    S = z.shape[0]
    L = S // d
    nb = L // BLK
    zc = D_IN // D_ATTN
    yc = D_MIX // D_ATTN
    zv = z.reshape(L, d * D_IN)
    dyv = dycat.reshape(L, d * D_MIX)
    yv = y.reshape(L, d * D_ATTN)
    lv = lse_tot.reshape(L, d * 128)

    def body(q_ref, kc_ref, kp_ref, vc_ref, vp_ref, dy_ref, y_ref, l_ref, dq_ref, dk_ref, dv_ref, ck, cv):
        n = pl.program_id(1)

        @pl.when(n < nb)
        def _():
            valid = _band_valid(n)
            first = n == 0
            lane = lax.broadcasted_iota(jnp.int32, (1, 128), 1)
            for hp in range(N_PAIR):
                sl = slice(hp * 128, (hp + 1) * 128)
                q2, dy2, y2 = q_ref[:, sl], dy_ref[:, sl], y_ref[:, sl]
                k2 = jnp.concatenate([kp_ref[:, sl], kc_ref[:, sl]], axis=0)
                v2 = jnp.concatenate([vp_ref[:, sl], vc_ref[:, sl]], axis=0).astype(BF16)
                k2b = k2.astype(BF16)
                dq2 = jnp.zeros((BLK, 128), F32)
                dk2 = jnp.zeros((2 * BLK, 128), F32)
                dv2 = jnp.zeros((2 * BLK, 128), F32)
                for h in range(2):
                    hm = (lane < HEAD_DIM) if h == 0 else (lane >= HEAD_DIM)
                    qm = jnp.where(hm, q2, 0.0).astype(BF16)
                    km = jnp.where(hm, k2, 0.0).astype(BF16)
                    dym = jnp.where(hm, dy2, 0.0)
                    dymb = dym.astype(BF16)
                    lse_h = l_ref[:, 2 * hp + h:2 * hp + h + 1]
                    s = _dot_nt(qm, k2b) * ATTN_SCALE
                    p = jnp.where(valid, jnp.exp(s - lse_h), 0.0)
                    dp = _dot_nt(dymb, v2)
                    delta = jnp.sum(dym * y2, axis=-1, keepdims=True)
                    ds = (p * (dp - delta) * ATTN_SCALE).astype(BF16)
                    dq2 = dq2 + _dot(ds, km)
                    dk2 = dk2 + _dot_tn(ds, qm)
                    dv2 = dv2 + _dot_tn(p.astype(BF16), dymb)
                dq_ref[:, sl] = dq2
                dk_ref[:, sl] = jnp.where(first, 0.0, ck[:, sl]) + dk2[:BLK]
                dv_ref[:, sl] = jnp.where(first, 0.0, cv[:, sl]) + dv2[:BLK]
                ck[:, sl] = dk2[BLK:]
                cv[:, sl] = dv2[BLK:]

        @pl.when(n == nb)
        def _():
            dk_ref[...] = ck[...]
            dv_ref[...] = cv[...]

    cur = lambda n: jnp.minimum(n, nb - 1)
    prev = lambda n: jnp.clip(n - 1, 0, nb - 1)
    zb = lambda c, p: pl.BlockSpec((BLK, D_ATTN), lambda r, n: ((prev(n) if p else cur(n)), r * zc + c))
    big_cur = pl.BlockSpec((BLK, D_ATTN), lambda r, n: (cur(n), r))
    big_lag = pl.BlockSpec((BLK, D_ATTN), lambda r, n: (jnp.maximum(n - 1, 0), r))
    dq, dk, dv = pl.pallas_call(
        body, grid=(d, nb + 1),
        in_specs=[zb(0, False), zb(1, False), zb(1, True), zb(2, False), zb(2, True),
                  pl.BlockSpec((BLK, D_ATTN), lambda r, n: (cur(n), r * yc)), big_cur,
                  pl.BlockSpec((BLK, 128), lambda r, n: (cur(n), r))],
        out_specs=[big_cur, big_lag, big_lag],
        out_shape=[_sds((L, d * D_ATTN), F32)] * 3,
        scratch_shapes=[pltpu.VMEM((BLK, D_ATTN), F32)] * 2,
        name=name, compiler_params=_cp(2),
    )(zv, zv, zv, zv, zv, dyv, yv, lv)
    return dq.reshape(S, D_ATTN), dk.reshape(S, D_ATTN), dv.reshape(S, D_ATTN)


ZC_GB, ZC_GC, ZC_CI, ZC_PI = 6, 7, 8, 9


def _pool_counts(i, tb):
    pos = lax.broadcasted_iota(jnp.int32, (tb, D_POOL), 0) + i * tb + 1
    grp = lax.broadcasted_iota(jnp.int32, (tb, D_POOL), 1) // POOL_GROUP
    win = jnp.where(grp == 0, POOL_WINDOWS[0], jnp.where(grp == 1, POOL_WINDOWS[1],
                    jnp.where(grp == 2, POOL_WINDOWS[2], POOL_WINDOWS[3])))
    return jnp.minimum(pos, win).astype(F32), grp


def _select_group(stages, grp):
    return jnp.where(grp == 0, stages[0], jnp.where(grp == 1, stages[1], jnp.where(grp == 2, stages[2], stages[3])))


def _causal_window_sums(x2):
    s1 = x2 + pltpu.roll(x2, 1, 0)
    s2 = s1 + pltpu.roll(s1, 2, 0)
    s3 = s2 + pltpu.roll(s2, 4, 0)
    s4 = s3 + pltpu.roll(s3, 8, 0)
    return [s1, s2, s3, s4]


def _anticausal_window_sums(x2):
    n = x2.shape[0]
    s1 = x2 + pltpu.roll(x2, n - 1, 0)
    s2 = s1 + pltpu.roll(s1, n - 2, 0)
    s3 = s2 + pltpu.roll(s2, n - 4, 0)
    s4 = s3 + pltpu.roll(s3, n - 8, 0)
    return [s1, s2, s3, s4]


def _pooled(p_prev, p_cur, i, tb):
    x2 = jnp.concatenate([jnp.where(i > 0, p_prev, 0.0), p_cur], axis=0)
    count, grp = _pool_counts(i, tb)
    win_sum = _select_group([s[tb:] for s in _causal_window_sums(x2)], grp)
    return win_sum / count - p_cur, count, grp


def _pool_mm(v, pw_ref, nt):
    outs = []
    for gi in range(len(POOL_WINDOWS)):
        sl = slice(gi * POOL_GROUP, (gi + 1) * POOL_GROUP)
        outs.append(_dot_nt(v[:, sl], pw_ref[gi]) if nt else _dot(v[:, sl], pw_ref[gi]))
    return jnp.concatenate(outs, axis=1)


def conv_pool_fwd(z, conv_w, pool_w, pool_scale, name):
    S = z.shape[0]
    tb = min(256, S)

    def body(gb_ref, gc_ref, gcp_ref, ci_ref, cip_ref, pi_ref, pip_ref, cw_ref, pw_ref, ps_ref, yc_ref, yp_ref):
        i = pl.program_id(0)
        u2 = jnp.concatenate([jnp.where(i > 0, gcp_ref[...] * cip_ref[...], 0.0), gc_ref[...] * ci_ref[...]], axis=0)
        conv = cw_ref[0:1, :] * pltpu.roll(u2, 2, 0) + cw_ref[1:2, :] * pltpu.roll(u2, 1, 0) + cw_ref[2:3, :] * u2
        yc_ref[...] = (gb_ref[...] * conv[tb:]).astype(BF16)
        pooled, _, _ = _pooled(pip_ref[...], pi_ref[...], i, tb)
        yp_ref[...] = (_pool_mm(pooled.astype(BF16), pw_ref, False) * ps_ref[...]).astype(BF16)

    cur = lambda c: pl.BlockSpec((tb, 512), lambda i: (i, c))
    prev = lambda c: pl.BlockSpec((tb, 512), lambda i: (jnp.maximum(i - 1, 0), c))
    full = lambda a: pl.BlockSpec(a.shape, lambda i: (0,) * a.ndim)
    out = pl.BlockSpec((tb, 512), lambda i: (i, 0))
    return pl.pallas_call(
        body, grid=(S // tb,),
        in_specs=[cur(ZC_GB), cur(ZC_GC), prev(ZC_GC), cur(ZC_CI), prev(ZC_CI), cur(ZC_PI), prev(ZC_PI),
                  full(conv_w), full(pool_w), full(pool_scale)],
        out_specs=[out, out], out_shape=[_sds((S, 512), BF16)] * 2, name=name, compiler_params=_cp(1),
    )(z, z, z, z, z, z, z, conv_w, pool_w, pool_scale)


def conv_pool_bwd(z, dycat, conv_w, pool_w, pool_scale, name):
    S = z.shape[0]
    tb = min(256, S)
    nblk = S // tb
    ng = len(POOL_WINDOWS)

    def body(gb_ref, gbn_ref, gc_ref, gcp_ref, ci_ref, cip_ref, pi_ref, pip_ref, dyc_ref, dycn_ref, dyp_ref, dypn_ref,
             cw_ref, pw_ref, ps_ref, dz_ref, dcw_ref, dpw_ref, dps_ref):
        i = pl.program_id(0)
        last = i == nblk - 1

        @pl.when(i == 0)
        def _():
            dcw_ref[...] = jnp.zeros_like(dcw_ref)
            dpw_ref[...] = jnp.zeros_like(dpw_ref)
            dps_ref[...] = jnp.zeros_like(dps_ref)

        gc, ci = gc_ref[...], ci_ref[...]
        u2 = jnp.concatenate([jnp.where(i > 0, gcp_ref[...] * cip_ref[...], 0.0), gc * ci], axis=0)
        um2, um1, u0 = pltpu.roll(u2, 2, 0)[tb:], pltpu.roll(u2, 1, 0)[tb:], u2[tb:]
        conv = cw_ref[0:1, :] * um2 + cw_ref[1:2, :] * um1 + cw_ref[2:3, :] * u0
        dyc = dyc_ref[...]
        dconv = dyc * gb_ref[...]
        dc2 = jnp.concatenate([dconv, jnp.where(last, 0.0, dycn_ref[...] * gbn_ref[...])], axis=0)
        du = (cw_ref[2:3, :] * dconv + cw_ref[1:2, :] * pltpu.roll(dc2, 2 * tb - 1, 0)[:tb]
              + cw_ref[0:1, :] * pltpu.roll(dc2, 2 * tb - 2, 0)[:tb])
        dz_ref[:, 0:512] = (dyc * conv).astype(BF16)
        dz_ref[:, 512:1024] = (du * ci).astype(BF16)
        dz_ref[:, 1024:1536] = (du * gc).astype(BF16)
        dcw_ref[0:1, :] += jnp.sum(dconv * um2, axis=0, keepdims=True)
        dcw_ref[1:2, :] += jnp.sum(dconv * um1, axis=0, keepdims=True)
        dcw_ref[2:3, :] += jnp.sum(dconv * u0, axis=0, keepdims=True)

        pooled, count, grp = _pooled(pip_ref[...], pi_ref[...], i, tb)
        pooled_b = pooled.astype(BF16)
        t = _pool_mm(pooled_b, pw_ref, False)
        dyp, ps = dyp_ref[...], ps_ref[...]
        dps_ref[...] += jnp.sum(dyp * t, axis=0, keepdims=True)
        dt_b = (dyp * ps).astype(BF16)
        for gi in range(ng):
            sl = slice(gi * POOL_GROUP, (gi + 1) * POOL_GROUP)
            dpw_ref[gi] += _dot_tn(pooled_b[:, sl], dt_b[:, sl])
        dpooled = _pool_mm(dt_b, pw_ref, True)
        dpooled_n = _pool_mm((dypn_ref[...] * ps).astype(BF16), pw_ref, True)
        count_n, _ = _pool_counts(i + 1, tb)
        dq2 = jnp.concatenate([dpooled / count, jnp.where(last, 0.0, dpooled_n / count_n)], axis=0)
        lead = _select_group([s[:tb] for s in _anticausal_window_sums(dq2)], grp)
        dz_ref[:, 1536:2048] = (lead - dpooled).astype(BF16)

    cur = lambda c: pl.BlockSpec((tb, 512), lambda i: (i, c))
    prev = lambda c: pl.BlockSpec((tb, 512), lambda i: (jnp.maximum(i - 1, 0), c))
    nxt = lambda c: pl.BlockSpec((tb, 512), lambda i: (jnp.minimum(i + 1, nblk - 1), c))
    full = lambda a: pl.BlockSpec(a.shape, lambda i: (0,) * a.ndim)
    yc_c, yp_c = D_ATTN // 512, D_ATTN // 512 + 1
    return pl.pallas_call(
        body, grid=(nblk,),
        in_specs=[cur(ZC_GB), nxt(ZC_GB), cur(ZC_GC), prev(ZC_GC), cur(ZC_CI), prev(ZC_CI), cur(ZC_PI), prev(ZC_PI),
                  cur(yc_c), nxt(yc_c), cur(yp_c), nxt(yp_c), full(conv_w), full(pool_w), full(pool_scale)],
        out_specs=[pl.BlockSpec((tb, 2048), lambda i: (i, 0)), pl.BlockSpec((3, 512), lambda i: (0, 0)),
                   pl.BlockSpec((ng, POOL_GROUP, POOL_GROUP), lambda i: (0, 0, 0)), pl.BlockSpec((1, 512), lambda i: (0, 0))],
        out_shape=[_sds((S, 2048), BF16), _sds((3, 512), F32), _sds((ng, POOL_GROUP, POOL_GROUP), F32), _sds((1, 512), F32)],
        name=name, compiler_params=_cp(1),
    )(z, z, z, z, z, z, z, z, dycat, dycat, dycat, dycat, conv_w, pool_w, pool_scale)


def assemble_dz(parts_q, parts_k, parts_v, dcp, name):
    S = dcp.shape[0]
    tm = _row_tile(S)
    npat = len(parts_q)

    def body(*refs):
        dz_ref = refs[-1]
        dcp_ref = refs[-2]
        for j in range(3):
            acc = refs[j * npat][...]
            for p in range(1, npat):
                acc = acc + refs[j * npat + p][...]
            dz_ref[:, j * D_ATTN:(j + 1) * D_ATTN] = acc.astype(BF16)
        dz_ref[:, 3 * D_ATTN:] = dcp_ref[...]

    big = pl.BlockSpec((tm, D_ATTN), lambda i: (i, 0))
    return pl.pallas_call(
        body, grid=(S // tm,), in_specs=[big] * (3 * npat) + [pl.BlockSpec((tm, D_IN - 3 * D_ATTN), lambda i: (i, 0))],
        out_specs=pl.BlockSpec((tm, D_IN), lambda i: (i, 0)), out_shape=_sds((S, D_IN), BF16),
        name=name, compiler_params=_cp(1),
    )(*parts_q, *parts_k, *parts_v, dcp)


def adamw(w, g, m, v, name):
    shape = w.shape
    cols = shape[-1]
    rows = w.size // cols
    tr = rows
    for cand in (256, 128, 64, 32, 16, 8):
        if rows % cand == 0:
            tr = cand
            break
    c1 = 1.0 - ADAM_B1 ** ADAM_STEP
    c2 = 1.0 - ADAM_B2 ** ADAM_STEP

    def body(w_ref, g_ref, m_ref, v_ref, d_ref, mo_ref, vo_ref):
        gv = g_ref[...]
        mn = ADAM_B1 * m_ref[...] + (1.0 - ADAM_B1) * gv
        vn = ADAM_B2 * v_ref[...] + (1.0 - ADAM_B2) * (gv * gv)
        d_ref[...] = -ADAM_LR * ((mn / c1) / (jnp.sqrt(vn / c2) + ADAM_EPS) + ADAM_WD * w_ref[...])
        mo_ref[...] = mn
        vo_ref[...] = vn

    blk = pl.BlockSpec((tr, cols), lambda i: (i, 0))
    outs = pl.pallas_call(
        body, grid=(rows // tr,), in_specs=[blk] * 4, out_specs=[blk] * 3,
        out_shape=[_sds((rows, cols), F32)] * 3, name=name, compiler_params=_cp(1),
    )(*(a.reshape(rows, cols) for a in (w, g, m, v)))
    return tuple(o.reshape(shape) for o in outs)


ANY = pl.BlockSpec(memory_space=pl.ANY)


def _place():
    x, y, c = lax.axis_index("x"), lax.axis_index("y"), lax.axis_index("c")
    chips = [(1 - x, y), (x, 1 - y), (1 - x, 1 - y)]
    return x, y, c, chips


def gather_weights(ws):
    nw = len(ws)
    split = [w.shape[1] % 32 == 0 for w in ws]

    def body(*refs):
        ins, outs = refs[:nw], refs[nw:2 * nw]
        send, recv, lsem = refs[2 * nw:]
        x, y, c, chips = _place()
        me, sib = 2 * x + y, (x, y, 1 - c)

        def half(j, k, hc):
            if not split[j]:
                return outs[j].at[k]
            ha = ws[j].shape[1] // 2
            return outs[j].at[k, :, pl.ds(hc * ha, ha), :]

        def rcopy(j, s, src, dst, to):
            return pltpu.make_async_remote_copy(src_ref=src, dst_ref=dst, send_sem=send.at[j, s], recv_sem=recv.at[j, s],
                                                device_id=to, device_id_type=MESH)

        local = [pltpu.make_async_copy(ins[j], outs[j].at[me], lsem.at[j]) for j in range(nw)]
        for cp in local:
            cp.start()
        first = []
        for j in range(nw):
            ha = ws[j].shape[1] // 2
            mine = ins[j].at[:, pl.ds(c * ha, ha), :] if split[j] else ins[j]
            for s, (px, py) in enumerate(chips):
                first.append(rcopy(j, s, mine, half(j, me, c), (px, py, c)))
        for cp in first:
            cp.start()
        passed = []
        for j in range(nw):
            for s, (px, py) in enumerate(chips):
                blk = half(j, 2 * px + py, c)
                rcopy(j, s, blk, blk, (px, py, c)).wait_recv()
                if split[j]:
                    fwd = rcopy(j, 3 + s, blk, blk, sib)
                    fwd.start()
                    passed.append(fwd)
        for j in range(nw):
            for s, (px, py) in enumerate(chips):
                if split[j]:
                    blk = half(j, 2 * px + py, 1 - c)
                    rcopy(j, 3 + s, blk, blk, sib).wait_recv()
        for cp in first + passed:
            cp.wait_send()
        for cp in local:
            cp.wait()

    return pl.pallas_call(
        body, in_specs=[ANY] * nw, out_specs=[ANY] * nw,
        out_shape=[_sds((NSH,) + w.shape, w.dtype) for w in ws],
        scratch_shapes=[pltpu.SemaphoreType.DMA((nw, 6)), pltpu.SemaphoreType.DMA((nw, 6)), pltpu.SemaphoreType.DMA((nw,))],
        name="gather_weights",
    )(*ws)


def sibling_send_half(gs):
    ng = len(gs)

    def body(*refs):
        ins, outs = refs[:ng], refs[ng:2 * ng]
        send, recv = refs[2 * ng:]
        x, y, c, _ = _place()
        cps = []
        for j in range(ng):
            ha = gs[j].shape[1] // 2
            cps.append(pltpu.make_async_remote_copy(
                src_ref=ins[j].at[:, pl.ds((1 - c) * ha, ha), :], dst_ref=outs[j], send_sem=send.at[j], recv_sem=recv.at[j],
                device_id=(x, y, 1 - c), device_id_type=MESH))
        for cp in cps:
            cp.start()
        for cp in cps:
            cp.wait()

    return pl.pallas_call(
        body, in_specs=[ANY] * ng, out_specs=[ANY] * ng,
        out_shape=[_sds((NSH, g.shape[1] // 2, g.shape[2]), g.dtype) for g in gs],
        scratch_shapes=[pltpu.SemaphoreType.DMA((ng,)), pltpu.SemaphoreType.DMA((ng,))],
        name="sibling_send_half",
    )(*gs)


def pair_sum(g, got, half_index, name):
    _, A, B = g.shape
    ha = A // 2
    tr = ha
    for cand in (512, 256, 128, 64):
        if ha % cand == 0:
            tr = cand
            break
    nt = ha // tr

    def body(c_ref, g_ref, r_ref, o_ref):
        del c_ref
        o_ref[...] = (g_ref[...] + r_ref[...]).astype(BF16)

    return pl.pallas_call(
        body,
        grid_spec=pltpu.PrefetchScalarGridSpec(
            num_scalar_prefetch=1, grid=(NSH, nt),
            in_specs=[pl.BlockSpec((None, tr, B), lambda k, t, c: (k, c[0] * nt + t, 0)),
                      pl.BlockSpec((None, tr, B), lambda k, t, c: (k, t, 0))],
            out_specs=pl.BlockSpec((None, tr, B), lambda k, t, c: (k, t, 0))),
        out_shape=_sds((NSH, ha, B), BF16), name=name, compiler_params=_cp(2),
    )(half_index, g, got)


def chip_exchange(ps):
    n = len(ps)

    def body(*refs):
        ins, outs = refs[:n], refs[n:2 * n]
        send, recv, lsem = refs[2 * n:]
        x, y, c, chips = _place()
        me = 2 * x + y
        local = [pltpu.make_async_copy(ins[j].at[me], outs[j].at[me], lsem.at[j]) for j in range(n)]
        for cp in local:
            cp.start()
        cps = []
        for j in range(n):
            for s, (px, py) in enumerate(chips):
                cps.append(pltpu.make_async_remote_copy(
                    src_ref=ins[j].at[2 * px + py], dst_ref=outs[j].at[me], send_sem=send.at[j, s], recv_sem=recv.at[j, s],
                    device_id=(px, py, c), device_id_type=MESH))
        for cp in cps:
            cp.start()
        for cp in cps:
            cp.wait()
        for cp in local:
            cp.wait()

    return pl.pallas_call(
        body, in_specs=[ANY] * n, out_specs=[ANY] * n, out_shape=[_sds(p.shape, p.dtype) for p in ps],
        scratch_shapes=[pltpu.SemaphoreType.DMA((n, 3)), pltpu.SemaphoreType.DMA((n, 3)), pltpu.SemaphoreType.DMA((n,))],
        name="chip_exchange",
    )(*ps)


def chip_sum(parts, name):
    _, ha, B = parts.shape
    tr = ha
    for cand in (512, 256, 128, 64):
        if ha % cand == 0:
            tr = cand
            break

    def body(p_ref, o_ref):
        acc = p_ref[0].astype(F32)
        for k in range(1, NSH):
            acc = acc + p_ref[k].astype(F32)
        o_ref[...] = acc

    return pl.pallas_call(
        body, grid=(ha // tr,), in_specs=[pl.BlockSpec((NSH, tr, B), lambda t: (0, t, 0))],
        out_specs=pl.BlockSpec((tr, B), lambda t: (t, 0)), out_shape=_sds((ha, B), F32),
        name=name, compiler_params=_cp(1),
    )(parts)


def sibling_join(halves):
    nw = len(halves)
    depth = len(halves[0])
    flat = [h for hs in halves for h in hs]

    def body(*refs):
        ins, outs = refs[:nw * depth], refs[nw * depth:nw * depth + nw]
        send, recv, lsem = refs[nw * depth + nw:]
        x, y, c, _ = _place()
        local, cps = [], []
        for w in range(nw):
            ha = halves[w][0].shape[0]
            for l in range(depth):
                src = ins[w * depth + l]
                dst = outs[w].at[l, pl.ds(c * ha, ha), :]
                local.append(pltpu.make_async_copy(src, dst, lsem.at[w, l]))
                cps.append(pltpu.make_async_remote_copy(src_ref=src, dst_ref=dst, send_sem=send.at[w, l], recv_sem=recv.at[w, l],
                                                        device_id=(x, y, 1 - c), device_id_type=MESH))
        for cp in local + cps:
            cp.start()
        for cp in cps:
            cp.wait()
        for cp in local:
            cp.wait()

    return pl.pallas_call(
        body, in_specs=[ANY] * (nw * depth), out_specs=[ANY] * nw,
        out_shape=[_sds((depth, 2 * hs[0].shape[0], hs[0].shape[1]), F32) for hs in halves],
        scratch_shapes=[pltpu.SemaphoreType.DMA((nw, depth))] * 3, name="sibling_join",
    )(*flat)


def small_all_reduce(v):
    R = v.shape[0]

    def body(v_ref, o_ref, slots, send, recv):
        x, y, c, _ = _place()
        me = 4 * x + 2 * y + c
        slots[me] = v_ref[...]
        cps = []
        for m in range(1, 8):
            mx, my, mc = (m >> 2) & 1, (m >> 1) & 1, m & 1
            cps.append(pltpu.make_async_remote_copy(
                src_ref=v_ref, dst_ref=slots.at[me], send_sem=send.at[m - 1], recv_sem=recv.at[m - 1],
                device_id=(x ^ mx, y ^ my, c ^ mc), device_id_type=MESH))
        for cp in cps:
            cp.start()
        for cp in cps:
            cp.wait()
        acc = slots[0]
        for d in range(1, 8):
            acc = acc + slots[d]
        o_ref[...] = acc

    vm = pl.BlockSpec(memory_space=pltpu.VMEM)
    return pl.pallas_call(
        body, in_specs=[vm], out_specs=vm, out_shape=_sds((R, 128), F32),
        scratch_shapes=[pltpu.VMEM((8, R, 128), F32), pltpu.SemaphoreType.DMA((7,)), pltpu.SemaphoreType.DMA((7,))],
        name="small_all_reduce",
    )(v)


BIG = ("ffn1_w_gate", "ffn1_w_up", "ffn1_w_down", "w_in", "w_out", "ffn2_w_gate", "ffn2_w_up", "ffn2_w_down")
SMALL = ("ffn1_norm", "mix_norm", "conv_w", "pool_w", "pool_scale", "ffn2_norm", "final_norm")
WEIGHTS = ("ffn1_norm", "ffn1_w_gate", "ffn1_w_up", "ffn1_w_down", "mix_norm", "w_in", "conv_w", "pool_w", "pool_scale",
           "w_out", "ffn2_norm", "ffn2_w_gate", "ffn2_w_up", "ffn2_w_down", "final_norm")


def _tile_for(n, cap=1024):
    best = 128
    for t in range(128, min(n, cap) + 1, 128):
        if n % t == 0:
            best = t
    return n if n <= cap else best


def _ffn_forward(x, norm, wg, wu, wd, l, tag):
    h = rms_fwd(x, norm[l], f"{tag}_norm")
    g, u, a = ffn_up(h, wg, wu, l, f"{tag}_up")
    out = resid_mm(a, wd, l, x, 0.5, f"{tag}_down")
    return out, (x, h, g, u, a)


def _ffn_backward(saved, dx, dxb, norm, wg, wu, wd, l, tag):
    x, h, g, u, a = saved
    D, FS = wg.shape[2], wg.shape[3]
    dg, du = ffn_bwd_act(dxb, wd, l, g, u, f"{tag}_bwd_act")
    g_wd = wgrad(a, dxb, True, False, 0.5, FS, _tile_for(D), f"{tag}_dwd")
    g_wg = wgrad(h, dg, False, True, 1.0, _tile_for(D), FS, f"{tag}_dwg")
    g_wu = wgrad(h, du, False, True, 1.0, _tile_for(D), FS, f"{tag}_dwu")
    dx, dxb, dnorm = bwd_dh([dg, du], [wg, wu], l, x, norm[l], dx, f"{tag}_bwd_dh")
    return dx, dxb, dnorm, g_wg, g_wu, g_wd


def kernel(x, ffn1_norm, ffn1_w_gate, ffn1_w_up, ffn1_w_down, mix_norm, w_in, conv_w, pool_w, pool_scale, w_out, ffn2_norm, ffn2_w_gate, ffn2_w_up, ffn2_w_down, final_norm, loss_target, m_ffn1_norm, m_ffn1_w_gate, m_ffn1_w_up, m_ffn1_w_down, m_mix_norm, m_w_in, m_conv_w, m_pool_w, m_pool_scale, m_w_out, m_ffn2_norm, m_ffn2_w_gate, m_ffn2_w_up, m_ffn2_w_down, m_final_norm, v_ffn1_norm, v_ffn1_w_gate, v_ffn1_w_up, v_ffn1_w_down, v_mix_norm, v_w_in, v_conv_w, v_pool_w, v_pool_scale, v_w_out, v_ffn2_norm, v_ffn2_w_gate, v_ffn2_w_up, v_ffn2_w_down, v_final_norm):
    given = dict(locals())
    W = {n: given[n] for n in WEIGHTS}
    M = {n: given["m_" + n] for n in WEIGHTS}
    V = {n: given["v_" + n] for n in WEIGHTS}
    depth = ffn1_norm.shape[0]
    D = x.shape[-1]
    xs = x[0]
    my_chip = 2 * lax.axis_index("x") + lax.axis_index("y")
    my_core = lax.axis_index("c")

    gathered = gather_weights([W[n].astype(BF16) for n in BIG] + [conv_w])
    G = dict(zip(BIG, gathered[:-1]))
    conv_full = jnp.moveaxis(gathered[-1], 0, 2).reshape(depth, 3, D_CONV)
    pool_b = pool_w.astype(BF16)

    saved = []
    cur = xs
    for l in range(depth):
        cur, s1 = _ffn_forward(cur, ffn1_norm, G["ffn1_w_gate"], G["ffn1_w_up"], G["ffn1_w_down"], l, "ffn1")
        hm = rms_fwd(cur, mix_norm[l], "mix_norm")
        z = col_mm(hm, G["w_in"], l, "mix_in")
        pats = [attn_fwd(z, d, f"attn_fwd_d{d}") for d in DILATIONS]
        y, yb, lse_tot = attn_merge([p[0] for p in pats], [p[1] for p in pats], "attn_merge")
        y_conv, y_pool = conv_pool_fwd(z, conv_full[l], pool_b[l], pool_scale[l].reshape(1, D_POOL), "conv_pool_fwd")
        ycat = jnp.concatenate([yb, y_conv, y_pool], axis=1)
        x1 = cur
        cur = resid_mm(ycat, G["w_out"], l, x1, 1.0, "mix_out")
        sm = (x1, hm, z, y, lse_tot, ycat)
        cur, s2 = _ffn_forward(cur, ffn2_norm, G["ffn2_w_gate"], G["ffn2_w_up"], G["ffn2_w_down"], l, "ffn2")
        saved.append((s1, sm, s2))

    loss11, dx, dxb, d_final = final_loss(cur, final_norm, loss_target[0], "final_loss")

    big_grads = {n: [None] * depth for n in BIG}
    small_grads = {n: [None] * depth for n in SMALL if n != "final_norm"}
    for l in reversed(range(depth)):
        s1, sm, s2 = saved[l]
        dx, dxb, dn2, gg, gu, gd = _ffn_backward(s2, dx, dxb, ffn2_norm, G["ffn2_w_gate"], G["ffn2_w_up"], G["ffn2_w_down"], l, "ffn2")
        big_grads["ffn2_w_gate"][l], big_grads["ffn2_w_up"][l], big_grads["ffn2_w_down"][l] = gg, gu, gd
        small_grads["ffn2_norm"][l] = dn2

        x1, hm, z, y, lse_tot, ycat = sm
        dycat = nt_col_mm(dxb, G["w_out"], l, "mix_out_bwd")
        big_grads["w_out"][l] = wgrad(ycat, dxb, True, False, 1.0, D_MIX // NSH, _tile_for(D, 2048), "mix_dwout")
        parts = [attn_bwd(z, dycat, y, lse_tot, d, f"attn_bwd_d{d}") for d in DILATIONS]
        dcp, dcw, dpw, dps = conv_pool_bwd(z, dycat, conv_full[l], pool_b[l], pool_scale[l].reshape(1, D_POOL), "conv_pool_bwd")
        dz = assemble_dz([p[0] for p in parts], [p[1] for p in parts], [p[2] for p in parts], dcp, "assemble_dz")
        big_grads["w_in"][l] = wgrad(hm, dz, False, True, 1.0, _tile_for(D), D_IN // NSH, "mix_dwin")
        dx, dxb, dnm = bwd_dh([dz], [G["w_in"]], l, x1, mix_norm[l], dx, "mix_bwd_dh")
        small_grads["mix_norm"][l], small_grads["conv_w"][l] = dnm, dcw
        small_grads["pool_w"][l], small_grads["pool_scale"][l] = dpw, dps

        dx, dxb, dn1, gg, gu, gd = _ffn_backward(s1, dx, dxb, ffn1_norm, G["ffn1_w_gate"], G["ffn1_w_up"], G["ffn1_w_down"], l, "ffn1")
        big_grads["ffn1_w_gate"][l], big_grads["ffn1_w_up"][l], big_grads["ffn1_w_down"][l] = gg, gu, gd
        small_grads["ffn1_norm"][l] = dn1

    flat = [big_grads[n][l] for n in BIG for l in range(depth)]
    got = sibling_send_half(flat)
    half_index = my_core.astype(jnp.int32).reshape(1)
    partial = [pair_sum(g, r, half_index, "pair_sum") for g, r in zip(flat, got)]
    slots = chip_exchange(partial)
    reduced = [chip_sum(s, "chip_sum") for s in slots]
    joined = sibling_join([reduced[i * depth:(i + 1) * depth] for i in range(len(BIG))])
    grads = dict(zip(BIG, joined))

    small_full = {n: jnp.stack([a.reshape(W[n].shape[1:] if n != "conv_w" else (3, D_CONV)) for a in small_grads[n]])
                  for n in small_grads}
    small_full["final_norm"] = d_final.reshape(D)
    order = list(SMALL)
    packed = jnp.concatenate([small_full[n].reshape(-1) for n in order])
    pad = (-packed.shape[0]) % (8 * 128)
    packed = jnp.pad(packed, (0, pad)).reshape(-1, 128)
    summed = small_all_reduce(packed).reshape(-1)
    off = 0
    for n in order:
        size = small_full[n].size
        grads[n] = summed[off:off + size].reshape(small_full[n].shape)
        off += size
    grads["conv_w"] = lax.dynamic_slice_in_dim(grads["conv_w"], my_chip * (D_CONV // NSH), D_CONV // NSH, axis=2)

    delta, new_m, new_v = {}, {}, {}
    for n in BIG:
        delta[n], new_m[n], new_v[n] = adamw(W[n], grads[n], M[n], V[n], "adamw")

    def pack(src):
        flat_ = jnp.concatenate([src[n].reshape(-1) for n in order])
        return jnp.pad(flat_, (0, (-flat_.shape[0]) % (8 * 128))).reshape(-1, 128)

    ds, ms, vs = adamw(pack(W), pack(grads), pack(M), pack(V), "adamw_small")
    off = 0
    for n in order:
        size = W[n].size
        for dst, src in ((delta, ds), (new_m, ms), (new_v, vs)):
            dst[n] = src.reshape(-1)[off:off + size].reshape(W[n].shape)
        off += size

    loss = lax.psum(loss11[0, 0], ("x", "y", "c"))
    return (loss, dx.reshape(x.shape), *[grads[n] for n in WEIGHTS], *[delta[n] for n in WEIGHTS],
            *[new_m[n] for n in WEIGHTS], *[new_v[n] for n in WEIGHTS])
```

```python
import functools

import jax
import jax.numpy as jnp
from jax import lax
from jax.experimental import pallas as pl
from jax.experimental.pallas import tpu as pltpu

F32 = jnp.float32
BF16 = jnp.bfloat16
MESH = pl.DeviceIdType.MESH

RMS_EPS = 1e-6
NEG_INF = -1e30
HEAD_DIM = 64
BLK = 128
SPAN = 128
DILATIONS = (1, 4, 16)
D_ATTN = 1024
D_CONV = 512
D_POOL = 512
POOL_WINDOWS = (2, 4, 8, 16)
POOL_GROUP = 128
D_IN = 3 * D_ATTN + 3 * D_CONV + D_POOL
D_MIX = D_ATTN + D_CONV + D_POOL
N_PAIR = D_ATTN // 128
ATTN_SCALE = HEAD_DIM ** -0.5
NSH = 4
ADAM_LR, ADAM_B1, ADAM_B2, ADAM_EPS, ADAM_WD, ADAM_STEP = 0.001, 0.9, 0.999, 1e-08, 0.01, 10

VMEM_LIMIT = 56 * 2 ** 20


def _cp(n_axes):
    return pltpu.CompilerParams(dimension_semantics=("arbitrary",) * n_axes, vmem_limit_bytes=VMEM_LIMIT)


def _sds(shape, dtype):
    return jax.ShapeDtypeStruct(shape, dtype)


def _dot(a, b):
    return jnp.dot(a, b, preferred_element_type=F32)


def _dot_nt(a, b):
    return lax.dot_general(a, b, (((1,), (1,)), ((), ())), preferred_element_type=F32)


def _dot_tn(a, b):
    return lax.dot_general(a, b, (((0,), (0,)), ((), ())), preferred_element_type=F32)


def _row_tile(s):
    return min(512, s)


def rms_fwd(x, g, name):
    S, D = x.shape
    tm = _row_tile(S)

    def body(x_ref, g_ref, h_ref):
        xv = x_ref[...]
        r = lax.rsqrt(jnp.mean(xv * xv, axis=-1, keepdims=True) + RMS_EPS)
        h_ref[...] = (xv * r * g_ref[...]).astype(BF16)

    return pl.pallas_call(
        body, grid=(S // tm,),
        in_specs=[pl.BlockSpec((tm, D), lambda i: (i, 0)), pl.BlockSpec((1, D), lambda i: (0, 0))],
        out_specs=pl.BlockSpec((tm, D), lambda i: (i, 0)),
        out_shape=_sds((S, D), BF16), name=name, compiler_params=_cp(1),
    )(x, g.reshape(1, D))


def _rms_bwd_tile(xv, gv, dh):
    r = lax.rsqrt(jnp.mean(xv * xv, axis=-1, keepdims=True) + RMS_EPS)
    xhat = xv * r
    dg = jnp.sum(dh * xhat, axis=0, keepdims=True)
    dxhat = dh * gv
    dx = r * (dxhat - xhat * jnp.mean(dxhat * xhat, axis=-1, keepdims=True))
    return dx, dg


def final_loss(x, g, target, name):
    S, D = x.shape
    tm = _row_tile(S)

    def body(x_ref, g_ref, t_ref, loss_ref, dx_ref, dxb_ref, dg_ref):
        i = pl.program_id(0)
        xv, gv = x_ref[...], g_ref[...]
        r = lax.rsqrt(jnp.mean(xv * xv, axis=-1, keepdims=True) + RMS_EPS)
        err = xv * r * gv - t_ref[...]
        part = 0.5 * jnp.sum(jnp.mean(err * err, axis=-1, keepdims=True), axis=0, keepdims=True)
        dx, dg = _rms_bwd_tile(xv, gv, err * (1.0 / D))

        @pl.when(i == 0)
        def _():
            loss_ref[...] = jnp.zeros_like(loss_ref)
            dg_ref[...] = jnp.zeros_like(dg_ref)

        loss_ref[...] += part
        dg_ref[...] += dg
        dx_ref[...] = dx
        dxb_ref[...] = dx.astype(BF16)

    row = pl.BlockSpec((tm, D), lambda i: (i, 0))
    vec = pl.BlockSpec((1, D), lambda i: (0, 0))
    return pl.pallas_call(
        body, grid=(S // tm,), in_specs=[row, vec, row],
        out_specs=[pl.BlockSpec((1, 1), lambda i: (0, 0)), row, row, vec],
        out_shape=[_sds((1, 1), F32), _sds((S, D), F32), _sds((S, D), BF16), _sds((1, D), F32)],
        name=name, compiler_params=_cp(1),
    )(x, g.reshape(1, D), target)


def _wspec(w, l, imap):
    _, _, a, b = w.shape
    return pl.BlockSpec((None, None, a, b), lambda *ids: (imap(*ids), l, 0, 0))


def ffn_up(h, wg, wu, l, name):
    S, D = h.shape
    FS = wg.shape[3]
    tm = _row_tile(S)

    def body(h_ref, wg_ref, wu_ref, g_ref, u_ref, a_ref):
        hv = h_ref[...]
        g = _dot(hv, wg_ref[...])
        u = _dot(hv, wu_ref[...])
        g_ref[...] = g.astype(BF16)
        u_ref[...] = u.astype(BF16)
        a_ref[...] = (g * jax.nn.sigmoid(g) * u).astype(BF16)

    out = pl.BlockSpec((tm, FS), lambda k, i: (i, k))
    shard = lambda k, i: k
    return pl.pallas_call(
        body, grid=(NSH, S // tm),
        in_specs=[pl.BlockSpec((tm, D), lambda k, i: (i, 0)), _wspec(wg, l, shard), _wspec(wu, l, shard)],
        out_specs=[out] * 3, out_shape=[_sds((S, NSH * FS), BF16)] * 3,
        name=name, compiler_params=_cp(2),
    )(h, wg, wu)


def col_mm(h, w, l, name):
    S, D = h.shape
    NS = w.shape[3]
    tm = _row_tile(S)

    def body(h_ref, w_ref, z_ref):
        z_ref[...] = _dot(h_ref[...], w_ref[...])

    return pl.pallas_call(
        body, grid=(NSH, S // tm),
        in_specs=[pl.BlockSpec((tm, D), lambda k, i: (i, 0)), _wspec(w, l, lambda k, i: k)],
        out_specs=pl.BlockSpec((tm, NS), lambda k, i: (i, k)), out_shape=_sds((S, NSH * NS), F32),
        name=name, compiler_params=_cp(2),
    )(h, w)


def resid_mm(a, w, l, x, scale, name):
    S = a.shape[0]
    KS, D = w.shape[2], w.shape[3]
    tm = _row_tile(S)

    def body(a_ref, w_ref, x_ref, o_ref, acc):
        k = pl.program_id(1)

        @pl.when(k == 0)
        def _():
            acc[...] = jnp.zeros_like(acc)

        acc[...] += _dot(a_ref[...], w_ref[...])

        @pl.when(k == NSH - 1)
        def _():
            o_ref[...] = x_ref[...] + scale * acc[...]

    row = pl.BlockSpec((tm, D), lambda i, k: (i, 0))
    return pl.pallas_call(
        body, grid=(S // tm, NSH),
        in_specs=[pl.BlockSpec((tm, KS), lambda i, k: (i, k)), _wspec(w, l, lambda i, k: k), row],
        out_specs=row, out_shape=_sds((S, D), F32),
        scratch_shapes=[pltpu.VMEM((tm, D), F32)], name=name, compiler_params=_cp(2),
    )(a, w, x)


def ffn_bwd_act(dxb, wd, l, g, u, name):
    S, D = dxb.shape
    FS = wd.shape[2]
    tm = _row_tile(S)

    def body(dx_ref, w_ref, g_ref, u_ref, dg_ref, du_ref):
        da = 0.5 * _dot_nt(dx_ref[...], w_ref[...])
        gv = g_ref[...].astype(F32)
        uv = u_ref[...].astype(F32)
        s = jax.nn.sigmoid(gv)
        du_ref[...] = (da * gv * s).astype(BF16)
        dg_ref[...] = (da * uv * s * (1.0 + gv * (1.0 - s))).astype(BF16)

    act = pl.BlockSpec((tm, FS), lambda k, i: (i, k))
    return pl.pallas_call(
        body, grid=(NSH, S // tm),
        in_specs=[pl.BlockSpec((tm, D), lambda k, i: (i, 0)), _wspec(wd, l, lambda k, i: k), act, act],
        out_specs=[act, act], out_shape=[_sds((S, NSH * FS), BF16)] * 2,
        name=name, compiler_params=_cp(2),
    )(dxb, wd, g, u)


def nt_col_mm(dxb, w, l, name):
    S, D = dxb.shape
    KS = w.shape[2]
    tm = _row_tile(S)

    def body(dx_ref, w_ref, o_ref):
        o_ref[...] = _dot_nt(dx_ref[...], w_ref[...])

    return pl.pallas_call(
        body, grid=(NSH, S // tm),
        in_specs=[pl.BlockSpec((tm, D), lambda k, i: (i, 0)), _wspec(w, l, lambda k, i: k)],
        out_specs=pl.BlockSpec((tm, KS), lambda k, i: (i, k)), out_shape=_sds((S, NSH * KS), F32),
        name=name, compiler_params=_cp(2),
    )(dxb, w)


def wgrad(lhs, rhs, lhs_sharded, rhs_sharded, scale, tr, tc, name):
    S = lhs.shape[0]
    R = lhs.shape[1] // (NSH if lhs_sharded else 1)
    C = rhs.shape[1] // (NSH if rhs_sharded else 1)
    ts = _row_tile(S)
    nr, nc = R // tr, C // tc

    def body(l_ref, r_ref, o_ref):
        @pl.when(pl.program_id(3) == 0)
        def _():
            o_ref[...] = jnp.zeros_like(o_ref)

        o_ref[...] += scale * _dot_tn(l_ref[...], r_ref[...])

    lmap = (lambda k, a, b, s: (s, k * nr + a)) if lhs_sharded else (lambda k, a, b, s: (s, a))
    rmap = (lambda k, a, b, s: (s, k * nc + b)) if rhs_sharded else (lambda k, a, b, s: (s, b))
    return pl.pallas_call(
        body, grid=(NSH, nr, nc, S // ts),
        in_specs=[pl.BlockSpec((ts, tr), lmap), pl.BlockSpec((ts, tc), rmap)],
        out_specs=pl.BlockSpec((None, tr, tc), lambda k, a, b, s: (k, a, b)),
        out_shape=_sds((NSH, R, C), F32), name=name, compiler_params=_cp(4),
    )(lhs, rhs)


def bwd_dh(dys, ws, l, x, g, dxin, name):
    S, D = x.shape
    NS = ws[0].shape[3]
    tm = _row_tile(S)
    nj = len(dys)

    def mm_body(*refs):
        dy_refs, w_refs, dh_ref = refs[:nj], refs[nj:2 * nj], refs[2 * nj]

        @pl.when(pl.program_id(1) == 0)
        def _():
            dh_ref[...] = jnp.zeros_like(dh_ref)

        for dy_ref, w_ref in zip(dy_refs, w_refs):
            dh_ref[...] += _dot_nt(dy_ref[...], w_ref[...])

    dh = pl.pallas_call(
        mm_body, grid=(S // tm, NSH),
        in_specs=[pl.BlockSpec((tm, NS), lambda i, k: (i, k))] * nj + [_wspec(w, l, lambda i, k: k) for w in ws],
        out_specs=pl.BlockSpec((tm, D), lambda i, k: (i, 0)), out_shape=_sds((S, D), F32),
        name=name + "_mm", compiler_params=_cp(2),
    )(*dys, *ws)

    def norm_body(dh_ref, x_ref, g_ref, dxin_ref, dx_ref, dxb_ref, dg_ref):
        dx, dg = _rms_bwd_tile(x_ref[...], g_ref[...], dh_ref[...])
        tot = dxin_ref[...] + dx
        dx_ref[...] = tot
        dxb_ref[...] = tot.astype(BF16)

        @pl.when(pl.program_id(0) == 0)
        def _():
            dg_ref[...] = jnp.zeros_like(dg_ref)

        dg_ref[...] += dg

    row = pl.BlockSpec((tm, D), lambda i: (i, 0))
    vec = pl.BlockSpec((1, D), lambda i: (0, 0))
    return pl.pallas_call(
        norm_body, grid=(S // tm,), in_specs=[row, row, vec, row], out_specs=[row, row, vec],
        out_shape=[_sds((S, D), F32), _sds((S, D), BF16), _sds((1, D), F32)],
        name=name + "_norm", compiler_params=_cp(1),
    )(dh, x, g.reshape(1, D), dxin)


PAIRS_PER_STEP = {1: 8, 4: 1, 16: 1}
ZQ, ZK, ZV = 0, D_ATTN // 128, 2 * D_ATTN // 128


def _band_valid(n):
    qi = lax.broadcasted_iota(jnp.int32, (BLK, 2 * BLK), 0)
    kj = lax.broadcasted_iota(jnp.int32, (BLK, 2 * BLK), 1)
    dist = qi + BLK - kj
    return (dist >= 0) & (dist <= SPAN) & ((kj >= BLK) | (n > 0))


def _split_residues(src_ref, dst, d, pp, row0=0):
    for r in range(d):
        for p in range(pp):
            dst[r * pp + p, row0:row0 + BLK, :] = src_ref[pl.ds(r, BLK, stride=d), p * 128:(p + 1) * 128].astype(dst.dtype)


def _lane_pick(stat, lane, idx):
    return jnp.sum(jnp.where(lane == idx, stat, 0.0), axis=-1, keepdims=True)


def attn_fwd(z, d, name):
    S = z.shape[0]
    R = BLK * d
    nb = S // R
    pp = PAIRS_PER_STEP[d]
    G = d * pp
    W = 128 * pp

    def body(q_ref, kc_ref, kp_ref, vc_ref, vp_ref, o_ref, lse_ref, qs, ks, vs, os_, ls):
        n, hb = pl.program_id(0), pl.program_id(1)
        valid = _band_valid(n)
        lane = lax.broadcasted_iota(jnp.int32, (1, 128), 1)
        _split_residues(q_ref, qs, d, pp)
        _split_residues(kp_ref, ks, d, pp)
        _split_residues(kc_ref, ks, d, pp, BLK)
        _split_residues(vp_ref, vs, d, pp)
        _split_residues(vc_ref, vs, d, pp, BLK)

        def step(j, carry):
            q2, k2, v2 = qs[j], ks[j], vs[j]
            pair = hb * pp + j % pp
            o2 = jnp.zeros((BLK, 128), F32)
            stat = jnp.zeros((BLK, 128), F32)
            for h in range(2):
                hm = (lane < HEAD_DIM) if h == 0 else (lane >= HEAD_DIM)
                qm = jnp.where(hm, q2, jnp.zeros_like(q2))
                vm = jnp.where(hm, v2, jnp.zeros_like(v2))
                s = jnp.where(valid, _dot_nt(qm, k2) * ATTN_SCALE, NEG_INF)
                m = jnp.max(s, axis=-1, keepdims=True)
                p = jnp.exp(s - m)
                lsum = jnp.sum(p, axis=-1, keepdims=True)
                o2 = o2 + _dot(p.astype(BF16), vm) / lsum
                stat = jnp.where(lane == 2 * pair + h, m + jnp.log(lsum), stat)
            os_[j] = o2
            ls[j] = stat
            return carry

        lax.fori_loop(0, G, step, 0)

        @pl.when(hb == 0)
        def _():
            lse_ref[...] = jnp.zeros_like(lse_ref)

        for r in range(d):
            rows = pl.ds(r, BLK, stride=d)
            acc = lse_ref[rows, :]
            for p in range(pp):
                o_ref[rows, p * 128:(p + 1) * 128] = os_[r * pp + p]
                acc = acc + ls[r * pp + p]
            lse_ref[rows, :] = acc

    cur = lambda c: pl.BlockSpec((R, W), lambda n, hb: (n, c // pp + hb))
    prev = lambda c: pl.BlockSpec((R, W), lambda n, hb: (jnp.maximum(n - 1, 0), c // pp + hb))
    return pl.pallas_call(
        body, grid=(nb, N_PAIR // pp),
        in_specs=[cur(ZQ), cur(ZK), prev(ZK), cur(ZV), prev(ZV)],
        out_specs=[pl.BlockSpec((R, W), lambda n, hb: (n, hb)), pl.BlockSpec((R, 128), lambda n, hb: (n, 0))],
        out_shape=[_sds((S, D_ATTN), F32), _sds((S, 128), F32)],
        scratch_shapes=[pltpu.VMEM((G, BLK, 128), BF16), pltpu.VMEM((G, 2 * BLK, 128), BF16), pltpu.VMEM((G, 2 * BLK, 128), BF16),
                        pltpu.VMEM((G, BLK, 128), F32), pltpu.VMEM((G, BLK, 128), F32)],
        name=name, compiler_params=_cp(2),
    )(z, z, z, z, z)


def _pair_weights(w, lane):
    return [jnp.where(lane < HEAD_DIM, w[:, 2 * hp:2 * hp + 1], w[:, 2 * hp + 1:2 * hp + 2]) for hp in range(N_PAIR)]


def attn_merge(os_, lses, name):
    S = os_[0].shape[0]
    tm = _row_tile(S)
    npat = len(os_)

    def body(*refs):
        o_refs, l_refs = refs[:npat], refs[npat:2 * npat]
        y_ref, yb_ref, lt_ref = refs[2 * npat:]
        lane = lax.broadcasted_iota(jnp.int32, (1, 128), 1)
        ls = [r[...] for r in l_refs]
        mx = functools.reduce(jnp.maximum, ls)
        es = [jnp.exp(v - mx) for v in ls]
        den = functools.reduce(jnp.add, es)
        lt_ref[...] = mx + jnp.log(den)
        ws = [_pair_weights(e / den, lane) for e in es]
        for hp in range(N_PAIR):
            sl = slice(hp * 128, (hp + 1) * 128)
            y = ws[0][hp] * o_refs[0][:, sl]
            for p in range(1, npat):
                y = y + ws[p][hp] * o_refs[p][:, sl]
            y_ref[:, sl] = y
            yb_ref[:, sl] = y.astype(BF16)

    big = pl.BlockSpec((tm, D_ATTN), lambda i: (i, 0))
    st = pl.BlockSpec((tm, 128), lambda i: (i, 0))
    return pl.pallas_call(
        body, grid=(S // tm,), in_specs=[big] * npat + [st] * npat, out_specs=[big, big, st],
        out_shape=[_sds((S, D_ATTN), F32), _sds((S, D_ATTN), BF16), _sds((S, 128), F32)],
        name=name, compiler_params=_cp(1),
    )(*os_, *lses)


def attn_bwd(z, dycat, y, lse_tot, d, name):
    S = z.shape[0]
    R = BLK * d
    nb = S // R
    pp = PAIRS_PER_STEP[d]
    G = d * pp
    W = 128 * pp

    def body(q_ref, kc_ref, kp_ref, vc_ref, vp_ref, dy_ref, y_ref, l_ref, dq_ref, dk_ref, dv_ref,
             qs, ks, vs, dys, ys, ls, dqs, dks, dvs, ck, cv):
        hb, n = pl.program_id(0), pl.program_id(1)

        def store_rows(ref, buf):
            for r in range(d):
                for p in range(pp):
                    ref[pl.ds(r, BLK, stride=d), p * 128:(p + 1) * 128] = buf[r * pp + p]

        @pl.when(n < nb)
        def _():
            valid = _band_valid(n)
            first = n == 0
            lane = lax.broadcasted_iota(jnp.int32, (1, 128), 1)
            _split_residues(q_ref, qs, d, pp)
            _split_residues(kp_ref, ks, d, pp)
            _split_residues(kc_ref, ks, d, pp, BLK)
            _split_residues(vp_ref, vs, d, pp)
            _split_residues(vc_ref, vs, d, pp, BLK)
            _split_residues(dy_ref, dys, d, pp)
            _split_residues(y_ref, ys, d, pp)
            _split_residues(l_ref, ls, d, 1)

            def step(j, carry):
                q2, k2, v2, dy2, y2 = qs[j], ks[j], vs[j], dys[j], ys[j]
                stat = ls[j // pp]
                pair = hb * pp + j % pp
                dq2 = jnp.zeros((BLK, 128), F32)
                dk2 = jnp.zeros((2 * BLK, 128), F32)
                dv2 = jnp.zeros((2 * BLK, 128), F32)
                for h in range(2):
                    hm = (lane < HEAD_DIM) if h == 0 else (lane >= HEAD_DIM)
                    qm = jnp.where(hm, q2, jnp.zeros_like(q2))
                    km = jnp.where(hm, k2, jnp.zeros_like(k2))
                    dym = jnp.where(hm, dy2, 0.0)
                    dymb = dym.astype(BF16)
                    s = _dot_nt(qm, k2) * ATTN_SCALE
                    p = jnp.where(valid, jnp.exp(s - _lane_pick(stat, lane, 2 * pair + h)), 0.0)
                    dp = _dot_nt(dymb, v2)
                    delta = jnp.sum(dym * y2, axis=-1, keepdims=True)
                    ds = (p * (dp - delta) * ATTN_SCALE).astype(BF16)
                    dq2 = dq2 + _dot(ds, km)
                    dk2 = dk2 + _dot_tn(ds, qm)
                    dv2 = dv2 + _dot_tn(p.astype(BF16), dymb)
                dqs[j] = dq2
                dks[j] = jnp.where(first, 0.0, ck[j]) + dk2[:BLK]
                dvs[j] = jnp.where(first, 0.0, cv[j]) + dv2[:BLK]
                ck[j] = dk2[BLK:]
                cv[j] = dv2[BLK:]
                return carry

            lax.fori_loop(0, G, step, 0)
            store_rows(dq_ref, dqs)
            store_rows(dk_ref, dks)
            store_rows(dv_ref, dvs)

        @pl.when(n == nb)
        def _():
            store_rows(dk_ref, ck)
            store_rows(dv_ref, cv)

    cur = lambda n: jnp.minimum(n, nb - 1)
    prev = lambda n: jnp.clip(n - 1, 0, nb - 1)
    zb = lambda c, p: pl.BlockSpec((R, W), lambda hb, n: ((prev(n) if p else cur(n)), c // pp + hb))
    big_cur = pl.BlockSpec((R, W), lambda hb, n: (cur(n), hb))
    big_lag = pl.BlockSpec((R, W), lambda hb, n: (jnp.maximum(n - 1, 0), hb))
    buf = lambda rows, dt: pltpu.VMEM((G, rows, 128), dt)
    return pl.pallas_call(
        body, grid=(N_PAIR // pp, nb + 1),
        in_specs=[zb(ZQ, False), zb(ZK, False), zb(ZK, True), zb(ZV, False), zb(ZV, True),
                  big_cur, big_cur, pl.BlockSpec((R, 128), lambda hb, n: (cur(n), 0))],
        out_specs=[big_cur, big_lag, big_lag],
        out_shape=[_sds((S, D_ATTN), F32)] * 3,
        scratch_shapes=[buf(BLK, BF16), buf(2 * BLK, BF16), buf(2 * BLK, BF16), buf(BLK, F32), buf(BLK, F32),
                        pltpu.VMEM((d, BLK, 128), F32), buf(BLK, F32), buf(BLK, F32), buf(BLK, F32), buf(BLK, F32), buf(BLK, F32)],
        name=name, compiler_params=_cp(2),
    )(z, z, z, z, z, dycat, y, lse_tot)


ZC_GB, ZC_GC, ZC_CI, ZC_PI = 6, 7, 8, 9


def _pool_counts(i, tb):
    pos = lax.broadcasted_iota(jnp.int32, (tb, D_POOL), 0) + i * tb + 1
    grp = lax.broadcasted_iota(jnp.int32, (tb, D_POOL), 1) // POOL_GROUP
    win = jnp.where(grp == 0, POOL_WINDOWS[0], jnp.where(grp == 1, POOL_WINDOWS[1],
                    jnp.where(grp == 2, POOL_WINDOWS[2], POOL_WINDOWS[3])))
    return jnp.minimum(pos, win).astype(F32), grp


def _select_group(stages, grp):
    return jnp.where(grp == 0, stages[0], jnp.where(grp == 1, stages[1], jnp.where(grp == 2, stages[2], stages[3])))


def _causal_window_sums(x2):
    s1 = x2 + pltpu.roll(x2, 1, 0)
    s2 = s1 + pltpu.roll(s1, 2, 0)
    s3 = s2 + pltpu.roll(s2, 4, 0)
    s4 = s3 + pltpu.roll(s3, 8, 0)
    return [s1, s2, s3, s4]


def _anticausal_window_sums(x2):
    n = x2.shape[0]
    s1 = x2 + pltpu.roll(x2, n - 1, 0)
    s2 = s1 + pltpu.roll(s1, n - 2, 0)
    s3 = s2 + pltpu.roll(s2, n - 4, 0)
    s4 = s3 + pltpu.roll(s3, n - 8, 0)
    return [s1, s2, s3, s4]


def _pooled(p_prev, p_cur, i, tb):
    x2 = jnp.concatenate([jnp.where(i > 0, p_prev, 0.0), p_cur], axis=0)
    count, grp = _pool_counts(i, tb)
    win_sum = _select_group([s[tb:] for s in _causal_window_sums(x2)], grp)
    return win_sum / count - p_cur, count, grp


def _pool_mm(v, pw_ref, nt):
    outs = []
    for gi in range(len(POOL_WINDOWS)):
        sl = slice(gi * POOL_GROUP, (gi + 1) * POOL_GROUP)
        outs.append(_dot_nt(v[:, sl], pw_ref[gi]) if nt else _dot(v[:, sl], pw_ref[gi]))
    return jnp.concatenate(outs, axis=1)


def conv_pool_fwd(z, conv_w, pool_w, pool_scale, name):
    S = z.shape[0]
    tb = min(256, S)

    def body(gb_ref, gc_ref, gcp_ref, ci_ref, cip_ref, pi_ref, pip_ref, cw_ref, pw_ref, ps_ref, yc_ref, yp_ref):
        i = pl.program_id(0)
        u2 = jnp.concatenate([jnp.where(i > 0, gcp_ref[...] * cip_ref[...], 0.0), gc_ref[...] * ci_ref[...]], axis=0)
        conv = cw_ref[0:1, :] * pltpu.roll(u2, 2, 0) + cw_ref[1:2, :] * pltpu.roll(u2, 1, 0) + cw_ref[2:3, :] * u2
        yc_ref[...] = (gb_ref[...] * conv[tb:]).astype(BF16)
        pooled, _, _ = _pooled(pip_ref[...], pi_ref[...], i, tb)
        yp_ref[...] = (_pool_mm(pooled.astype(BF16), pw_ref, False) * ps_ref[...]).astype(BF16)

    cur = lambda c: pl.BlockSpec((tb, 512), lambda i: (i, c))
    prev = lambda c: pl.BlockSpec((tb, 512), lambda i: (jnp.maximum(i - 1, 0), c))
    full = lambda a: pl.BlockSpec(a.shape, lambda i: (0,) * a.ndim)
    out = pl.BlockSpec((tb, 512), lambda i: (i, 0))
    return pl.pallas_call(
        body, grid=(S // tb,),
        in_specs=[cur(ZC_GB), cur(ZC_GC), prev(ZC_GC), cur(ZC_CI), prev(ZC_CI), cur(ZC_PI), prev(ZC_PI),
                  full(conv_w), full(pool_w), full(pool_scale)],
        out_specs=[out, out], out_shape=[_sds((S, 512), BF16)] * 2, name=name, compiler_params=_cp(1),
    )(z, z, z, z, z, z, z, conv_w, pool_w, pool_scale)


def conv_pool_bwd(z, dycat, conv_w, pool_w, pool_scale, name):
    S = z.shape[0]
    tb = min(256, S)
    nblk = S // tb
    ng = len(POOL_WINDOWS)

    def body(gb_ref, gbn_ref, gc_ref, gcp_ref, ci_ref, cip_ref, pi_ref, pip_ref, dyc_ref, dycn_ref, dyp_ref, dypn_ref,
             cw_ref, pw_ref, ps_ref, dz_ref, dcw_ref, dpw_ref, dps_ref):
        i = pl.program_id(0)
        last = i == nblk - 1

        @pl.when(i == 0)
        def _():
            dcw_ref[...] = jnp.zeros_like(dcw_ref)
            dpw_ref[...] = jnp.zeros_like(dpw_ref)
            dps_ref[...] = jnp.zeros_like(dps_ref)

        gc, ci = gc_ref[...], ci_ref[...]
        u2 = jnp.concatenate([jnp.where(i > 0, gcp_ref[...] * cip_ref[...], 0.0), gc * ci], axis=0)
        um2, um1, u0 = pltpu.roll(u2, 2, 0)[tb:], pltpu.roll(u2, 1, 0)[tb:], u2[tb:]
        conv = cw_ref[0:1, :] * um2 + cw_ref[1:2, :] * um1 + cw_ref[2:3, :] * u0
        dyc = dyc_ref[...]
        dconv = dyc * gb_ref[...]
        dc2 = jnp.concatenate([dconv, jnp.where(last, 0.0, dycn_ref[...] * gbn_ref[...])], axis=0)
        du = (cw_ref[2:3, :] * dconv + cw_ref[1:2, :] * pltpu.roll(dc2, 2 * tb - 1, 0)[:tb]
              + cw_ref[0:1, :] * pltpu.roll(dc2, 2 * tb - 2, 0)[:tb])
        dz_ref[:, 0:512] = (dyc * conv).astype(BF16)
        dz_ref[:, 512:1024] = (du * ci).astype(BF16)
        dz_ref[:, 1024:1536] = (du * gc).astype(BF16)
        dcw_ref[0:1, :] += jnp.sum(dconv * um2, axis=0, keepdims=True)
        dcw_ref[1:2, :] += jnp.sum(dconv * um1, axis=0, keepdims=True)
        dcw_ref[2:3, :] += jnp.sum(dconv * u0, axis=0, keepdims=True)

        pooled, count, grp = _pooled(pip_ref[...], pi_ref[...], i, tb)
        pooled_b = pooled.astype(BF16)
        t = _pool_mm(pooled_b, pw_ref, False)
        dyp, ps = dyp_ref[...], ps_ref[...]
        dps_ref[...] += jnp.sum(dyp * t, axis=0, keepdims=True)
        dt_b = (dyp * ps).astype(BF16)
        for gi in range(ng):
            sl = slice(gi * POOL_GROUP, (gi + 1) * POOL_GROUP)
            dpw_ref[gi] += _dot_tn(pooled_b[:, sl], dt_b[:, sl])
        dpooled = _pool_mm(dt_b, pw_ref, True)
        dpooled_n = _pool_mm((dypn_ref[...] * ps).astype(BF16), pw_ref, True)
        count_n, _ = _pool_counts(i + 1, tb)
        dq2 = jnp.concatenate([dpooled / count, jnp.where(last, 0.0, dpooled_n / count_n)], axis=0)
        lead = _select_group([s[:tb] for s in _anticausal_window_sums(dq2)], grp)
        dz_ref[:, 1536:2048] = (lead - dpooled).astype(BF16)

    cur = lambda c: pl.BlockSpec((tb, 512), lambda i: (i, c))
    prev = lambda c: pl.BlockSpec((tb, 512), lambda i: (jnp.maximum(i - 1, 0), c))
    nxt = lambda c: pl.BlockSpec((tb, 512), lambda i: (jnp.minimum(i + 1, nblk - 1), c))
    full = lambda a: pl.BlockSpec(a.shape, lambda i: (0,) * a.ndim)
    yc_c, yp_c = D_ATTN // 512, D_ATTN // 512 + 1
    return pl.pallas_call(
        body, grid=(nblk,),
        in_specs=[cur(ZC_GB), nxt(ZC_GB), cur(ZC_GC), prev(ZC_GC), cur(ZC_CI), prev(ZC_CI), cur(ZC_PI), prev(ZC_PI),
                  cur(yc_c), nxt(yc_c), cur(yp_c), nxt(yp_c), full(conv_w), full(pool_w), full(pool_scale)],
        out_specs=[pl.BlockSpec((tb, 2048), lambda i: (i, 0)), pl.BlockSpec((3, 512), lambda i: (0, 0)),
                   pl.BlockSpec((ng, POOL_GROUP, POOL_GROUP), lambda i: (0, 0, 0)), pl.BlockSpec((1, 512), lambda i: (0, 0))],
        out_shape=[_sds((S, 2048), BF16), _sds((3, 512), F32), _sds((ng, POOL_GROUP, POOL_GROUP), F32), _sds((1, 512), F32)],
        name=name, compiler_params=_cp(1),
    )(z, z, z, z, z, z, z, z, dycat, dycat, dycat, dycat, conv_w, pool_w, pool_scale)


def assemble_dz(parts_q, parts_k, parts_v, dcp, name):
    S = dcp.shape[0]
    tm = _row_tile(S)
    npat = len(parts_q)

    def body(*refs):
        dz_ref = refs[-1]
        dcp_ref = refs[-2]
        for j in range(3):
            acc = refs[j * npat][...]
            for p in range(1, npat):
                acc = acc + refs[j * npat + p][...]
            dz_ref[:, j * D_ATTN:(j + 1) * D_ATTN] = acc.astype(BF16)
        dz_ref[:, 3 * D_ATTN:] = dcp_ref[...]

    big = pl.BlockSpec((tm, D_ATTN), lambda i: (i, 0))
    return pl.pallas_call(
        body, grid=(S // tm,), in_specs=[big] * (3 * npat) + [pl.BlockSpec((tm, D_IN - 3 * D_ATTN), lambda i: (i, 0))],
        out_specs=pl.BlockSpec((tm, D_IN), lambda i: (i, 0)), out_shape=_sds((S, D_IN), BF16),
        name=name, compiler_params=_cp(1),
    )(*parts_q, *parts_k, *parts_v, dcp)


def adamw(w, g, m, v, name):
    shape = w.shape
    cols = shape[-1]
    rows = w.size // cols
    tr = rows
    for cand in (256, 128, 64, 32, 16, 8):
        if rows % cand == 0:
            tr = cand
            break
    c1 = 1.0 - ADAM_B1 ** ADAM_STEP
    c2 = 1.0 - ADAM_B2 ** ADAM_STEP

    def body(w_ref, g_ref, m_ref, v_ref, d_ref, mo_ref, vo_ref):
        gv = g_ref[...]
        mn = ADAM_B1 * m_ref[...] + (1.0 - ADAM_B1) * gv
        vn = ADAM_B2 * v_ref[...] + (1.0 - ADAM_B2) * (gv * gv)
        d_ref[...] = -ADAM_LR * ((mn / c1) / (jnp.sqrt(vn / c2) + ADAM_EPS) + ADAM_WD * w_ref[...])
        mo_ref[...] = mn
        vo_ref[...] = vn

    blk = pl.BlockSpec((tr, cols), lambda i: (i, 0))
    outs = pl.pallas_call(
        body, grid=(rows // tr,), in_specs=[blk] * 4, out_specs=[blk] * 3,
        out_shape=[_sds((rows, cols), F32)] * 3, name=name, compiler_params=_cp(1),
    )(*(a.reshape(rows, cols) for a in (w, g, m, v)))
    return tuple(o.reshape(shape) for o in outs)


ANY = pl.BlockSpec(memory_space=pl.ANY)


def _place():
    x, y, c = lax.axis_index("x"), lax.axis_index("y"), lax.axis_index("c")
    chips = [(1 - x, y), (x, 1 - y), (1 - x, 1 - y)]
    return x, y, c, chips


def gather_weights(ws):
    nw = len(ws)
    split = [w.shape[1] % 32 == 0 for w in ws]

    def body(*refs):
        ins, outs = refs[:nw], refs[nw:2 * nw]
        send, recv = refs[2 * nw:]
        x, y, c, chips = _place()
        me, sib = 2 * x + y, (x, y, 1 - c)

        def half(j, k, hc):
            if not split[j]:
                return outs[j].at[k]
            ha = ws[j].shape[1] // 2
            return outs[j].at[k, :, pl.ds(hc * ha, ha), :]

        def rcopy(j, s, src, dst, to):
            return pltpu.make_async_remote_copy(src_ref=src, dst_ref=dst, send_sem=send.at[j, s], recv_sem=recv.at[j, s],
                                                device_id=to, device_id_type=MESH)

        first = [rcopy(j, 6, ins[j], outs[j].at[me], sib) for j in range(nw)]
        for j in range(nw):
            ha = ws[j].shape[1] // 2
            mine = ins[j].at[:, pl.ds(c * ha, ha), :] if split[j] else ins[j]
            for s, (px, py) in enumerate(chips):
                first.append(rcopy(j, s, mine, half(j, me, c), (px, py, c)))
        for cp in first:
            cp.start()
        passed = []
        for j in range(nw):
            for s, (px, py) in enumerate(chips):
                blk = half(j, 2 * px + py, c)
                rcopy(j, s, blk, blk, (px, py, c)).wait_recv()
                if split[j]:
                    fwd = rcopy(j, 3 + s, blk, blk, sib)
                    fwd.start()
                    passed.append(fwd)
        for j in range(nw):
            for s, (px, py) in enumerate(chips):
                if split[j]:
                    blk = half(j, 2 * px + py, 1 - c)
                    rcopy(j, 3 + s, blk, blk, sib).wait_recv()
        for j in range(nw):
            rcopy(j, 6, ins[j], outs[j].at[me], sib).wait_recv()
        for cp in first + passed:
            cp.wait_send()

    return pl.pallas_call(
        body, in_specs=[ANY] * nw, out_specs=[ANY] * nw,
        out_shape=[_sds((NSH,) + w.shape, w.dtype) for w in ws],
        scratch_shapes=[pltpu.SemaphoreType.DMA((nw, 7)), pltpu.SemaphoreType.DMA((nw, 7))],
        name="gather_weights",
    )(*ws)


def sibling_send_half(gs):
    ng = len(gs)

    def body(*refs):
        ins, outs = refs[:ng], refs[ng:2 * ng]
        send, recv = refs[2 * ng:]
        x, y, c, _ = _place()
        cps = []
        for j in range(ng):
            ha = gs[j].shape[1] // 2
            cps.append(pltpu.make_async_remote_copy(
                src_ref=ins[j].at[:, pl.ds((1 - c) * ha, ha), :], dst_ref=outs[j], send_sem=send.at[j], recv_sem=recv.at[j],
                device_id=(x, y, 1 - c), device_id_type=MESH))
        for cp in cps:
            cp.start()
        for cp in cps:
            cp.wait()

    return pl.pallas_call(
        body, in_specs=[ANY] * ng, out_specs=[ANY] * ng,
        out_shape=[_sds((NSH, g.shape[1] // 2, g.shape[2]), g.dtype) for g in gs],
        scratch_shapes=[pltpu.SemaphoreType.DMA((ng,)), pltpu.SemaphoreType.DMA((ng,))],
        name="sibling_send_half",
    )(*gs)


def pair_sum(g, got, half_index, name):
    _, A, B = g.shape
    ha = A // 2
    tr = ha
    for cand in (512, 256, 128, 64):
        if ha % cand == 0:
            tr = cand
            break
    nt = ha // tr

    def body(c_ref, g_ref, r_ref, o_ref):
        del c_ref
        o_ref[...] = (g_ref[...] + r_ref[...]).astype(BF16)

    return pl.pallas_call(
        body,
        grid_spec=pltpu.PrefetchScalarGridSpec(
            num_scalar_prefetch=1, grid=(NSH, nt),
            in_specs=[pl.BlockSpec((None, tr, B), lambda k, t, c: (k, c[0] * nt + t, 0)),
                      pl.BlockSpec((None, tr, B), lambda k, t, c: (k, t, 0))],
            out_specs=pl.BlockSpec((None, tr, B), lambda k, t, c: (k, t, 0))),
        out_shape=_sds((NSH, ha, B), BF16), name=name, compiler_params=_cp(2),
    )(half_index, g, got)


def chip_exchange(ps):
    n = len(ps)

    def body(*refs):
        ins, outs = refs[:n], refs[n:2 * n]
        send, recv = refs[2 * n:]
        x, y, c, chips = _place()
        me = 2 * x + y
        cps = []
        for j in range(n):
            for s, (px, py) in enumerate(chips):
                cps.append(pltpu.make_async_remote_copy(
                    src_ref=ins[j].at[2 * px + py], dst_ref=outs[j].at[me], send_sem=send.at[j, s], recv_sem=recv.at[j, s],
                    device_id=(px, py, c), device_id_type=MESH))
        for cp in cps:
            cp.start()
        for cp in cps:
            cp.wait()

    return pl.pallas_call(
        body, in_specs=[ANY] * n, out_specs=[ANY] * n, out_shape=[_sds(p.shape, p.dtype) for p in ps],
        scratch_shapes=[pltpu.SemaphoreType.DMA((n, 3)), pltpu.SemaphoreType.DMA((n, 3))],
        name="chip_exchange",
    )(*ps)


def _half_tile(ha):
    for cand in (512, 256, 128, 64):
        if ha % cand == 0:
            return cand
    return ha


def chip_sum(partials, slots, chip_index, name):
    depth = len(partials)
    _, ha, B = partials[0].shape
    tr = _half_tile(ha)
    nt = ha // tr

    def body(me_ref, *refs):
        p_refs, s_refs, o_ref = refs[:depth], refs[depth:depth + depth * NSH], refs[depth + depth * NSH]
        l = pl.program_id(0)
        for ll in range(depth):
            @pl.when(l == ll)
            def _(ll=ll):
                own = p_refs[ll][...].astype(F32)
                acc = jnp.where(me_ref[0] == 0, own, s_refs[ll * NSH][...].astype(F32))
                for k in range(1, NSH):
                    acc = acc + jnp.where(me_ref[0] == k, own, s_refs[ll * NSH + k][...].astype(F32))
                o_ref[...] = acc

    def frozen(ll):
        return lambda l, t: jnp.where(l == ll, t, jnp.where(l < ll, 0, nt - 1))

    def slot(k):
        return lambda me: jnp.where(me[0] == k, (k + 1) % NSH, k)

    in_specs = [pl.BlockSpec((None, tr, B), lambda l, t, me, ll=ll: (me[0], frozen(ll)(l, t), 0)) for ll in range(depth)]
    in_specs += [pl.BlockSpec((None, tr, B), lambda l, t, me, ll=ll, k=k: (slot(k)(me), frozen(ll)(l, t), 0))
                 for ll in range(depth) for k in range(NSH)]
    return pl.pallas_call(
        body,
        grid_spec=pltpu.PrefetchScalarGridSpec(
            num_scalar_prefetch=1, grid=(depth, nt), in_specs=in_specs,
            out_specs=pl.BlockSpec((None, tr, B), lambda l, t, me: (l, t, 0))),
        out_shape=_sds((depth, ha, B), F32), name=name, compiler_params=_cp(2),
    )(chip_index, *partials, *[s for s in slots for _ in range(NSH)])


def sibling_swap(mine):
    n = len(mine)

    def body(*refs):
        ins, outs = refs[:n], refs[n:2 * n]
        send, recv = refs[2 * n:]
        x, y, c, _ = _place()
        cps = [pltpu.make_async_remote_copy(src_ref=ins[j], dst_ref=outs[j], send_sem=send.at[j], recv_sem=recv.at[j],
                                            device_id=(x, y, 1 - c), device_id_type=MESH) for j in range(n)]
        for cp in cps:
            cp.start()
        for cp in cps:
            cp.wait()

    return pl.pallas_call(
        body, in_specs=[ANY] * n, out_specs=[ANY] * n, out_shape=[_sds(m.shape, m.dtype) for m in mine],
        scratch_shapes=[pltpu.SemaphoreType.DMA((n,)), pltpu.SemaphoreType.DMA((n,))], name="sibling_swap",
    )(*mine)


def adamw_halves(w, mine, theirs, m, v, core_index, name):
    depth, A, B = w.shape
    ha = A // 2
    tr = _half_tile(ha)
    while tr * B * 4 > 2 ** 20 and tr % 16 == 0:
        tr //= 2
    nt = ha // tr
    c1 = 1.0 - ADAM_B1 ** ADAM_STEP
    c2 = 1.0 - ADAM_B2 ** ADAM_STEP

    def body(c_ref, w_ref, a_ref, b_ref, m_ref, v_ref, g_ref, d_ref, mo_ref, vo_ref):
        gv = jnp.where(pl.program_id(1) == c_ref[0], a_ref[...], b_ref[...])
        mn = ADAM_B1 * m_ref[...] + (1.0 - ADAM_B1) * gv
        vn = ADAM_B2 * v_ref[...] + (1.0 - ADAM_B2) * (gv * gv)
        g_ref[...] = gv
        d_ref[...] = -ADAM_LR * ((mn / c1) / (jnp.sqrt(vn / c2) + ADAM_EPS) + ADAM_WD * w_ref[...])
        mo_ref[...] = mn
        vo_ref[...] = vn

    full = pl.BlockSpec((None, tr, B), lambda l, h, t, c: (l, h * nt + t, 0))
    a_spec = pl.BlockSpec((None, tr, B), lambda l, h, t, c: (l, jnp.where(h == c[0], t, 0), 0))
    b_spec = pl.BlockSpec((None, tr, B), lambda l, h, t, c: (l, jnp.where(h == c[0], 0, t), 0))
    return pl.pallas_call(
        body,
        grid_spec=pltpu.PrefetchScalarGridSpec(
            num_scalar_prefetch=1, grid=(depth, 2, nt), in_specs=[full, a_spec, b_spec, full, full], out_specs=[full] * 4),
        out_shape=[_sds(w.shape, F32)] * 4, name=name, compiler_params=_cp(3),
    )(core_index, w, mine, theirs, m, v)


def small_all_reduce(v):
    R = v.shape[0]

    def body(v_ref, o_ref, slots, send, recv):
        x, y, c, _ = _place()
        me = 4 * x + 2 * y + c
        slots[me] = v_ref[...]
        cps = []
        for m in range(1, 8):
            mx, my, mc = (m >> 2) & 1, (m >> 1) & 1, m & 1
            cps.append(pltpu.make_async_remote_copy(
                src_ref=v_ref, dst_ref=slots.at[me], send_sem=send.at[m - 1], recv_sem=recv.at[m - 1],
                device_id=(x ^ mx, y ^ my, c ^ mc), device_id_type=MESH))
        for cp in cps:
            cp.start()
        for cp in cps:
            cp.wait()
        acc = slots[0]
        for d in range(1, 8):
            acc = acc + slots[d]
        o_ref[...] = acc

    vm = pl.BlockSpec(memory_space=pltpu.VMEM)
    return pl.pallas_call(
        body, in_specs=[vm], out_specs=vm, out_shape=_sds((R, 128), F32),
        scratch_shapes=[pltpu.VMEM((8, R, 128), F32), pltpu.SemaphoreType.DMA((7,)), pltpu.SemaphoreType.DMA((7,))],
        name="small_all_reduce",
    )(v)


BIG = ("ffn1_w_gate", "ffn1_w_up", "ffn1_w_down", "w_in", "w_out", "ffn2_w_gate", "ffn2_w_up", "ffn2_w_down")
SMALL = ("ffn1_norm", "mix_norm", "conv_w", "pool_w", "pool_scale", "ffn2_norm", "final_norm")
WEIGHTS = ("ffn1_norm", "ffn1_w_gate", "ffn1_w_up", "ffn1_w_down", "mix_norm", "w_in", "conv_w", "pool_w", "pool_scale",
           "w_out", "ffn2_norm", "ffn2_w_gate", "ffn2_w_up", "ffn2_w_down", "final_norm")


def _tile_for(n, cap=1024):
    best = 128
    for t in range(128, min(n, cap) + 1, 128):
        if n % t == 0:
            best = t
    return n if n <= cap else best


def _ffn_forward(x, norm, wg, wu, wd, l, tag):
    h = rms_fwd(x, norm[l], f"{tag}_norm")
    g, u, a = ffn_up(h, wg, wu, l, f"{tag}_up")
    out = resid_mm(a, wd, l, x, 0.5, f"{tag}_down")
    return out, (x, h, g, u, a)


def _ffn_backward(saved, dx, dxb, norm, wg, wu, wd, l, tag):
    x, h, g, u, a = saved
    D, FS = wg.shape[2], wg.shape[3]
    dg, du = ffn_bwd_act(dxb, wd, l, g, u, f"{tag}_bwd_act")
    g_wd = wgrad(a, dxb, True, False, 0.5, FS, _tile_for(D), f"{tag}_dwd")
    g_wg = wgrad(h, dg, False, True, 1.0, _tile_for(D), FS, f"{tag}_dwg")
    g_wu = wgrad(h, du, False, True, 1.0, _tile_for(D), FS, f"{tag}_dwu")
    dx, dxb, dnorm = bwd_dh([dg, du], [wg, wu], l, x, norm[l], dx, f"{tag}_bwd_dh")
    return dx, dxb, dnorm, g_wg, g_wu, g_wd


def kernel(x, ffn1_norm, ffn1_w_gate, ffn1_w_up, ffn1_w_down, mix_norm, w_in, conv_w, pool_w, pool_scale, w_out, ffn2_norm, ffn2_w_gate, ffn2_w_up, ffn2_w_down, final_norm, loss_target, m_ffn1_norm, m_ffn1_w_gate, m_ffn1_w_up, m_ffn1_w_down, m_mix_norm, m_w_in, m_conv_w, m_pool_w, m_pool_scale, m_w_out, m_ffn2_norm, m_ffn2_w_gate, m_ffn2_w_up, m_ffn2_w_down, m_final_norm, v_ffn1_norm, v_ffn1_w_gate, v_ffn1_w_up, v_ffn1_w_down, v_mix_norm, v_w_in, v_conv_w, v_pool_w, v_pool_scale, v_w_out, v_ffn2_norm, v_ffn2_w_gate, v_ffn2_w_up, v_ffn2_w_down, v_final_norm):
    given = dict(locals())
    W = {n: given[n] for n in WEIGHTS}
    M = {n: given["m_" + n] for n in WEIGHTS}
    V = {n: given["v_" + n] for n in WEIGHTS}
    depth = ffn1_norm.shape[0]
    D = x.shape[-1]
    xs = x[0]
    my_chip = 2 * lax.axis_index("x") + lax.axis_index("y")
    my_core = lax.axis_index("c")

    gathered = gather_weights([W[n].astype(BF16) for n in BIG] + [conv_w])
    G = dict(zip(BIG, gathered[:-1]))
    conv_full = jnp.moveaxis(gathered[-1], 0, 2).reshape(depth, 3, D_CONV)
    pool_b = pool_w.astype(BF16)

    saved = []
    cur = xs
    for l in range(depth):
        cur, s1 = _ffn_forward(cur, ffn1_norm, G["ffn1_w_gate"], G["ffn1_w_up"], G["ffn1_w_down"], l, "ffn1")
        hm = rms_fwd(cur, mix_norm[l], "mix_norm")
        z = col_mm(hm, G["w_in"], l, "mix_in")
        pats = [attn_fwd(z, d, f"attn_fwd_d{d}") for d in DILATIONS]
        y, yb, lse_tot = attn_merge([p[0] for p in pats], [p[1] for p in pats], "attn_merge")
        y_conv, y_pool = conv_pool_fwd(z, conv_full[l], pool_b[l], pool_scale[l].reshape(1, D_POOL), "conv_pool_fwd")
        ycat = jnp.concatenate([yb, y_conv, y_pool], axis=1)
        x1 = cur
        cur = resid_mm(ycat, G["w_out"], l, x1, 1.0, "mix_out")
        sm = (x1, hm, z, y, lse_tot, ycat)
        cur, s2 = _ffn_forward(cur, ffn2_norm, G["ffn2_w_gate"], G["ffn2_w_up"], G["ffn2_w_down"], l, "ffn2")
        saved.append((s1, sm, s2))

    loss11, dx, dxb, d_final = final_loss(cur, final_norm, loss_target[0], "final_loss")

    big_grads = {n: [None] * depth for n in BIG}
    small_grads = {n: [None] * depth for n in SMALL if n != "final_norm"}
    for l in reversed(range(depth)):
        s1, sm, s2 = saved[l]
        dx, dxb, dn2, gg, gu, gd = _ffn_backward(s2, dx, dxb, ffn2_norm, G["ffn2_w_gate"], G["ffn2_w_up"], G["ffn2_w_down"], l, "ffn2")
        big_grads["ffn2_w_gate"][l], big_grads["ffn2_w_up"][l], big_grads["ffn2_w_down"][l] = gg, gu, gd
        small_grads["ffn2_norm"][l] = dn2

        x1, hm, z, y, lse_tot, ycat = sm
        dycat = nt_col_mm(dxb, G["w_out"], l, "mix_out_bwd")
        big_grads["w_out"][l] = wgrad(ycat, dxb, True, False, 1.0, D_MIX // NSH, _tile_for(D, 2048), "mix_dwout")
        parts = [attn_bwd(z, dycat, y, lse_tot, d, f"attn_bwd_d{d}") for d in DILATIONS]
        dcp, dcw, dpw, dps = conv_pool_bwd(z, dycat, conv_full[l], pool_b[l], pool_scale[l].reshape(1, D_POOL), "conv_pool_bwd")
        dz = assemble_dz([p[0] for p in parts], [p[1] for p in parts], [p[2] for p in parts], dcp, "assemble_dz")
        big_grads["w_in"][l] = wgrad(hm, dz, False, True, 1.0, _tile_for(D), D_IN // NSH, "mix_dwin")
        dx, dxb, dnm = bwd_dh([dz], [G["w_in"]], l, x1, mix_norm[l], dx, "mix_bwd_dh")
        small_grads["mix_norm"][l], small_grads["conv_w"][l] = dnm, dcw
        small_grads["pool_w"][l], small_grads["pool_scale"][l] = dpw, dps

        dx, dxb, dn1, gg, gu, gd = _ffn_backward(s1, dx, dxb, ffn1_norm, G["ffn1_w_gate"], G["ffn1_w_up"], G["ffn1_w_down"], l, "ffn1")
        big_grads["ffn1_w_gate"][l], big_grads["ffn1_w_up"][l], big_grads["ffn1_w_down"][l] = gg, gu, gd
        small_grads["ffn1_norm"][l] = dn1

    flat = [big_grads[n][l] for n in BIG for l in range(depth)]
    got = sibling_send_half(flat)
    half_index = my_core.astype(jnp.int32).reshape(1)
    partial = [pair_sum(g, r, half_index, "pair_sum") for g, r in zip(flat, got)]
    slots = chip_exchange(partial)
    chip_index = my_chip.astype(jnp.int32).reshape(1)
    mine = [chip_sum(partial[i * depth:(i + 1) * depth], slots[i * depth:(i + 1) * depth], chip_index, "chip_sum")
            for i in range(len(BIG))]
    theirs = sibling_swap(mine)
    grads, delta, new_m, new_v = {}, {}, {}, {}
    for n, a, b in zip(BIG, mine, theirs):
        grads[n], delta[n], new_m[n], new_v[n] = adamw_halves(W[n], a, b, M[n], V[n], half_index, "adamw")

    small_full = {n: jnp.stack([a.reshape(W[n].shape[1:] if n != "conv_w" else (3, D_CONV)) for a in small_grads[n]])
                  for n in small_grads}
    small_full["final_norm"] = d_final.reshape(D)
    order = list(SMALL)
    packed = jnp.concatenate([small_full[n].reshape(-1) for n in order])
    pad = (-packed.shape[0]) % (8 * 128)
    packed = jnp.pad(packed, (0, pad)).reshape(-1, 128)
    summed = small_all_reduce(packed).reshape(-1)
    off = 0
    for n in order:
        size = small_full[n].size
        grads[n] = summed[off:off + size].reshape(small_full[n].shape)
        off += size
    grads["conv_w"] = lax.dynamic_slice_in_dim(grads["conv_w"], my_chip * (D_CONV // NSH), D_CONV // NSH, axis=2)

    def pack(src):
        flat_ = jnp.concatenate([src[n].reshape(-1) for n in order])
        return jnp.pad(flat_, (0, (-flat_.shape[0]) % (8 * 128))).reshape(-1, 128)

    ds, ms, vs = adamw(pack(W), pack(grads), pack(M), pack(V), "adamw_small")
    off = 0
    for n in order:
        size = W[n].size
        for dst, src in ((delta, ds), (new_m, ms), (new_v, vs)):
            dst[n] = src.reshape(-1)[off:off + size].reshape(W[n].shape)
        off += size

    loss = lax.psum(loss11[0, 0], ("x", "y", "c"))
    return (loss, dx.reshape(x.shape), *[grads[n] for n in WEIGHTS], *[delta[n] for n in WEIGHTS],
            *[new_m[n] for n in WEIGHTS], *[new_v[n] for n in WEIGHTS])
```

```python
import functools

import jax
import jax.numpy as jnp
from jax import lax
from jax.experimental import pallas as pl
from jax.experimental.pallas import tpu as pltpu

F32 = jnp.float32
BF16 = jnp.bfloat16
MESH = pl.DeviceIdType.MESH

RMS_EPS = 1e-6
NEG_INF = -1e30
HEAD_DIM = 64
BLK = 128
SPAN = 128
DILATIONS = (1, 4, 16)
D_ATTN = 1024
D_CONV = 512
D_POOL = 512
POOL_WINDOWS = (2, 4, 8, 16)
POOL_GROUP = 128
D_IN = 3 * D_ATTN + 3 * D_CONV + D_POOL
D_MIX = D_ATTN + D_CONV + D_POOL
N_PAIR = D_ATTN // 128
ATTN_SCALE = HEAD_DIM ** -0.5
NSH = 4
ADAM_LR, ADAM_B1, ADAM_B2, ADAM_EPS, ADAM_WD, ADAM_STEP = 0.001, 0.9, 0.999, 1e-08, 0.01, 10

VMEM_LIMIT = 56 * 2 ** 20


def _cp(n_axes):
    return pltpu.CompilerParams(dimension_semantics=("arbitrary",) * n_axes, vmem_limit_bytes=VMEM_LIMIT)


def _sds(shape, dtype):
    return jax.ShapeDtypeStruct(shape, dtype)


def _dot(a, b):
    return jnp.dot(a, b, preferred_element_type=F32)


def _dot_nt(a, b):
    return lax.dot_general(a, b, (((1,), (1,)), ((), ())), preferred_element_type=F32)


def _dot_tn(a, b):
    return lax.dot_general(a, b, (((0,), (0,)), ((), ())), preferred_element_type=F32)


def _row_tile(s):
    return min(512, s)


def rms_fwd(x, g, name):
    S, D = x.shape
    tm = _row_tile(S)

    def body(x_ref, g_ref, h_ref):
        xv = x_ref[...]
        r = lax.rsqrt(jnp.mean(xv * xv, axis=-1, keepdims=True) + RMS_EPS)
        h_ref[...] = (xv * r * g_ref[...]).astype(BF16)

    return pl.pallas_call(
        body, grid=(S // tm,),
        in_specs=[pl.BlockSpec((tm, D), lambda i: (i, 0)), pl.BlockSpec((1, D), lambda i: (0, 0))],
        out_specs=pl.BlockSpec((tm, D), lambda i: (i, 0)),
        out_shape=_sds((S, D), BF16), name=name, compiler_params=_cp(1),
    )(x, g.reshape(1, D))


def _rms_bwd_tile(xv, gv, dh):
    r = lax.rsqrt(jnp.mean(xv * xv, axis=-1, keepdims=True) + RMS_EPS)
    xhat = xv * r
    dg = jnp.sum(dh * xhat, axis=0, keepdims=True)
    dxhat = dh * gv
    dx = r * (dxhat - xhat * jnp.mean(dxhat * xhat, axis=-1, keepdims=True))
    return dx, dg


def final_loss(x, g, target, name):
    S, D = x.shape
    tm = _row_tile(S)

    def body(x_ref, g_ref, t_ref, loss_ref, dx_ref, dxb_ref, dg_ref):
        i = pl.program_id(0)
        xv, gv = x_ref[...], g_ref[...]
        r = lax.rsqrt(jnp.mean(xv * xv, axis=-1, keepdims=True) + RMS_EPS)
        err = xv * r * gv - t_ref[...]
        part = 0.5 * jnp.sum(jnp.mean(err * err, axis=-1, keepdims=True), axis=0, keepdims=True)
        dx, dg = _rms_bwd_tile(xv, gv, err * (1.0 / D))

        @pl.when(i == 0)
        def _():
            loss_ref[...] = jnp.zeros_like(loss_ref)
            dg_ref[...] = jnp.zeros_like(dg_ref)

        loss_ref[...] += part
        dg_ref[...] += dg
        dx_ref[...] = dx
        dxb_ref[...] = dx.astype(BF16)

    row = pl.BlockSpec((tm, D), lambda i: (i, 0))
    vec = pl.BlockSpec((1, D), lambda i: (0, 0))
    return pl.pallas_call(
        body, grid=(S // tm,), in_specs=[row, vec, row],
        out_specs=[pl.BlockSpec((1, 1), lambda i: (0, 0)), row, row, vec],
        out_shape=[_sds((1, 1), F32), _sds((S, D), F32), _sds((S, D), BF16), _sds((1, D), F32)],
        name=name, compiler_params=_cp(1),
    )(x, g.reshape(1, D), target)


def _wspec(w, imap):
    _, a, b = w.shape
    return pl.BlockSpec((None, a, b), lambda *ids: (imap(*ids), 0, 0))


class _Comm:
    def __init__(self):
        self.ins, self.out_shapes, self.aliases, self.items = [], [], {}, []

    def _add(self, kind, operand, out_shape, alias):
        if alias:
            self.aliases[len(self.ins)] = len(self.out_shapes)
        self.items.append((kind, len(self.ins), len(self.out_shapes)))
        self.ins.append(operand)
        self.out_shapes.append(out_shape)

    def gather_first(self, src):
        self._add("first", src, _sds((NSH,) + src.shape, src.dtype), False)

    def gather_pass(self, buf):
        self._add("pass", buf, _sds(buf.shape, buf.dtype), True)

    def exchange(self, partial):
        self._add("xchg", partial, _sds(partial.shape, partial.dtype), False)

    def run(self, cins, couts, send, recv, start):
        x, y, c = lax.axis_index("x"), lax.axis_index("y"), lax.axis_index("c")
        chips = [(1 - x, y), (x, 1 - y), (1 - x, 1 - y)]
        me, sib = 2 * x + y, (x, y, 1 - c)
        for it, (kind, i, o) in enumerate(self.items):
            def rc(s, src, dst, to, it=it):
                return pltpu.make_async_remote_copy(src_ref=src, dst_ref=dst, send_sem=send.at[it, s], recv_sem=recv.at[it, s],
                                                    device_id=to, device_id_type=MESH)
            src, buf = cins[i], couts[o]
            if kind == "first":
                ha = src.shape[0] // 2
                rows = pl.ds(c * ha, ha)
                cps = [rc(s, src.at[rows], buf.at[me, rows], (px, py, c)) for s, (px, py) in enumerate(chips)]
                cps.append(rc(3, src, buf.at[me], sib))
                landing = [buf.at[2 * px + py, rows] for px, py in chips] + [buf.at[me]]
            elif kind == "pass":
                ha = buf.shape[1] // 2
                rows, other = pl.ds(c * ha, ha), pl.ds((1 - c) * ha, ha)
                cps = [rc(s, buf.at[2 * px + py, rows], buf.at[2 * px + py, rows], sib) for s, (px, py) in enumerate(chips)]
                landing = [buf.at[2 * px + py, other] for px, py in chips]
            else:
                cps = [rc(s, src.at[2 * px + py], buf.at[me], (px, py, c)) for s, (px, py) in enumerate(chips)]
                landing = [buf.at[2 * px + py] for px, py in chips]
            if start:
                for cp in cps:
                    cp.start()
            else:
                for s, dst in enumerate(landing):
                    rc(s, dst, dst, sib).wait_recv()
                for cp in cps:
                    cp.wait_send()


def _call(body, *, grid, in_specs, out_specs, out_shape, name, args, scratch=(), comm=None):
    multi = isinstance(out_shape, (list, tuple))
    oshape = list(out_shape) if multi else [out_shape]
    ospecs = list(out_specs) if multi else [out_specs]
    if comm is None or not comm.items:
        res = pl.pallas_call(body, grid=grid, in_specs=list(in_specs), out_specs=ospecs, out_shape=oshape,
                             scratch_shapes=list(scratch), name=name, compiler_params=_cp(len(grid)))(*args)
        return (list(res) if multi else res[0]), []
    nin, nout, nci, nco, nscr = len(in_specs), len(oshape), len(comm.ins), len(comm.out_shapes), len(scratch)

    def full(*refs):
        ins, cins = refs[:nin], refs[nin:nin + nci]
        outs, couts = refs[nin + nci:nin + nci + nout], refs[nin + nci + nout:nin + nci + nout + nco]
        scr = refs[nin + nci + nout + nco:nin + nci + nout + nco + nscr]
        send, recv = refs[-2:]
        ids = [pl.program_id(a) for a in range(len(grid))]
        first = functools.reduce(jnp.logical_and, [i == 0 for i in ids])
        last = functools.reduce(jnp.logical_and, [i == g - 1 for i, g in zip(ids, grid)])

        @pl.when(first)
        def _():
            comm.run(cins, couts, send, recv, True)

        body(*ins, *outs, *scr)

        @pl.when(last)
        def _():
            comm.run(cins, couts, send, recv, False)

    sems = pltpu.SemaphoreType.DMA((len(comm.items), 4))
    res = pl.pallas_call(
        full, grid=grid, in_specs=list(in_specs) + [ANY] * nci, out_specs=ospecs + [ANY] * nco,
        out_shape=oshape + comm.out_shapes, scratch_shapes=list(scratch) + [sems, sems],
        input_output_aliases={nin + i: nout + o for i, o in comm.aliases.items()},
        name=name, compiler_params=_cp(len(grid)),
    )(*args, *comm.ins)
    main = list(res[:nout])
    return (main if multi else main[0]), list(res[nout:])


def ffn_up(h, wg, wu, name, comm=None):
    S, D = h.shape
    FS = wg.shape[-1]
    tm = _row_tile(S)

    def body(h_ref, wg_ref, wu_ref, g_ref, u_ref, a_ref):
        hv = h_ref[...]
        g = _dot(hv, wg_ref[...])
        u = _dot(hv, wu_ref[...])
        g_ref[...] = g.astype(BF16)
        u_ref[...] = u.astype(BF16)
        a_ref[...] = (g * jax.nn.sigmoid(g) * u).astype(BF16)

    out = pl.BlockSpec((tm, FS), lambda k, i: (i, k))
    shard = lambda k, i: k
    return _call(
        body, grid=(NSH, S // tm),
        in_specs=[pl.BlockSpec((tm, D), lambda k, i: (i, 0)), _wspec(wg, shard), _wspec(wu, shard)],
        out_specs=[out] * 3, out_shape=[_sds((S, NSH * FS), BF16)] * 3,
        name=name, args=(h, wg, wu), comm=comm)


def col_mm(h, w, name, comm=None):
    S, D = h.shape
    NS = w.shape[-1]
    tm = _row_tile(S)

    def body(h_ref, w_ref, z_ref):
        z_ref[...] = _dot(h_ref[...], w_ref[...])

    return _call(
        body, grid=(NSH, S // tm),
        in_specs=[pl.BlockSpec((tm, D), lambda k, i: (i, 0)), _wspec(w, lambda k, i: k)],
        out_specs=pl.BlockSpec((tm, NS), lambda k, i: (i, k)), out_shape=_sds((S, NSH * NS), F32),
        name=name, args=(h, w), comm=comm)


def resid_mm(a, w, x, scale, name, comm=None):
    S = a.shape[0]
    KS, D = w.shape[-2], w.shape[-1]
    tm = _row_tile(S)

    def body(a_ref, w_ref, x_ref, o_ref, acc):
        k = pl.program_id(1)

        @pl.when(k == 0)
        def _():
            acc[...] = jnp.zeros_like(acc)

        acc[...] += _dot(a_ref[...], w_ref[...])

        @pl.when(k == NSH - 1)
        def _():
            o_ref[...] = x_ref[...] + scale * acc[...]

    row = pl.BlockSpec((tm, D), lambda i, k: (i, 0))
    return _call(
        body, grid=(S // tm, NSH),
        in_specs=[pl.BlockSpec((tm, KS), lambda i, k: (i, k)), _wspec(w, lambda i, k: k), row],
        out_specs=row, out_shape=_sds((S, D), F32),
        scratch=[pltpu.VMEM((tm, D), F32)], name=name, args=(a, w, x), comm=comm)


def ffn_bwd_act(dxb, wd, g, u, name):
    S, D = dxb.shape
    FS = wd.shape[-2]
    tm = _row_tile(S)

    def body(dx_ref, w_ref, g_ref, u_ref, dg_ref, du_ref):
        da = 0.5 * _dot_nt(dx_ref[...], w_ref[...])
        gv = g_ref[...].astype(F32)
        uv = u_ref[...].astype(F32)
        s = jax.nn.sigmoid(gv)
        du_ref[...] = (da * gv * s).astype(BF16)
        dg_ref[...] = (da * uv * s * (1.0 + gv * (1.0 - s))).astype(BF16)

    act = pl.BlockSpec((tm, FS), lambda k, i: (i, k))
    return pl.pallas_call(
        body, grid=(NSH, S // tm),
        in_specs=[pl.BlockSpec((tm, D), lambda k, i: (i, 0)), _wspec(wd, lambda k, i: k), act, act],
        out_specs=[act, act], out_shape=[_sds((S, NSH * FS), BF16)] * 2,
        name=name, compiler_params=_cp(2),
    )(dxb, wd, g, u)


def nt_col_mm(dxb, w, name):
    S, D = dxb.shape
    KS = w.shape[-2]
    tm = _row_tile(S)

    def body(dx_ref, w_ref, o_ref):
        o_ref[...] = _dot_nt(dx_ref[...], w_ref[...])

    return pl.pallas_call(
        body, grid=(NSH, S // tm),
        in_specs=[pl.BlockSpec((tm, D), lambda k, i: (i, 0)), _wspec(w, lambda k, i: k)],
        out_specs=pl.BlockSpec((tm, KS), lambda k, i: (i, k)), out_shape=_sds((S, NSH * KS), F32),
        name=name, compiler_params=_cp(2),
    )(dxb, w)


def wgrad(lhs, rhs, lhs_sharded, rhs_sharded, scale, tr, tc, name):
    S = lhs.shape[0]
    R = lhs.shape[1] // (NSH if lhs_sharded else 1)
    C = rhs.shape[1] // (NSH if rhs_sharded else 1)
    ts = _row_tile(S)
    nr, nc = R // tr, C // tc

    def body(l_ref, r_ref, o_ref):
        @pl.when(pl.program_id(3) == 0)
        def _():
            o_ref[...] = jnp.zeros_like(o_ref)

        o_ref[...] += scale * _dot_tn(l_ref[...], r_ref[...])

    lmap = (lambda k, a, b, s: (s, k * nr + a)) if lhs_sharded else (lambda k, a, b, s: (s, a))
    rmap = (lambda k, a, b, s: (s, k * nc + b)) if rhs_sharded else (lambda k, a, b, s: (s, b))
    return pl.pallas_call(
        body, grid=(NSH, nr, nc, S // ts),
        in_specs=[pl.BlockSpec((ts, tr), lmap), pl.BlockSpec((ts, tc), rmap)],
        out_specs=pl.BlockSpec((None, tr, tc), lambda k, a, b, s: (k, a, b)),
        out_shape=_sds((NSH, R, C), F32), name=name, compiler_params=_cp(4),
    )(lhs, rhs)


def bwd_dh(dys, ws, x, g, dxin, name, comm=None):
    S, D = x.shape
    NS = ws[0].shape[-1]
    tm = _row_tile(S)
    nj = len(dys)

    def mm_body(*refs):
        dy_refs, w_refs, dh_ref = refs[:nj], refs[nj:2 * nj], refs[2 * nj]

        @pl.when(pl.program_id(1) == 0)
        def _():
            dh_ref[...] = jnp.zeros_like(dh_ref)

        for dy_ref, w_ref in zip(dy_refs, w_refs):
            dh_ref[...] += _dot_nt(dy_ref[...], w_ref[...])

    dh, comm_out = _call(
        mm_body, grid=(S // tm, NSH),
        in_specs=[pl.BlockSpec((tm, NS), lambda i, k: (i, k))] * nj + [_wspec(w, lambda i, k: k) for w in ws],
        out_specs=pl.BlockSpec((tm, D), lambda i, k: (i, 0)), out_shape=_sds((S, D), F32),
        name=name + "_mm", args=(*dys, *ws), comm=comm)

    def norm_body(dh_ref, x_ref, g_ref, dxin_ref, dx_ref, dxb_ref, dg_ref):
        dx, dg = _rms_bwd_tile(x_ref[...], g_ref[...], dh_ref[...])
        tot = dxin_ref[...] + dx
        dx_ref[...] = tot
        dxb_ref[...] = tot.astype(BF16)

        @pl.when(pl.program_id(0) == 0)
        def _():
            dg_ref[...] = jnp.zeros_like(dg_ref)

        dg_ref[...] += dg

    row = pl.BlockSpec((tm, D), lambda i: (i, 0))
    vec = pl.BlockSpec((1, D), lambda i: (0, 0))
    dx, dxb, dgain = pl.pallas_call(
        norm_body, grid=(S // tm,), in_specs=[row, row, vec, row], out_specs=[row, row, vec],
        out_shape=[_sds((S, D), F32), _sds((S, D), BF16), _sds((1, D), F32)],
        name=name + "_norm", compiler_params=_cp(1),
    )(dh, x, g.reshape(1, D), dxin)
    return dx, dxb, dgain, comm_out


PAIRS_PER_STEP = {1: 8, 4: 1, 16: 1}
ZQ, ZK, ZV = 0, D_ATTN // 128, 2 * D_ATTN // 128


def _band_valid(n):
    qi = lax.broadcasted_iota(jnp.int32, (BLK, 2 * BLK), 0)
    kj = lax.broadcasted_iota(jnp.int32, (BLK, 2 * BLK), 1)
    dist = qi + BLK - kj
    return (dist >= 0) & (dist <= SPAN) & ((kj >= BLK) | (n > 0))


def _split_residues(src_ref, dst, d, pp, row0=0):
    for r in range(d):
        for p in range(pp):
            dst[r * pp + p, row0:row0 + BLK, :] = src_ref[pl.ds(r, BLK, stride=d), p * 128:(p + 1) * 128].astype(dst.dtype)


def _lane_pick(stat, lane, idx):
    return jnp.sum(jnp.where(lane == idx, stat, 0.0), axis=-1, keepdims=True)


def attn_fwd(z, d, name):
    S = z.shape[0]
    R = BLK * d
    nb = S // R
    pp = PAIRS_PER_STEP[d]
    G = d * pp
    W = 128 * pp

    def body(q_ref, kc_ref, kp_ref, vc_ref, vp_ref, o_ref, lse_ref, qs, ks, vs, os_, ls):
        n, hb = pl.program_id(0), pl.program_id(1)
        valid = _band_valid(n)
        lane = lax.broadcasted_iota(jnp.int32, (1, 128), 1)
        _split_residues(q_ref, qs, d, pp)
        _split_residues(kp_ref, ks, d, pp)
        _split_residues(kc_ref, ks, d, pp, BLK)
        _split_residues(vp_ref, vs, d, pp)
        _split_residues(vc_ref, vs, d, pp, BLK)

        def step(j, carry):
            q2, k2, v2 = qs[j], ks[j], vs[j]
            pair = hb * pp + j % pp
            o2 = jnp.zeros((BLK, 128), F32)
            stat = jnp.zeros((BLK, 128), F32)
            for h in range(2):
                hm = (lane < HEAD_DIM) if h == 0 else (lane >= HEAD_DIM)
                qm = jnp.where(hm, q2, jnp.zeros_like(q2))
                vm = jnp.where(hm, v2, jnp.zeros_like(v2))
                s = jnp.where(valid, _dot_nt(qm, k2) * ATTN_SCALE, NEG_INF)
                m = jnp.max(s, axis=-1, keepdims=True)
                p = jnp.exp(s - m)
                lsum = jnp.sum(p, axis=-1, keepdims=True)
                o2 = o2 + _dot(p.astype(BF16), vm) / lsum
                stat = jnp.where(lane == 2 * pair + h, m + jnp.log(lsum), stat)
            os_[j] = o2
            ls[j] = stat
            return carry

        lax.fori_loop(0, G, step, 0, unroll=min(G, 8))

        @pl.when(hb == 0)
        def _():
            lse_ref[...] = jnp.zeros_like(lse_ref)

        for r in range(d):
            rows = pl.ds(r, BLK, stride=d)
            acc = lse_ref[rows, :]
            for p in range(pp):
                o_ref[rows, p * 128:(p + 1) * 128] = os_[r * pp + p]
                acc = acc + ls[r * pp + p]
            lse_ref[rows, :] = acc

    cur = lambda c: pl.BlockSpec((R, W), lambda n, hb: (n, c // pp + hb))
    prev = lambda c: pl.BlockSpec((R, W), lambda n, hb: (jnp.maximum(n - 1, 0), c // pp + hb))
    return pl.pallas_call(
        body, grid=(nb, N_PAIR // pp),
        in_specs=[cur(ZQ), cur(ZK), prev(ZK), cur(ZV), prev(ZV)],
        out_specs=[pl.BlockSpec((R, W), lambda n, hb: (n, hb)), pl.BlockSpec((R, 128), lambda n, hb: (n, 0))],
        out_shape=[_sds((S, D_ATTN), F32), _sds((S, 128), F32)],
        scratch_shapes=[pltpu.VMEM((G, BLK, 128), BF16), pltpu.VMEM((G, 2 * BLK, 128), BF16), pltpu.VMEM((G, 2 * BLK, 128), BF16),
                        pltpu.VMEM((G, BLK, 128), F32), pltpu.VMEM((G, BLK, 128), F32)],
        name=name, compiler_params=_cp(2),
    )(z, z, z, z, z)


def _pair_weights(w, lane):
    return [jnp.where(lane < HEAD_DIM, w[:, 2 * hp:2 * hp + 1], w[:, 2 * hp + 1:2 * hp + 2]) for hp in range(N_PAIR)]


def attn_merge(os_, lses, name):
    S = os_[0].shape[0]
    tm = _row_tile(S)
    npat = len(os_)

    def body(*refs):
        o_refs, l_refs = refs[:npat], refs[npat:2 * npat]
        y_ref, yb_ref, lt_ref = refs[2 * npat:]
        lane = lax.broadcasted_iota(jnp.int32, (1, 128), 1)
        ls = [r[...] for r in l_refs]
        mx = functools.reduce(jnp.maximum, ls)
        es = [jnp.exp(v - mx) for v in ls]
        den = functools.reduce(jnp.add, es)
        lt_ref[...] = mx + jnp.log(den)
        ws = [_pair_weights(e / den, lane) for e in es]
        for hp in range(N_PAIR):
            sl = slice(hp * 128, (hp + 1) * 128)
            y = ws[0][hp] * o_refs[0][:, sl]
            for p in range(1, npat):
                y = y + ws[p][hp] * o_refs[p][:, sl]
            y_ref[:, sl] = y
            yb_ref[:, sl] = y.astype(BF16)

    big = pl.BlockSpec((tm, D_ATTN), lambda i: (i, 0))
    st = pl.BlockSpec((tm, 128), lambda i: (i, 0))
    return pl.pallas_call(
        body, grid=(S // tm,), in_specs=[big] * npat + [st] * npat, out_specs=[big, big, st],
        out_shape=[_sds((S, D_ATTN), F32), _sds((S, D_ATTN), BF16), _sds((S, 128), F32)],
        name=name, compiler_params=_cp(1),
    )(*os_, *lses)


def attn_bwd(z, dycat, y, lse_tot, d, name):
    S = z.shape[0]
    R = BLK * d
    nb = S // R
    pp = PAIRS_PER_STEP[d]
    G = d * pp
    W = 128 * pp

    def body(q_ref, kc_ref, kp_ref, vc_ref, vp_ref, dy_ref, y_ref, l_ref, dq_ref, dk_ref, dv_ref,
             qs, ks, vs, dys, ys, ls, dqs, dks, dvs, ck, cv):
        hb, n = pl.program_id(0), pl.program_id(1)

        def store_rows(ref, buf):
            for r in range(d):
                for p in range(pp):
                    ref[pl.ds(r, BLK, stride=d), p * 128:(p + 1) * 128] = buf[r * pp + p]

        @pl.when(n < nb)
        def _():
            valid = _band_valid(n)
            first = n == 0
            lane = lax.broadcasted_iota(jnp.int32, (1, 128), 1)
            _split_residues(q_ref, qs, d, pp)
            _split_residues(kp_ref, ks, d, pp)
            _split_residues(kc_ref, ks, d, pp, BLK)
            _split_residues(vp_ref, vs, d, pp)
            _split_residues(vc_ref, vs, d, pp, BLK)
            _split_residues(dy_ref, dys, d, pp)
            _split_residues(y_ref, ys, d, pp)
            _split_residues(l_ref, ls, d, 1)

            def step(j, carry):
                q2, k2, v2, dy2, y2 = qs[j], ks[j], vs[j], dys[j], ys[j]
                stat = ls[j // pp]
                pair = hb * pp + j % pp
                dq2 = jnp.zeros((BLK, 128), F32)
                dk2 = jnp.zeros((2 * BLK, 128), F32)
                dv2 = jnp.zeros((2 * BLK, 128), F32)
                for h in range(2):
                    hm = (lane < HEAD_DIM) if h == 0 else (lane >= HEAD_DIM)
                    qm = jnp.where(hm, q2, jnp.zeros_like(q2))
                    km = jnp.where(hm, k2, jnp.zeros_like(k2))
                    dym = jnp.where(hm, dy2, 0.0)
                    dymb = dym.astype(BF16)
                    s = _dot_nt(qm, k2) * ATTN_SCALE
                    p = jnp.where(valid, jnp.exp(s - _lane_pick(stat, lane, 2 * pair + h)), 0.0)
                    dp = _dot_nt(dymb, v2)
                    delta = jnp.sum(dym * y2, axis=-1, keepdims=True)
                    ds = (p * (dp - delta) * ATTN_SCALE).astype(BF16)
                    dq2 = dq2 + _dot(ds, km)
                    dk2 = dk2 + _dot_tn(ds, qm)
                    dv2 = dv2 + _dot_tn(p.astype(BF16), dymb)
                dqs[j] = dq2
                dks[j] = jnp.where(first, 0.0, ck[j]) + dk2[:BLK]
                dvs[j] = jnp.where(first, 0.0, cv[j]) + dv2[:BLK]
                ck[j] = dk2[BLK:]
                cv[j] = dv2[BLK:]
                return carry

            lax.fori_loop(0, G, step, 0, unroll=min(G, 8))
            store_rows(dq_ref, dqs)
            store_rows(dk_ref, dks)
            store_rows(dv_ref, dvs)

        @pl.when(n == nb)
        def _():
            store_rows(dk_ref, ck)
            store_rows(dv_ref, cv)

    cur = lambda n: jnp.minimum(n, nb - 1)
    prev = lambda n: jnp.clip(n - 1, 0, nb - 1)
    zb = lambda c, p: pl.BlockSpec((R, W), lambda hb, n: ((prev(n) if p else cur(n)), c // pp + hb))
    big_cur = pl.BlockSpec((R, W), lambda hb, n: (cur(n), hb))
    big_lag = pl.BlockSpec((R, W), lambda hb, n: (jnp.maximum(n - 1, 0), hb))
    buf = lambda rows, dt: pltpu.VMEM((G, rows, 128), dt)
    return pl.pallas_call(
        body, grid=(N_PAIR // pp, nb + 1),
        in_specs=[zb(ZQ, False), zb(ZK, False), zb(ZK, True), zb(ZV, False), zb(ZV, True),
                  big_cur, big_cur, pl.BlockSpec((R, 128), lambda hb, n: (cur(n), 0))],
        out_specs=[big_cur, big_lag, big_lag],
        out_shape=[_sds((S, D_ATTN), F32)] * 3,
        scratch_shapes=[buf(BLK, BF16), buf(2 * BLK, BF16), buf(2 * BLK, BF16), buf(BLK, F32), buf(BLK, F32),
                        pltpu.VMEM((d, BLK, 128), F32), buf(BLK, F32), buf(BLK, F32), buf(BLK, F32), buf(BLK, F32), buf(BLK, F32)],
        name=name, compiler_params=_cp(2),
    )(z, z, z, z, z, dycat, y, lse_tot)


ZC_GB, ZC_GC, ZC_CI, ZC_PI = 6, 7, 8, 9


def _pool_counts(i, tb):
    pos = lax.broadcasted_iota(jnp.int32, (tb, D_POOL), 0) + i * tb + 1
    grp = lax.broadcasted_iota(jnp.int32, (tb, D_POOL), 1) // POOL_GROUP
    win = jnp.where(grp == 0, POOL_WINDOWS[0], jnp.where(grp == 1, POOL_WINDOWS[1],
                    jnp.where(grp == 2, POOL_WINDOWS[2], POOL_WINDOWS[3])))
    return jnp.minimum(pos, win).astype(F32), grp


def _select_group(stages, grp):
    return jnp.where(grp == 0, stages[0], jnp.where(grp == 1, stages[1], jnp.where(grp == 2, stages[2], stages[3])))


def _causal_window_sums(x2):
    s1 = x2 + pltpu.roll(x2, 1, 0)
    s2 = s1 + pltpu.roll(s1, 2, 0)
    s3 = s2 + pltpu.roll(s2, 4, 0)
    s4 = s3 + pltpu.roll(s3, 8, 0)
    return [s1, s2, s3, s4]


def _anticausal_window_sums(x2):
    n = x2.shape[0]
    s1 = x2 + pltpu.roll(x2, n - 1, 0)
    s2 = s1 + pltpu.roll(s1, n - 2, 0)
    s3 = s2 + pltpu.roll(s2, n - 4, 0)
    s4 = s3 + pltpu.roll(s3, n - 8, 0)
    return [s1, s2, s3, s4]


def _pooled(p_prev, p_cur, i, tb):
    x2 = jnp.concatenate([jnp.where(i > 0, p_prev, 0.0), p_cur], axis=0)
    count, grp = _pool_counts(i, tb)
    win_sum = _select_group([s[tb:] for s in _causal_window_sums(x2)], grp)
    return win_sum / count - p_cur, count, grp


def _pool_mm(v, pw_ref, nt):
    outs = []
    for gi in range(len(POOL_WINDOWS)):
        sl = slice(gi * POOL_GROUP, (gi + 1) * POOL_GROUP)
        outs.append(_dot_nt(v[:, sl], pw_ref[gi]) if nt else _dot(v[:, sl], pw_ref[gi]))
    return jnp.concatenate(outs, axis=1)


def conv_pool_fwd(z, conv_w, pool_w, pool_scale, name):
    S = z.shape[0]
    tb = min(256, S)

    def body(gb_ref, gc_ref, gcp_ref, ci_ref, cip_ref, pi_ref, pip_ref, cw_ref, pw_ref, ps_ref, yc_ref, yp_ref):
        i = pl.program_id(0)
        u2 = jnp.concatenate([jnp.where(i > 0, gcp_ref[...] * cip_ref[...], 0.0), gc_ref[...] * ci_ref[...]], axis=0)
        conv = cw_ref[0:1, :] * pltpu.roll(u2, 2, 0) + cw_ref[1:2, :] * pltpu.roll(u2, 1, 0) + cw_ref[2:3, :] * u2
        yc_ref[...] = (gb_ref[...] * conv[tb:]).astype(BF16)
        pooled, _, _ = _pooled(pip_ref[...], pi_ref[...], i, tb)
        yp_ref[...] = (_pool_mm(pooled.astype(BF16), pw_ref, False) * ps_ref[...]).astype(BF16)

    cur = lambda c: pl.BlockSpec((tb, 512), lambda i: (i, c))
    prev = lambda c: pl.BlockSpec((tb, 512), lambda i: (jnp.maximum(i - 1, 0), c))
    full = lambda a: pl.BlockSpec(a.shape, lambda i: (0,) * a.ndim)
    out = pl.BlockSpec((tb, 512), lambda i: (i, 0))
    return pl.pallas_call(
        body, grid=(S // tb,),
        in_specs=[cur(ZC_GB), cur(ZC_GC), prev(ZC_GC), cur(ZC_CI), prev(ZC_CI), cur(ZC_PI), prev(ZC_PI),
                  full(conv_w), full(pool_w), full(pool_scale)],
        out_specs=[out, out], out_shape=[_sds((S, 512), BF16)] * 2, name=name, compiler_params=_cp(1),
    )(z, z, z, z, z, z, z, conv_w, pool_w, pool_scale)


def conv_pool_bwd(z, dycat, conv_w, pool_w, pool_scale, name):
    S = z.shape[0]
    tb = min(256, S)
    nblk = S // tb
    ng = len(POOL_WINDOWS)

    def body(gb_ref, gbn_ref, gc_ref, gcp_ref, ci_ref, cip_ref, pi_ref, pip_ref, dyc_ref, dycn_ref, dyp_ref, dypn_ref,
             cw_ref, pw_ref, ps_ref, dz_ref, dcw_ref, dpw_ref, dps_ref):
        i = pl.program_id(0)
        last = i == nblk - 1

        @pl.when(i == 0)
        def _():
            dcw_ref[...] = jnp.zeros_like(dcw_ref)
            dpw_ref[...] = jnp.zeros_like(dpw_ref)
            dps_ref[...] = jnp.zeros_like(dps_ref)

        gc, ci = gc_ref[...], ci_ref[...]
        u2 = jnp.concatenate([jnp.where(i > 0, gcp_ref[...] * cip_ref[...], 0.0), gc * ci], axis=0)
        um2, um1, u0 = pltpu.roll(u2, 2, 0)[tb:], pltpu.roll(u2, 1, 0)[tb:], u2[tb:]
        conv = cw_ref[0:1, :] * um2 + cw_ref[1:2, :] * um1 + cw_ref[2:3, :] * u0
        dyc = dyc_ref[...]
        dconv = dyc * gb_ref[...]
        dc2 = jnp.concatenate([dconv, jnp.where(last, 0.0, dycn_ref[...] * gbn_ref[...])], axis=0)
        du = (cw_ref[2:3, :] * dconv + cw_ref[1:2, :] * pltpu.roll(dc2, 2 * tb - 1, 0)[:tb]
              + cw_ref[0:1, :] * pltpu.roll(dc2, 2 * tb - 2, 0)[:tb])
        dz_ref[:, 0:512] = (dyc * conv).astype(BF16)
        dz_ref[:, 512:1024] = (du * ci).astype(BF16)
        dz_ref[:, 1024:1536] = (du * gc).astype(BF16)
        dcw_ref[0:1, :] += jnp.sum(dconv * um2, axis=0, keepdims=True)
        dcw_ref[1:2, :] += jnp.sum(dconv * um1, axis=0, keepdims=True)
        dcw_ref[2:3, :] += jnp.sum(dconv * u0, axis=0, keepdims=True)

        pooled, count, grp = _pooled(pip_ref[...], pi_ref[...], i, tb)
        pooled_b = pooled.astype(BF16)
        t = _pool_mm(pooled_b, pw_ref, False)
        dyp, ps = dyp_ref[...], ps_ref[...]
        dps_ref[...] += jnp.sum(dyp * t, axis=0, keepdims=True)
        dt_b = (dyp * ps).astype(BF16)
        for gi in range(ng):
            sl = slice(gi * POOL_GROUP, (gi + 1) * POOL_GROUP)
            dpw_ref[gi] += _dot_tn(pooled_b[:, sl], dt_b[:, sl])
        dpooled = _pool_mm(dt_b, pw_ref, True)
        dpooled_n = _pool_mm((dypn_ref[...] * ps).astype(BF16), pw_ref, True)
        count_n, _ = _pool_counts(i + 1, tb)
        dq2 = jnp.concatenate([dpooled / count, jnp.where(last, 0.0, dpooled_n / count_n)], axis=0)
        lead = _select_group([s[:tb] for s in _anticausal_window_sums(dq2)], grp)
        dz_ref[:, 1536:2048] = (lead - dpooled).astype(BF16)

    cur = lambda c: pl.BlockSpec((tb, 512), lambda i: (i, c))
    prev = lambda c: pl.BlockSpec((tb, 512), lambda i: (jnp.maximum(i - 1, 0), c))
    nxt = lambda c: pl.BlockSpec((tb, 512), lambda i: (jnp.minimum(i + 1, nblk - 1), c))
    full = lambda a: pl.BlockSpec(a.shape, lambda i: (0,) * a.ndim)
    yc_c, yp_c = D_ATTN // 512, D_ATTN // 512 + 1
    return pl.pallas_call(
        body, grid=(nblk,),
        in_specs=[cur(ZC_GB), nxt(ZC_GB), cur(ZC_GC), prev(ZC_GC), cur(ZC_CI), prev(ZC_CI), cur(ZC_PI), prev(ZC_PI),
                  cur(yc_c), nxt(yc_c), cur(yp_c), nxt(yp_c), full(conv_w), full(pool_w), full(pool_scale)],
        out_specs=[pl.BlockSpec((tb, 2048), lambda i: (i, 0)), pl.BlockSpec((3, 512), lambda i: (0, 0)),
                   pl.BlockSpec((ng, POOL_GROUP, POOL_GROUP), lambda i: (0, 0, 0)), pl.BlockSpec((1, 512), lambda i: (0, 0))],
        out_shape=[_sds((S, 2048), BF16), _sds((3, 512), F32), _sds((ng, POOL_GROUP, POOL_GROUP), F32), _sds((1, 512), F32)],
        name=name, compiler_params=_cp(1),
    )(z, z, z, z, z, z, z, z, dycat, dycat, dycat, dycat, conv_w, pool_w, pool_scale)


def assemble_dz(parts_q, parts_k, parts_v, dcp, name):
    S = dcp.shape[0]
    tm = _row_tile(S)
    npat = len(parts_q)

    def body(*refs):
        dz_ref = refs[-1]
        dcp_ref = refs[-2]
        for j in range(3):
            acc = refs[j * npat][...]
            for p in range(1, npat):
                acc = acc + refs[j * npat + p][...]
            dz_ref[:, j * D_ATTN:(j + 1) * D_ATTN] = acc.astype(BF16)
        dz_ref[:, 3 * D_ATTN:] = dcp_ref[...]

    big = pl.BlockSpec((tm, D_ATTN), lambda i: (i, 0))
    return pl.pallas_call(
        body, grid=(S // tm,), in_specs=[big] * (3 * npat) + [pl.BlockSpec((tm, D_IN - 3 * D_ATTN), lambda i: (i, 0))],
        out_specs=pl.BlockSpec((tm, D_IN), lambda i: (i, 0)), out_shape=_sds((S, D_IN), BF16),
        name=name, compiler_params=_cp(1),
    )(*parts_q, *parts_k, *parts_v, dcp)


def adamw(w, g, m, v, name):
    shape = w.shape
    cols = shape[-1]
    rows = w.size // cols
    tr = rows
    for cand in (256, 128, 64, 32, 16, 8):
        if rows % cand == 0:
            tr = cand
            break
    c1 = 1.0 - ADAM_B1 ** ADAM_STEP
    c2 = 1.0 - ADAM_B2 ** ADAM_STEP

    def body(w_ref, g_ref, m_ref, v_ref, d_ref, mo_ref, vo_ref):
        gv = g_ref[...]
        mn = ADAM_B1 * m_ref[...] + (1.0 - ADAM_B1) * gv
        vn = ADAM_B2 * v_ref[...] + (1.0 - ADAM_B2) * (gv * gv)
        d_ref[...] = -ADAM_LR * ((mn / c1) / (jnp.sqrt(vn / c2) + ADAM_EPS) + ADAM_WD * w_ref[...])
        mo_ref[...] = mn
        vo_ref[...] = vn

    blk = pl.BlockSpec((tr, cols), lambda i: (i, 0))
    outs = pl.pallas_call(
        body, grid=(rows // tr,), in_specs=[blk] * 4, out_specs=[blk] * 3,
        out_shape=[_sds((rows, cols), F32)] * 3, name=name, compiler_params=_cp(1),
    )(*(a.reshape(rows, cols) for a in (w, g, m, v)))
    return tuple(o.reshape(shape) for o in outs)


ANY = pl.BlockSpec(memory_space=pl.ANY)


def _place():
    x, y, c = lax.axis_index("x"), lax.axis_index("y"), lax.axis_index("c")
    chips = [(1 - x, y), (x, 1 - y), (1 - x, 1 - y)]
    return x, y, c, chips


def gather_weights(ws):
    nw = len(ws)
    split = [w.ndim == 2 and w.shape[0] % 32 == 0 for w in ws]

    def body(*refs):
        ins, outs = refs[:nw], refs[nw:2 * nw]
        send, recv = refs[2 * nw:]
        x, y, c, chips = _place()
        me, sib = 2 * x + y, (x, y, 1 - c)

        def half(j, k, hc):
            if not split[j]:
                return outs[j].at[k]
            ha = ws[j].shape[0] // 2
            return outs[j].at[k, pl.ds(hc * ha, ha), :]

        def rcopy(j, s, src, dst, to):
            return pltpu.make_async_remote_copy(src_ref=src, dst_ref=dst, send_sem=send.at[j, s], recv_sem=recv.at[j, s],
                                                device_id=to, device_id_type=MESH)

        first = [rcopy(j, 6, ins[j], outs[j].at[me], sib) for j in range(nw)]
        for j in range(nw):
            ha = ws[j].shape[0] // 2
            mine = ins[j].at[pl.ds(c * ha, ha), :] if split[j] else ins[j]
            for s, (px, py) in enumerate(chips):
                first.append(rcopy(j, s, mine, half(j, me, c), (px, py, c)))
        for cp in first:
            cp.start()
        passed = []
        for j in range(nw):
            for s, (px, py) in enumerate(chips):
                blk = half(j, 2 * px + py, c)
                rcopy(j, s, blk, blk, (px, py, c)).wait_recv()
                if split[j]:
                    fwd = rcopy(j, 3 + s, blk, blk, sib)
                    fwd.start()
                    passed.append(fwd)
        for j in range(nw):
            for s, (px, py) in enumerate(chips):
                if split[j]:
                    blk = half(j, 2 * px + py, 1 - c)
                    rcopy(j, 3 + s, blk, blk, sib).wait_recv()
        for j in range(nw):
            rcopy(j, 6, ins[j], outs[j].at[me], sib).wait_recv()
        for cp in first + passed:
            cp.wait_send()

    return pl.pallas_call(
        body, in_specs=[ANY] * nw, out_specs=[ANY] * nw,
        out_shape=[_sds((NSH,) + w.shape, w.dtype) for w in ws],
        scratch_shapes=[pltpu.SemaphoreType.DMA((nw, 7)), pltpu.SemaphoreType.DMA((nw, 7))],
        name="gather_weights",
    )(*ws)


def sibling_send_half(gs):
    ng = len(gs)

    def body(*refs):
        ins, outs = refs[:ng], refs[ng:2 * ng]
        send, recv = refs[2 * ng:]
        x, y, c, _ = _place()
        cps = []
        for j in range(ng):
            ha = gs[j].shape[1] // 2
            cps.append(pltpu.make_async_remote_copy(
                src_ref=ins[j].at[:, pl.ds((1 - c) * ha, ha), :], dst_ref=outs[j], send_sem=send.at[j], recv_sem=recv.at[j],
                device_id=(x, y, 1 - c), device_id_type=MESH))
        for cp in cps:
            cp.start()
        for cp in cps:
            cp.wait()

    return pl.pallas_call(
        body, in_specs=[ANY] * ng, out_specs=[ANY] * ng,
        out_shape=[_sds((NSH, g.shape[1] // 2, g.shape[2]), g.dtype) for g in gs],
        scratch_shapes=[pltpu.SemaphoreType.DMA((ng,)), pltpu.SemaphoreType.DMA((ng,))],
        name="sibling_send_half",
    )(*gs)


def pair_sum(g, got, half_index, name):
    _, A, B = g.shape
    ha = A // 2
    tr = ha
    for cand in (512, 256, 128, 64):
        if ha % cand == 0:
            tr = cand
            break
    nt = ha // tr

    def body(c_ref, g_ref, r_ref, o_ref):
        del c_ref
        o_ref[...] = (g_ref[...] + r_ref[...]).astype(BF16)

    return pl.pallas_call(
        body,
        grid_spec=pltpu.PrefetchScalarGridSpec(
            num_scalar_prefetch=1, grid=(NSH, nt),
            in_specs=[pl.BlockSpec((None, tr, B), lambda k, t, c: (k, c[0] * nt + t, 0)),
                      pl.BlockSpec((None, tr, B), lambda k, t, c: (k, t, 0))],
            out_specs=pl.BlockSpec((None, tr, B), lambda k, t, c: (k, t, 0))),
        out_shape=_sds((NSH, ha, B), BF16), name=name, compiler_params=_cp(2),
    )(half_index, g, got)


def _half_tile(ha):
    for cand in (512, 256, 128, 64):
        if ha % cand == 0:
            return cand
    return ha


def chip_sum(partials, slots, chip_index, name):
    depth = len(partials)
    _, ha, B = partials[0].shape
    tr = _half_tile(ha)
    nt = ha // tr

    def body(me_ref, *refs):
        p_refs, s_refs, o_ref = refs[:depth], refs[depth:depth + depth * NSH], refs[depth + depth * NSH]
        l = pl.program_id(0)
        for ll in range(depth):
            @pl.when(l == ll)
            def _(ll=ll):
                own = p_refs[ll][...].astype(F32)
                acc = jnp.where(me_ref[0] == 0, own, s_refs[ll * NSH][...].astype(F32))
                for k in range(1, NSH):
                    acc = acc + jnp.where(me_ref[0] == k, own, s_refs[ll * NSH + k][...].astype(F32))
                o_ref[...] = acc

    def frozen(ll):
        return lambda l, t: jnp.where(l == ll, t, jnp.where(l < ll, 0, nt - 1))

    def slot(k):
        return lambda me: jnp.where(me[0] == k, (k + 1) % NSH, k)

    in_specs = [pl.BlockSpec((None, tr, B), lambda l, t, me, ll=ll: (me[0], frozen(ll)(l, t), 0)) for ll in range(depth)]
    in_specs += [pl.BlockSpec((None, tr, B), lambda l, t, me, ll=ll, k=k: (slot(k)(me), frozen(ll)(l, t), 0))
                 for ll in range(depth) for k in range(NSH)]
    return pl.pallas_call(
        body,
        grid_spec=pltpu.PrefetchScalarGridSpec(
            num_scalar_prefetch=1, grid=(depth, nt), in_specs=in_specs,
            out_specs=pl.BlockSpec((None, tr, B), lambda l, t, me: (l, t, 0))),
        out_shape=_sds((depth, ha, B), F32), name=name, compiler_params=_cp(2),
    )(chip_index, *partials, *[s for s in slots for _ in range(NSH)])


def sibling_swap(mine):
    n = len(mine)

    def body(*refs):
        ins, outs = refs[:n], refs[n:2 * n]
        send, recv = refs[2 * n:]
        x, y, c, _ = _place()
        cps = [pltpu.make_async_remote_copy(src_ref=ins[j], dst_ref=outs[j], send_sem=send.at[j], recv_sem=recv.at[j],
                                            device_id=(x, y, 1 - c), device_id_type=MESH) for j in range(n)]
        for cp in cps:
            cp.start()
        for cp in cps:
            cp.wait()

    return pl.pallas_call(
        body, in_specs=[ANY] * n, out_specs=[ANY] * n, out_shape=[_sds(m.shape, m.dtype) for m in mine],
        scratch_shapes=[pltpu.SemaphoreType.DMA((n,)), pltpu.SemaphoreType.DMA((n,))], name="sibling_swap",
    )(*mine)


def adamw_halves(w, mine, theirs, m, v, core_index, name):
    depth, A, B = w.shape
    ha = A // 2
    tr = _half_tile(ha)
    while tr * B * 4 > 2 ** 20 and tr % 16 == 0:
        tr //= 2
    nt = ha // tr
    c1 = 1.0 - ADAM_B1 ** ADAM_STEP
    c2 = 1.0 - ADAM_B2 ** ADAM_STEP

    def body(c_ref, w_ref, a_ref, b_ref, m_ref, v_ref, g_ref, d_ref, mo_ref, vo_ref):
        gv = jnp.where(pl.program_id(1) == c_ref[0], a_ref[...], b_ref[...])
        mn = ADAM_B1 * m_ref[...] + (1.0 - ADAM_B1) * gv
        vn = ADAM_B2 * v_ref[...] + (1.0 - ADAM_B2) * (gv * gv)
        g_ref[...] = gv
        d_ref[...] = -ADAM_LR * ((mn / c1) / (jnp.sqrt(vn / c2) + ADAM_EPS) + ADAM_WD * w_ref[...])
        mo_ref[...] = mn
        vo_ref[...] = vn

    full = pl.BlockSpec((None, tr, B), lambda l, h, t, c: (l, h * nt + t, 0))
    a_spec = pl.BlockSpec((None, tr, B), lambda l, h, t, c: (l, jnp.where(h == c[0], t, 0), 0))
    b_spec = pl.BlockSpec((None, tr, B), lambda l, h, t, c: (l, jnp.where(h == c[0], 0, t), 0))
    return pl.pallas_call(
        body,
        grid_spec=pltpu.PrefetchScalarGridSpec(
            num_scalar_prefetch=1, grid=(depth, 2, nt), in_specs=[full, a_spec, b_spec, full, full], out_specs=[full] * 4),
        out_shape=[_sds(w.shape, F32)] * 4, name=name, compiler_params=_cp(3),
    )(core_index, w, mine, theirs, m, v)


def small_all_reduce(v):
    R = v.shape[0]

    def body(v_ref, o_ref, slots, send, recv):
        x, y, c, _ = _place()
        me = 4 * x + 2 * y + c
        slots[me] = v_ref[...]
        cps = []
        for m in range(1, 8):
            mx, my, mc = (m >> 2) & 1, (m >> 1) & 1, m & 1
            cps.append(pltpu.make_async_remote_copy(
                src_ref=v_ref, dst_ref=slots.at[me], send_sem=send.at[m - 1], recv_sem=recv.at[m - 1],
                device_id=(x ^ mx, y ^ my, c ^ mc), device_id_type=MESH))
        for cp in cps:
            cp.start()
        for cp in cps:
            cp.wait()
        acc = slots[0]
        for d in range(1, 8):
            acc = acc + slots[d]
        o_ref[...] = acc

    vm = pl.BlockSpec(memory_space=pltpu.VMEM)
    return pl.pallas_call(
        body, in_specs=[vm], out_specs=vm, out_shape=_sds((R, 128), F32),
        scratch_shapes=[pltpu.VMEM((8, R, 128), F32), pltpu.SemaphoreType.DMA((7,)), pltpu.SemaphoreType.DMA((7,))],
        name="small_all_reduce",
    )(v)


BIG = ("ffn1_w_gate", "ffn1_w_up", "ffn1_w_down", "w_in", "w_out", "ffn2_w_gate", "ffn2_w_up", "ffn2_w_down")
SMALL = ("ffn1_norm", "mix_norm", "conv_w", "pool_w", "pool_scale", "ffn2_norm", "final_norm")
WEIGHTS = ("ffn1_norm", "ffn1_w_gate", "ffn1_w_up", "ffn1_w_down", "mix_norm", "w_in", "conv_w", "pool_w", "pool_scale",
           "w_out", "ffn2_norm", "ffn2_w_gate", "ffn2_w_up", "ffn2_w_down", "final_norm")


def _tile_for(n, cap=1024):
    best = 128
    for t in range(128, min(n, cap) + 1, 128):
        if n % t == 0:
            best = t
    return n if n <= cap else best


GATHER_BEHIND = {
    (0, "ffn1_up"): [("w_in", 0), ("w_out", 0), ("ffn2_w_gate", 0), ("ffn2_w_up", 0)],
    (0, "ffn1_down"): [("ffn2_w_down", 0)],
    (0, "mix_in"): [("ffn1_w_gate", 1)],
    (0, "mix_out"): [("ffn1_w_up", 1)],
    (0, "ffn2_up"): [("ffn1_w_down", 1), ("w_in", 1), ("w_out", 1)],
    (0, "ffn2_down"): [("ffn2_w_gate", 1)],
    (1, "ffn1_up"): [("ffn2_w_up", 1), ("ffn2_w_down", 1)],
}


class _GatherPlan:
    def __init__(self, local):
        self.local, self.pending, self.ready = local, [], {}

    def comm(self, firsts):
        cm = _Comm()
        self._passing, self._firsts = list(self.pending), list(firsts)
        for _, buf in self._passing:
            cm.gather_pass(buf)
        for key in self._firsts:
            cm.gather_first(self.local[key])
        return cm

    def done(self, couts):
        npass = len(self._passing)
        for (key, _), buf in zip(self._passing, couts[:npass]):
            self.ready[key] = buf
        self.pending = list(zip(self._firsts, couts[npass:]))


def _reduce_start(grads, half_index, comm):
    got = sibling_send_half(grads)
    partial = [pair_sum(g, r, half_index, "pair_sum") for g, r in zip(grads, got)]
    for p in partial:
        comm.exchange(p)
    return partial


def kernel(x, ffn1_norm, ffn1_w_gate, ffn1_w_up, ffn1_w_down, mix_norm, w_in, conv_w, pool_w, pool_scale, w_out, ffn2_norm, ffn2_w_gate, ffn2_w_up, ffn2_w_down, final_norm, loss_target, m_ffn1_norm, m_ffn1_w_gate, m_ffn1_w_up, m_ffn1_w_down, m_mix_norm, m_w_in, m_conv_w, m_pool_w, m_pool_scale, m_w_out, m_ffn2_norm, m_ffn2_w_gate, m_ffn2_w_up, m_ffn2_w_down, m_final_norm, v_ffn1_norm, v_ffn1_w_gate, v_ffn1_w_up, v_ffn1_w_down, v_mix_norm, v_w_in, v_conv_w, v_pool_w, v_pool_scale, v_w_out, v_ffn2_norm, v_ffn2_w_gate, v_ffn2_w_up, v_ffn2_w_down, v_final_norm):
    given = dict(locals())
    W = {n: given[n] for n in WEIGHTS}
    M = {n: given["m_" + n] for n in WEIGHTS}
    V = {n: given["v_" + n] for n in WEIGHTS}
    depth = ffn1_norm.shape[0]
    D = x.shape[-1]
    xs = x[0]
    my_chip = 2 * lax.axis_index("x") + lax.axis_index("y")
    my_core = lax.axis_index("c")

    half_index = my_core.astype(jnp.int32).reshape(1)
    chip_index = my_chip.astype(jnp.int32).reshape(1)
    plan = _GatherPlan({(n, l): W[n][l].astype(BF16) for n in BIG for l in range(depth)})
    head = [("ffn1_w_gate", 0), ("ffn1_w_up", 0), ("ffn1_w_down", 0)]
    gathered = gather_weights([plan.local[key] for key in head] + [conv_w])
    plan.ready.update(zip(head, gathered[:-1]))
    conv_full = jnp.moveaxis(gathered[-1], 0, 2).reshape(depth, 3, D_CONV)
    pool_b = pool_w.astype(BF16)
    G = plan.ready

    def behind(l, stage):
        return plan.comm([(n, ll) for n, ll in GATHER_BEHIND.get((l, stage), []) if ll < depth])

    saved = []
    cur = xs
    for l in range(depth):
        x0 = cur
        h = rms_fwd(x0, ffn1_norm[l], "ffn1_norm")
        (g, u, a), got = ffn_up(h, G["ffn1_w_gate", l], G["ffn1_w_up", l], "ffn1_up", behind(l, "ffn1_up"))
        plan.done(got)
        cur, got = resid_mm(a, G["ffn1_w_down", l], x0, 0.5, "ffn1_down", behind(l, "ffn1_down"))
        plan.done(got)
        s1 = (x0, h, g, u, a)

        x1 = cur
        hm = rms_fwd(x1, mix_norm[l], "mix_norm")
        z, got = col_mm(hm, G["w_in", l], "mix_in", behind(l, "mix_in"))
        plan.done(got)
        pats = [attn_fwd(z, d, f"attn_fwd_d{d}") for d in DILATIONS]
        y, yb, lse_tot = attn_merge([p[0] for p in pats], [p[1] for p in pats], "attn_merge")
        y_conv, y_pool = conv_pool_fwd(z, conv_full[l], pool_b[l], pool_scale[l].reshape(1, D_POOL), "conv_pool_fwd")
        ycat = jnp.concatenate([yb, y_conv, y_pool], axis=1)
        cur, got = resid_mm(ycat, G["w_out", l], x1, 1.0, "mix_out", behind(l, "mix_out"))
        plan.done(got)
        sm = (x1, hm, z, y, lse_tot, ycat)

        x2 = cur
        h = rms_fwd(x2, ffn2_norm[l], "ffn2_norm")
        (g, u, a), got = ffn_up(h, G["ffn2_w_gate", l], G["ffn2_w_up", l], "ffn2_up", behind(l, "ffn2_up"))
        plan.done(got)
        cur, got = resid_mm(a, G["ffn2_w_down", l], x2, 0.5, "ffn2_down", behind(l, "ffn2_down"))
        plan.done(got)
        saved.append((s1, sm, (x2, h, g, u, a)))
    assert not plan.pending and len(G) == len(BIG) * depth

    loss11, dx, dxb, d_final = final_loss(cur, final_norm, loss_target[0], "final_loss")

    small_grads = {n: [None] * depth for n in SMALL if n != "final_norm"}
    partials = {n: [None] * depth for n in BIG}
    slots = {n: [None] * depth for n in BIG}

    def ffn_backward(sv, dx, dxb, norm, tag, l):
        x_in, h, g, u, a = sv
        names = [f"{tag}_w_down", f"{tag}_w_gate", f"{tag}_w_up"]
        wd, wg, wu = (G[n, l] for n in names)
        FS = wg.shape[-1]
        dg, du = ffn_bwd_act(dxb, wd, g, u, f"{tag}_bwd_act")
        fresh = [wgrad(a, dxb, True, False, 0.5, FS, _tile_for(D), f"{tag}_dwd"),
                 wgrad(h, dg, False, True, 1.0, _tile_for(D), FS, f"{tag}_dwg"),
                 wgrad(h, du, False, True, 1.0, _tile_for(D), FS, f"{tag}_dwu")]
        cm = _Comm()
        part = _reduce_start(fresh, half_index, cm)
        dx, dxb, dnorm, got = bwd_dh([dg, du], [wg, wu], x_in, norm[l], dx, f"{tag}_bwd_dh", cm)
        for n, p, s in zip(names, part, got):
            partials[n][l], slots[n][l] = p, s
        return dx, dxb, dnorm

    for l in reversed(range(depth)):
        s1, sm, s2 = saved[l]
        dx, dxb, small_grads["ffn2_norm"][l] = ffn_backward(s2, dx, dxb, ffn2_norm, "ffn2", l)

        x1, hm, z, y, lse_tot, ycat = sm
        dycat = nt_col_mm(dxb, G["w_out", l], "mix_out_bwd")
        g_wout = wgrad(ycat, dxb, True, False, 1.0, D_MIX // NSH, _tile_for(D, 2048), "mix_dwout")
        parts = [attn_bwd(z, dycat, y, lse_tot, d, f"attn_bwd_d{d}") for d in DILATIONS]
        dcp, dcw, dpw, dps = conv_pool_bwd(z, dycat, conv_full[l], pool_b[l], pool_scale[l].reshape(1, D_POOL), "conv_pool_bwd")
        dz = assemble_dz([p[0] for p in parts], [p[1] for p in parts], [p[2] for p in parts], dcp, "assemble_dz")
        g_win = wgrad(hm, dz, False, True, 1.0, _tile_for(D), D_IN // NSH, "mix_dwin")
        cm = _Comm()
        part = _reduce_start([g_wout, g_win], half_index, cm)
        dx, dxb, dnm, got = bwd_dh([dz], [G["w_in", l]], x1, mix_norm[l], dx, "mix_bwd_dh", cm)
        for n, p, s in zip(["w_out", "w_in"], part, got):
            partials[n][l], slots[n][l] = p, s
        small_grads["mix_norm"][l], small_grads["conv_w"][l] = dnm, dcw
        small_grads["pool_w"][l], small_grads["pool_scale"][l] = dpw, dps

        dx, dxb, small_grads["ffn1_norm"][l] = ffn_backward(s1, dx, dxb, ffn1_norm, "ffn1", l)

    mine = [chip_sum(partials[n], slots[n], chip_index, "chip_sum") for n in BIG]
    theirs = sibling_swap(mine)
    grads, delta, new_m, new_v = {}, {}, {}, {}
    for n, a, b in zip(BIG, mine, theirs):
        grads[n], delta[n], new_m[n], new_v[n] = adamw_halves(W[n], a, b, M[n], V[n], half_index, "adamw")

    small_full = {n: jnp.stack([a.reshape(W[n].shape[1:] if n != "conv_w" else (3, D_CONV)) for a in small_grads[n]])
                  for n in small_grads}
    small_full["final_norm"] = d_final.reshape(D)
    order = list(SMALL)
    packed = jnp.concatenate([small_full[n].reshape(-1) for n in order])
    pad = (-packed.shape[0]) % (8 * 128)
    packed = jnp.pad(packed, (0, pad)).reshape(-1, 128)
    summed = small_all_reduce(packed).reshape(-1)
    off = 0
    for n in order:
        size = small_full[n].size
        grads[n] = summed[off:off + size].reshape(small_full[n].shape)
        off += size
    grads["conv_w"] = lax.dynamic_slice_in_dim(grads["conv_w"], my_chip * (D_CONV // NSH), D_CONV // NSH, axis=2)

    def pack(src):
        flat_ = jnp.concatenate([src[n].reshape(-1) for n in order])
        return jnp.pad(flat_, (0, (-flat_.shape[0]) % (8 * 128))).reshape(-1, 128)

    ds, ms, vs = adamw(pack(W), pack(grads), pack(M), pack(V), "adamw_small")
    off = 0
    for n in order:
        size = W[n].size
        for dst, src in ((delta, ds), (new_m, ms), (new_v, vs)):
            dst[n] = src.reshape(-1)[off:off + size].reshape(W[n].shape)
        off += size

    loss = lax.psum(loss11[0, 0], ("x", "y", "c"))
    return (loss, dx.reshape(x.shape), *[grads[n] for n in WEIGHTS], *[delta[n] for n in WEIGHTS],
            *[new_m[n] for n in WEIGHTS], *[new_v[n] for n in WEIGHTS])
```

```python
import functools

import jax
import jax.numpy as jnp
from jax import lax
from jax.experimental import pallas as pl
from jax.experimental.pallas import tpu as pltpu

F32 = jnp.float32
BF16 = jnp.bfloat16
MESH = pl.DeviceIdType.MESH

RMS_EPS = 1e-6
NEG_INF = -1e30
HEAD_DIM = 64
BLK = 128
SPAN = 128
DILATIONS = (1, 4, 16)
D_ATTN = 1024
D_CONV = 512
D_POOL = 512
POOL_WINDOWS = (2, 4, 8, 16)
POOL_GROUP = 128
D_IN = 3 * D_ATTN + 3 * D_CONV + D_POOL
D_MIX = D_ATTN + D_CONV + D_POOL
N_PAIR = D_ATTN // 128
ATTN_SCALE = HEAD_DIM ** -0.5
NSH = 4
ADAM_LR, ADAM_B1, ADAM_B2, ADAM_EPS, ADAM_WD, ADAM_STEP = 0.001, 0.9, 0.999, 1e-08, 0.01, 10

VMEM_LIMIT = 56 * 2 ** 20


def _cp(n_axes):
    return pltpu.CompilerParams(dimension_semantics=("arbitrary",) * n_axes, vmem_limit_bytes=VMEM_LIMIT)


def _sds(shape, dtype):
    return jax.ShapeDtypeStruct(shape, dtype)


def _dot(a, b):
    return jnp.dot(a, b, preferred_element_type=F32)


def _dot_nt(a, b):
    return lax.dot_general(a, b, (((1,), (1,)), ((), ())), preferred_element_type=F32)


def _dot_tn(a, b):
    return lax.dot_general(a, b, (((0,), (0,)), ((), ())), preferred_element_type=F32)


def _row_tile(s):
    return min(512, s)


def rms_fwd(x, g, name):
    S, D = x.shape
    tm = _row_tile(S)

    def body(x_ref, g_ref, h_ref):
        xv = x_ref[...]
        r = lax.rsqrt(jnp.mean(xv * xv, axis=-1, keepdims=True) + RMS_EPS)
        h_ref[...] = (xv * r * g_ref[...]).astype(BF16)

    return pl.pallas_call(
        body, grid=(S // tm,),
        in_specs=[pl.BlockSpec((tm, D), lambda i: (i, 0)), pl.BlockSpec((1, D), lambda i: (0, 0))],
        out_specs=pl.BlockSpec((tm, D), lambda i: (i, 0)),
        out_shape=_sds((S, D), BF16), name=name, compiler_params=_cp(1),
    )(x, g.reshape(1, D))


def _rms_bwd_tile(xv, gv, dh):
    r = lax.rsqrt(jnp.mean(xv * xv, axis=-1, keepdims=True) + RMS_EPS)
    xhat = xv * r
    dg = jnp.sum(dh * xhat, axis=0, keepdims=True)
    dxhat = dh * gv
    dx = r * (dxhat - xhat * jnp.mean(dxhat * xhat, axis=-1, keepdims=True))
    return dx, dg


def final_loss(x, g, target, name):
    S, D = x.shape
    tm = _row_tile(S)

    def body(x_ref, g_ref, t_ref, loss_ref, dx_ref, dxb_ref, dg_ref):
        i = pl.program_id(0)
        xv, gv = x_ref[...], g_ref[...]
        r = lax.rsqrt(jnp.mean(xv * xv, axis=-1, keepdims=True) + RMS_EPS)
        err = xv * r * gv - t_ref[...]
        part = 0.5 * jnp.sum(jnp.mean(err * err, axis=-1, keepdims=True), axis=0, keepdims=True)
        dx, dg = _rms_bwd_tile(xv, gv, err * (1.0 / D))

        @pl.when(i == 0)
        def _():
            loss_ref[...] = jnp.zeros_like(loss_ref)
            dg_ref[...] = jnp.zeros_like(dg_ref)

        loss_ref[...] += part
        dg_ref[...] += dg
        dx_ref[...] = dx
        dxb_ref[...] = dx.astype(BF16)

    row = pl.BlockSpec((tm, D), lambda i: (i, 0))
    vec = pl.BlockSpec((1, D), lambda i: (0, 0))
    return pl.pallas_call(
        body, grid=(S // tm,), in_specs=[row, vec, row],
        out_specs=[pl.BlockSpec((1, 1), lambda i: (0, 0)), row, row, vec],
        out_shape=[_sds((1, 1), F32), _sds((S, D), F32), _sds((S, D), BF16), _sds((1, D), F32)],
        name=name, compiler_params=_cp(1),
    )(x, g.reshape(1, D), target)


def _wspec(w, imap):
    _, a, b = w.shape
    return pl.BlockSpec((None, a, b), lambda *ids: (imap(*ids), 0, 0))


class _Comm:
    def __init__(self):
        self.ins, self.out_shapes, self.aliases, self.items = [], [], {}, []

    def _add(self, kind, operand, out_shape, alias):
        if alias:
            self.aliases[len(self.ins)] = len(self.out_shapes)
        self.items.append((kind, len(self.ins), len(self.out_shapes)))
        self.ins.append(operand)
        self.out_shapes.append(out_shape)

    def gather_first(self, src):
        self._add("first", src, _sds((NSH,) + src.shape, src.dtype), False)

    def gather_pass(self, buf):
        self._add("pass", buf, _sds(buf.shape, buf.dtype), True)

    def sibling_half(self, grad):
        self._add("half", grad, _sds((NSH, grad.shape[1] // 2, grad.shape[2]), grad.dtype), False)

    def exchange(self, partial):
        self._add("xchg", partial, _sds(partial.shape, partial.dtype), False)

    def run(self, cins, couts, send, recv, start):
        x, y, c = lax.axis_index("x"), lax.axis_index("y"), lax.axis_index("c")
        chips = [(1 - x, y), (x, 1 - y), (1 - x, 1 - y)]
        me, sib = 2 * x + y, (x, y, 1 - c)
        for it, (kind, i, o) in enumerate(self.items):
            def rc(s, src, dst, to, it=it):
                return pltpu.make_async_remote_copy(src_ref=src, dst_ref=dst, send_sem=send.at[it, s], recv_sem=recv.at[it, s],
                                                    device_id=to, device_id_type=MESH)
            src, buf = cins[i], couts[o]
            if kind == "first":
                ha = src.shape[0] // 2
                rows = pl.ds(c * ha, ha)
                cps = [rc(s, src.at[rows], buf.at[me, rows], (px, py, c)) for s, (px, py) in enumerate(chips)]
                cps.append(rc(3, src, buf.at[me], sib))
                landing = [buf.at[2 * px + py, rows] for px, py in chips] + [buf.at[me]]
            elif kind == "pass":
                ha = buf.shape[1] // 2
                rows, other = pl.ds(c * ha, ha), pl.ds((1 - c) * ha, ha)
                cps = [rc(s, buf.at[2 * px + py, rows], buf.at[2 * px + py, rows], sib) for s, (px, py) in enumerate(chips)]
                landing = [buf.at[2 * px + py, other] for px, py in chips]
            elif kind == "half":
                ha = src.shape[1] // 2
                cps = [rc(0, src.at[:, pl.ds((1 - c) * ha, ha), :], buf, sib)]
                landing = [buf]
            else:
                cps = [rc(s, src.at[2 * px + py], buf.at[me], (px, py, c)) for s, (px, py) in enumerate(chips)]
                landing = [buf.at[2 * px + py] for px, py in chips]
            if start:
                for cp in cps:
                    cp.start()
            else:
                for s, dst in enumerate(landing):
                    rc(s, dst, dst, sib).wait_recv()
                for cp in cps:
                    cp.wait_send()


def _call(body, *, grid, in_specs, out_specs, out_shape, name, args, scratch=(), comm=None):
    multi = isinstance(out_shape, (list, tuple))
    oshape = list(out_shape) if multi else [out_shape]
    ospecs = list(out_specs) if multi else [out_specs]
    if comm is None or not comm.items:
        res = pl.pallas_call(body, grid=grid, in_specs=list(in_specs), out_specs=ospecs, out_shape=oshape,
                             scratch_shapes=list(scratch), name=name, compiler_params=_cp(len(grid)))(*args)
        return (list(res) if multi else res[0]), []
    nin, nout, nci, nco, nscr = len(in_specs), len(oshape), len(comm.ins), len(comm.out_shapes), len(scratch)

    def full(*refs):
        ins, cins = refs[:nin], refs[nin:nin + nci]
        outs, couts = refs[nin + nci:nin + nci + nout], refs[nin + nci + nout:nin + nci + nout + nco]
        scr = refs[nin + nci + nout + nco:nin + nci + nout + nco + nscr]
        send, recv = refs[-2:]
        ids = [pl.program_id(a) for a in range(len(grid))]
        first = functools.reduce(jnp.logical_and, [i == 0 for i in ids])
        last = functools.reduce(jnp.logical_and, [i == g - 1 for i, g in zip(ids, grid)])

        @pl.when(first)
        def _():
            comm.run(cins, couts, send, recv, True)

        body(*ins, *outs, *scr)

        @pl.when(last)
        def _():
            comm.run(cins, couts, send, recv, False)

    sems = pltpu.SemaphoreType.DMA((len(comm.items), 4))
    res = pl.pallas_call(
        full, grid=grid, in_specs=list(in_specs) + [ANY] * nci, out_specs=ospecs + [ANY] * nco,
        out_shape=oshape + comm.out_shapes, scratch_shapes=list(scratch) + [sems, sems],
        input_output_aliases={nin + i: nout + o for i, o in comm.aliases.items()},
        name=name, compiler_params=_cp(len(grid)),
    )(*args, *comm.ins)
    main = list(res[:nout])
    return (main if multi else main[0]), list(res[nout:])


def ffn_up(h, wg, wu, name, comm=None):
    S, D = h.shape
    FS = wg.shape[-1]
    tm = _row_tile(S)

    def body(h_ref, wg_ref, wu_ref, g_ref, u_ref, a_ref):
        hv = h_ref[...]
        g = _dot(hv, wg_ref[...])
        u = _dot(hv, wu_ref[...])
        g_ref[...] = g.astype(BF16)
        u_ref[...] = u.astype(BF16)
        a_ref[...] = (g * jax.nn.sigmoid(g) * u).astype(BF16)

    out = pl.BlockSpec((tm, FS), lambda k, i: (i, k))
    shard = lambda k, i: k
    return _call(
        body, grid=(NSH, S // tm),
        in_specs=[pl.BlockSpec((tm, D), lambda k, i: (i, 0)), _wspec(wg, shard), _wspec(wu, shard)],
        out_specs=[out] * 3, out_shape=[_sds((S, NSH * FS), BF16)] * 3,
        name=name, args=(h, wg, wu), comm=comm)


def col_mm(h, w, name, comm=None):
    S, D = h.shape
    NS = w.shape[-1]
    tm = _row_tile(S)

    def body(h_ref, w_ref, z_ref):
        z_ref[...] = _dot(h_ref[...], w_ref[...])

    return _call(
        body, grid=(NSH, S // tm),
        in_specs=[pl.BlockSpec((tm, D), lambda k, i: (i, 0)), _wspec(w, lambda k, i: k)],
        out_specs=pl.BlockSpec((tm, NS), lambda k, i: (i, k)), out_shape=_sds((S, NSH * NS), F32),
        name=name, args=(h, w), comm=comm)


def resid_mm(a, w, x, scale, name, comm=None):
    S, K = a.shape
    D = w.shape[-1]
    tm = _row_tile(S)
    tn = D // 2 if D % 256 == 0 else D

    def body(a_ref, w_ref, x_ref, o_ref):
        o_ref[...] = x_ref[...] + scale * _dot(a_ref[...], w_ref[...])

    out = pl.BlockSpec((tm, tn), lambda j, i: (i, j))
    return _call(
        body, grid=(D // tn, S // tm),
        in_specs=[pl.BlockSpec((tm, K), lambda j, i: (i, 0)), pl.BlockSpec((K, tn), lambda j, i: (0, j)), out],
        out_specs=out, out_shape=_sds((S, D), F32), name=name, args=(a, w.reshape(K, D), x), comm=comm)


def ffn_bwd_act(dxb, wd, g, u, name):
    S, D = dxb.shape
    FS = wd.shape[-2]
    tm = _row_tile(S)

    nsub = 2 if tm % 32 == 0 else 1

    def body(dx_ref, w_ref, g_ref, u_ref, dg_ref, du_ref):
        for hh in range(nsub):
            rows = slice(hh * (tm // nsub), (hh + 1) * (tm // nsub))
            da = 0.5 * _dot_nt(dx_ref[rows, :], w_ref[...])
            gv = g_ref[rows, :].astype(F32)
            uv = u_ref[rows, :].astype(F32)
            s = jax.nn.sigmoid(gv)
            du_ref[rows, :] = (da * gv * s).astype(BF16)
            dg_ref[rows, :] = (da * uv * s * (1.0 + gv * (1.0 - s))).astype(BF16)

    act = pl.BlockSpec((tm, FS), lambda k, i: (i, k))
    return pl.pallas_call(
        body, grid=(NSH, S // tm),
        in_specs=[pl.BlockSpec((tm, D), lambda k, i: (i, 0)), _wspec(wd, lambda k, i: k), act, act],
        out_specs=[act, act], out_shape=[_sds((S, NSH * FS), BF16)] * 2,
        name=name, compiler_params=_cp(2),
    )(dxb, wd, g, u)


def nt_col_mm(dxb, w, name):
    S, D = dxb.shape
    KS = w.shape[-2]
    tm = _row_tile(S)

    def body(dx_ref, w_ref, o_ref):
        o_ref[...] = _dot_nt(dx_ref[...], w_ref[...])

    return pl.pallas_call(
        body, grid=(NSH, S // tm),
        in_specs=[pl.BlockSpec((tm, D), lambda k, i: (i, 0)), _wspec(w, lambda k, i: k)],
        out_specs=pl.BlockSpec((tm, KS), lambda k, i: (i, k)), out_shape=_sds((S, NSH * KS), F32),
        name=name, compiler_params=_cp(2),
    )(dxb, w)


def wgrad(lhs, rhs, lhs_sharded, rhs_sharded, scale, tr, tc, name, comm=None):
    S = lhs.shape[0]
    R = lhs.shape[1] // (NSH if lhs_sharded else 1)
    C = rhs.shape[1] // (NSH if rhs_sharded else 1)
    ts = min(2048, S)
    nr, nc = R // tr, C // tc

    def body(l_ref, r_ref, o_ref):
        @pl.when(pl.program_id(3) == 0)
        def _():
            o_ref[...] = jnp.zeros_like(o_ref)

        o_ref[...] += scale * _dot_tn(l_ref[...], r_ref[...])

    lmap = (lambda k, a, b, s: (s, k * nr + a)) if lhs_sharded else (lambda k, a, b, s: (s, a))
    rmap = (lambda k, a, b, s: (s, k * nc + b)) if rhs_sharded else (lambda k, a, b, s: (s, b))
    return _call(
        body, grid=(NSH, nr, nc, S // ts),
        in_specs=[pl.BlockSpec((ts, tr), lmap), pl.BlockSpec((ts, tc), rmap)],
        out_specs=pl.BlockSpec((None, tr, tc), lambda k, a, b, s: (k, a, b)),
        out_shape=_sds((NSH, R, C), F32), name=name, args=(lhs, rhs), comm=comm)


def bwd_dh(dys, ws, x, g, dxin, name, comm=None, after=None):
    S, D = x.shape
    NS = ws[0].shape[-1]
    tm = _row_tile(S)
    nj = len(dys)

    def mm_body(*refs):
        dy_refs, w_refs, dh_ref = refs[:nj], refs[nj:2 * nj], refs[2 * nj]

        @pl.when(pl.program_id(1) == 0)
        def _():
            dh_ref[...] = jnp.zeros_like(dh_ref)

        for dy_ref, w_ref in zip(dy_refs, w_refs):
            dh_ref[...] += _dot_nt(dy_ref[...], w_ref[...])

    dh, comm_out = _call(
        mm_body, grid=(S // tm, NSH),
        in_specs=[pl.BlockSpec((tm, NS), lambda i, k: (i, k))] * nj + [_wspec(w, lambda i, k: k) for w in ws],
        out_specs=pl.BlockSpec((tm, D), lambda i, k: (i, 0)), out_shape=_sds((S, D), F32),
        name=name + "_mm", args=(*dys, *ws), comm=comm)

    def norm_body(dh_ref, x_ref, g_ref, dxin_ref, dx_ref, dxb_ref, dg_ref):
        dx, dg = _rms_bwd_tile(x_ref[...], g_ref[...], dh_ref[...])
        tot = dxin_ref[...] + dx
        dx_ref[...] = tot
        dxb_ref[...] = tot.astype(BF16)

        @pl.when(pl.program_id(0) == 0)
        def _():
            dg_ref[...] = jnp.zeros_like(dg_ref)

        dg_ref[...] += dg

    row = pl.BlockSpec((tm, D), lambda i: (i, 0))
    vec = pl.BlockSpec((1, D), lambda i: (0, 0))
    comm2 = after(comm_out) if after is not None else None
    (dx, dxb, dgain), comm2_out = _call(
        norm_body, grid=(S // tm,), in_specs=[row, row, vec, row], out_specs=[row, row, vec],
        out_shape=[_sds((S, D), F32), _sds((S, D), BF16), _sds((1, D), F32)],
        name=name + "_norm", args=(dh, x, g.reshape(1, D), dxin), comm=comm2)
    return dx, dxb, dgain, comm_out, comm2_out


PAIRS_PER_STEP = {1: 8, 4: 1, 16: 1}
ZQ, ZK, ZV = 0, D_ATTN // 128, 2 * D_ATTN // 128


def _band_valid(n):
    qi = lax.broadcasted_iota(jnp.int32, (BLK, 2 * BLK), 0)
    kj = lax.broadcasted_iota(jnp.int32, (BLK, 2 * BLK), 1)
    dist = qi + BLK - kj
    return (dist >= 0) & (dist <= SPAN) & ((kj >= BLK) | (n > 0))


def _residue_perm(d, transpose):
    q = 128 // d
    a = lax.broadcasted_iota(jnp.int32, (128, 128), 1 if transpose else 0)
    b = lax.broadcasted_iota(jnp.int32, (128, 128), 0 if transpose else 1)
    return (b == (a % q) * d + a // q).astype(BF16)


def _perm_apply(perm, x, n_terms):
    out, rest = None, x
    for t in range(n_terms):
        term = rest.astype(BF16)
        out = _dot(perm, term) if out is None else out + _dot(perm, term)
        if t + 1 < n_terms:
            rest = rest - term.astype(F32)
    return out


def _split_residues(src_ref, dst, d, pp, row0=0, n_terms=1):
    if d == 1:
        for p in range(pp):
            dst[p, row0:row0 + BLK, :] = src_ref[:, p * 128:(p + 1) * 128]
        return
    perm, q = _residue_perm(d, False), 128 // d
    for c in range(d):
        t = _perm_apply(perm, src_ref[c * 128:(c + 1) * 128, :], n_terms)
        for r in range(d):
            dst[r, row0 + c * q:row0 + (c + 1) * q, :] = t[r * q:(r + 1) * q]


def _merge_residues(bufs, d, pp, c):
    q = 128 // d
    t = jnp.concatenate([bufs[r, c * q:(c + 1) * q, :] for r in range(d)], axis=0)
    return _perm_apply(_residue_perm(d, True), t, 3)


def _lane_pick(stat, lane, idx):
    return jnp.sum(jnp.where(lane == idx, stat, 0.0), axis=-1, keepdims=True)


def _residue_view(dilation):
    return 1, dilation


def _view_shape(shape, dv):
    return (shape[0] // dv, dv, shape[1]) if dv > 1 else tuple(shape)


def _viewed(a, dv):
    return a.reshape(_view_shape(a.shape, dv))


def _view_spec(dv, rows, width, row_of, col_of):
    if dv > 1:
        return pl.BlockSpec((rows, None, width), lambda r, a, b: (row_of(a, b), r, col_of(a, b)))
    return pl.BlockSpec((rows, width), lambda r, a, b: (row_of(a, b), col_of(a, b)))


def attn_fwd(z, dilation, name):
    S = z.shape[0]
    dv, d = _residue_view(dilation)
    R = BLK * d
    nb = S // (BLK * dilation)
    pp = PAIRS_PER_STEP[d]
    G = d * pp
    W = 128 * pp

    def body(q_ref, kc_ref, kp_ref, vc_ref, vp_ref, o_ref, lse_ref, qs, ks, vs, os_, ls):
        n, hb = pl.program_id(1), pl.program_id(2)
        valid = _band_valid(n)
        lane = lax.broadcasted_iota(jnp.int32, (1, 128), 1)
        _split_residues(q_ref, qs, d, pp)
        _split_residues(kp_ref, ks, d, pp)
        _split_residues(kc_ref, ks, d, pp, BLK)
        _split_residues(vp_ref, vs, d, pp)
        _split_residues(vc_ref, vs, d, pp, BLK)

        def step(j, carry):
            q2, k2, v2 = qs[j].astype(BF16), ks[j].astype(BF16), vs[j].astype(BF16)
            pair = hb * pp + j % pp
            o2 = jnp.zeros((BLK, 128), F32)
            stat = jnp.zeros((BLK, 128), F32)
            for h in range(2):
                hm = (lane < HEAD_DIM) if h == 0 else (lane >= HEAD_DIM)
                qm = jnp.where(hm, q2, jnp.zeros_like(q2))
                vm = jnp.where(hm, v2, jnp.zeros_like(v2))
                s = jnp.where(valid, _dot_nt(qm, k2) * ATTN_SCALE, NEG_INF)
                m = jnp.max(s, axis=-1, keepdims=True)
                p = jnp.exp(s - m)
                lsum = jnp.sum(p, axis=-1, keepdims=True)
                o2 = o2 + _dot(p.astype(BF16), vm) / lsum
                stat = jnp.where(lane == 2 * pair + h, m + jnp.log(lsum), stat)
            os_[j] = o2
            ls[j] = stat
            return carry

        lax.fori_loop(0, G, step, 0, unroll=min(G, 8))

        @pl.when(hb == 0)
        def _():
            lse_ref[...] = jnp.zeros_like(lse_ref)

        if d == 1:
            acc = lse_ref[...]
            for p in range(pp):
                o_ref[:, p * 128:(p + 1) * 128] = os_[p]
                acc = acc + ls[p]
            lse_ref[...] = acc
        else:
            for c in range(d):
                rows = slice(c * 128, (c + 1) * 128)
                o_ref[rows, :] = _merge_residues(os_, d, pp, c)
                lse_ref[rows, :] += _merge_residues(ls, d, pp, c)

    cur = lambda c: _view_spec(dv, R, W, lambda n, hb: n, lambda n, hb: c // pp + hb)
    prev = lambda c: _view_spec(dv, R, W, lambda n, hb: jnp.maximum(n - 1, 0), lambda n, hb: c // pp + hb)
    zv = _viewed(z, dv)
    o, lse = pl.pallas_call(
        body, grid=(dv, nb, N_PAIR // pp),
        in_specs=[cur(ZQ), cur(ZK), prev(ZK), cur(ZV), prev(ZV)],
        out_specs=[_view_spec(dv, R, W, lambda n, hb: n, lambda n, hb: hb), _view_spec(dv, R, 128, lambda n, hb: n, lambda n, hb: 0)],
        out_shape=[_sds(_view_shape((S, D_ATTN), dv), F32), _sds(_view_shape((S, 128), dv), F32)],
        scratch_shapes=[pltpu.VMEM((G, BLK, 128), F32), pltpu.VMEM((G, 2 * BLK, 128), F32), pltpu.VMEM((G, 2 * BLK, 128), F32),
                        pltpu.VMEM((G, BLK, 128), F32), pltpu.VMEM((G, BLK, 128), F32)],
        name=name, compiler_params=_cp(3),
    )(zv, zv, zv, zv, zv)
    return o.reshape(S, D_ATTN), lse.reshape(S, 128)


def _pair_weights(w, lane):
    return [jnp.where(lane < HEAD_DIM, w[:, 2 * hp:2 * hp + 1], w[:, 2 * hp + 1:2 * hp + 2]) for hp in range(N_PAIR)]


def attn_merge(os_, lses, name):
    S = os_[0].shape[0]
    tm = _row_tile(S)
    npat = len(os_)

    def body(*refs):
        o_refs, l_refs = refs[:npat], refs[npat:2 * npat]
        y_ref, yb_ref, lt_ref = refs[2 * npat:]
        lane = lax.broadcasted_iota(jnp.int32, (1, 128), 1)
        ls = [r[...] for r in l_refs]
        mx = functools.reduce(jnp.maximum, ls)
        es = [jnp.exp(v - mx) for v in ls]
        den = functools.reduce(jnp.add, es)
        lt_ref[...] = mx + jnp.log(den)
        ws = [_pair_weights(e / den, lane) for e in es]
        for hp in range(N_PAIR):
            sl = slice(hp * 128, (hp + 1) * 128)
            y = ws[0][hp] * o_refs[0][:, sl]
            for p in range(1, npat):
                y = y + ws[p][hp] * o_refs[p][:, sl]
            y_ref[:, sl] = y
            yb_ref[:, sl] = y.astype(BF16)

    big = pl.BlockSpec((tm, D_ATTN), lambda i: (i, 0))
    st = pl.BlockSpec((tm, 128), lambda i: (i, 0))
    return pl.pallas_call(
        body, grid=(S // tm,), in_specs=[big] * npat + [st] * npat, out_specs=[big, big, st],
        out_shape=[_sds((S, D_ATTN), F32), _sds((S, D_ATTN), BF16), _sds((S, 128), F32)],
        name=name, compiler_params=_cp(1),
    )(*os_, *lses)


def attn_bwd(z, dycat, y, lse_tot, dilation, name):
    S = z.shape[0]
    dv, d = _residue_view(dilation)
    R = BLK * d
    nb = S // (BLK * dilation)
    pp = PAIRS_PER_STEP[d]
    G = d * pp
    W = 128 * pp

    def body(q_ref, kc_ref, kp_ref, vc_ref, vp_ref, dy_ref, y_ref, l_ref, dq_ref, dk_ref, dv_ref,
             qs, ks, vs, dys, ys, ls, dqs, dks, dvs, ck, cv):
        hb, n = pl.program_id(1), pl.program_id(2)

        def store_rows(ref, buf):
            if d == 1:
                for p in range(pp):
                    ref[:, p * 128:(p + 1) * 128] = buf[p]
            else:
                for c in range(d):
                    ref[c * 128:(c + 1) * 128, :] = _merge_residues(buf, d, pp, c)

        @pl.when(n < nb)
        def _():
            valid = _band_valid(n)
            first = n == 0
            lane = lax.broadcasted_iota(jnp.int32, (1, 128), 1)
            _split_residues(q_ref, qs, d, pp)
            _split_residues(kp_ref, ks, d, pp)
            _split_residues(kc_ref, ks, d, pp, BLK)
            _split_residues(vp_ref, vs, d, pp)
            _split_residues(vc_ref, vs, d, pp, BLK)
            _split_residues(dy_ref, dys, d, pp, n_terms=3)
            _split_residues(y_ref, ys, d, pp, n_terms=3)
            _split_residues(l_ref, ls, d, 1, n_terms=3)

            def step(j, carry):
                q2, k2, v2, dy2, y2 = qs[j].astype(BF16), ks[j].astype(BF16), vs[j].astype(BF16), dys[j], ys[j]
                stat = ls[j // pp]
                pair = hb * pp + j % pp
                dq2 = jnp.zeros((BLK, 128), F32)
                dk2 = jnp.zeros((2 * BLK, 128), F32)
                dv2 = jnp.zeros((2 * BLK, 128), F32)
                for h in range(2):
                    hm = (lane < HEAD_DIM) if h == 0 else (lane >= HEAD_DIM)
                    qm = jnp.where(hm, q2, jnp.zeros_like(q2))
                    km = jnp.where(hm, k2, jnp.zeros_like(k2))
                    dym = jnp.where(hm, dy2, 0.0)
                    dymb = dym.astype(BF16)
                    s = _dot_nt(qm, k2) * ATTN_SCALE
                    p = jnp.where(valid, jnp.exp(s - _lane_pick(stat, lane, 2 * pair + h)), 0.0)
                    dp = _dot_nt(dymb, v2)
                    delta = jnp.sum(dym * y2, axis=-1, keepdims=True)
                    ds = (p * (dp - delta) * ATTN_SCALE).astype(BF16)
                    dq2 = dq2 + _dot(ds, km)
                    dk2 = dk2 + _dot_tn(ds, qm)
                    dv2 = dv2 + _dot_tn(p.astype(BF16), dymb)
                dqs[j] = dq2
                dks[j] = jnp.where(first, 0.0, ck[j]) + dk2[:BLK]
                dvs[j] = jnp.where(first, 0.0, cv[j]) + dv2[:BLK]
                ck[j] = dk2[BLK:]
                cv[j] = dv2[BLK:]
                return carry

            lax.fori_loop(0, G, step, 0, unroll=min(G, 8))
            store_rows(dq_ref, dqs)
            store_rows(dk_ref, dks)
            store_rows(dv_ref, dvs)

        @pl.when(n == nb)
        def _():
            store_rows(dk_ref, ck)
            store_rows(dv_ref, cv)

    cur = lambda n: jnp.minimum(n, nb - 1)
    prev = lambda n: jnp.clip(n - 1, 0, nb - 1)
    zb = lambda c, p: _view_spec(dv, R, W, lambda hb, n: (prev(n) if p else cur(n)), lambda hb, n: c // pp + hb)
    big_cur = _view_spec(dv, R, W, lambda hb, n: cur(n), lambda hb, n: hb)
    big_lag = _view_spec(dv, R, W, lambda hb, n: jnp.maximum(n - 1, 0), lambda hb, n: hb)
    buf = lambda rows, dt: pltpu.VMEM((G, rows, 128), dt)
    zv = _viewed(z, dv)
    outs = pl.pallas_call(
        body, grid=(dv, N_PAIR // pp, nb + 1),
        in_specs=[zb(ZQ, False), zb(ZK, False), zb(ZK, True), zb(ZV, False), zb(ZV, True),
                  big_cur, big_cur, _view_spec(dv, R, 128, lambda hb, n: cur(n), lambda hb, n: 0)],
        out_specs=[big_cur, big_lag, big_lag],
        out_shape=[_sds(_view_shape((S, D_ATTN), dv), F32)] * 3,
        scratch_shapes=[buf(BLK, F32), buf(2 * BLK, F32), buf(2 * BLK, F32), buf(BLK, F32), buf(BLK, F32),
                        pltpu.VMEM((d, BLK, 128), F32), buf(BLK, F32), buf(BLK, F32), buf(BLK, F32), buf(BLK, F32), buf(BLK, F32)],
        name=name, compiler_params=_cp(3),
    )(zv, zv, zv, zv, zv, _viewed(dycat, dv), _viewed(y, dv), _viewed(lse_tot, dv))
    return tuple(o.reshape(S, D_ATTN) for o in outs)


ZC_GB, ZC_GC, ZC_CI, ZC_PI = 6, 7, 8, 9


def _pool_counts(i, tb):
    pos = lax.broadcasted_iota(jnp.int32, (tb, D_POOL), 0) + i * tb + 1
    grp = lax.broadcasted_iota(jnp.int32, (tb, D_POOL), 1) // POOL_GROUP
    win = jnp.where(grp == 0, POOL_WINDOWS[0], jnp.where(grp == 1, POOL_WINDOWS[1],
                    jnp.where(grp == 2, POOL_WINDOWS[2], POOL_WINDOWS[3])))
    return jnp.minimum(pos, win).astype(F32), grp


def _select_group(stages, grp):
    return jnp.where(grp == 0, stages[0], jnp.where(grp == 1, stages[1], jnp.where(grp == 2, stages[2], stages[3])))


def _causal_window_sums(x2):
    s1 = x2 + pltpu.roll(x2, 1, 0)
    s2 = s1 + pltpu.roll(s1, 2, 0)
    s3 = s2 + pltpu.roll(s2, 4, 0)
    s4 = s3 + pltpu.roll(s3, 8, 0)
    return [s1, s2, s3, s4]


def _anticausal_window_sums(x2):
    n = x2.shape[0]
    s1 = x2 + pltpu.roll(x2, n - 1, 0)
    s2 = s1 + pltpu.roll(s1, n - 2, 0)
    s3 = s2 + pltpu.roll(s2, n - 4, 0)
    s4 = s3 + pltpu.roll(s3, n - 8, 0)
    return [s1, s2, s3, s4]


def _pooled(p_prev, p_cur, i, tb):
    x2 = jnp.concatenate([jnp.where(i > 0, p_prev, 0.0), p_cur], axis=0)
    count, grp = _pool_counts(i, tb)
    win_sum = _select_group([s[tb:] for s in _causal_window_sums(x2)], grp)
    return win_sum / count - p_cur, count, grp


def _pool_mm(v, pw_ref, nt):
    outs = []
    for gi in range(len(POOL_WINDOWS)):
        sl = slice(gi * POOL_GROUP, (gi + 1) * POOL_GROUP)
        outs.append(_dot_nt(v[:, sl], pw_ref[gi]) if nt else _dot(v[:, sl], pw_ref[gi]))
    return jnp.concatenate(outs, axis=1)


def conv_pool_fwd(z, conv_w, pool_w, pool_scale, name):
    S = z.shape[0]
    tb = min(256, S)

    def body(gb_ref, gc_ref, gcp_ref, ci_ref, cip_ref, pi_ref, pip_ref, cw_ref, pw_ref, ps_ref, yc_ref, yp_ref):
        i = pl.program_id(0)
        u2 = jnp.concatenate([jnp.where(i > 0, gcp_ref[...] * cip_ref[...], 0.0), gc_ref[...] * ci_ref[...]], axis=0)
        conv = cw_ref[0:1, :] * pltpu.roll(u2, 2, 0) + cw_ref[1:2, :] * pltpu.roll(u2, 1, 0) + cw_ref[2:3, :] * u2
        yc_ref[...] = (gb_ref[...] * conv[tb:]).astype(BF16)
        pooled, _, _ = _pooled(pip_ref[...], pi_ref[...], i, tb)
        yp_ref[...] = (_pool_mm(pooled.astype(BF16), pw_ref, False) * ps_ref[...]).astype(BF16)

    cur = lambda c: pl.BlockSpec((tb, 512), lambda i: (i, c))
    prev = lambda c: pl.BlockSpec((tb, 512), lambda i: (jnp.maximum(i - 1, 0), c))
    full = lambda a: pl.BlockSpec(a.shape, lambda i: (0,) * a.ndim)
    out = pl.BlockSpec((tb, 512), lambda i: (i, 0))
    return pl.pallas_call(
        body, grid=(S // tb,),
        in_specs=[cur(ZC_GB), cur(ZC_GC), prev(ZC_GC), cur(ZC_CI), prev(ZC_CI), cur(ZC_PI), prev(ZC_PI),
                  full(conv_w), full(pool_w), full(pool_scale)],
        out_specs=[out, out], out_shape=[_sds((S, 512), BF16)] * 2, name=name, compiler_params=_cp(1),
    )(z, z, z, z, z, z, z, conv_w, pool_w, pool_scale)


def conv_pool_bwd(z, dycat, conv_w, pool_w, pool_scale, name):
    S = z.shape[0]
    tb = min(256, S)
    nblk = S // tb
    ng = len(POOL_WINDOWS)

    def body(gb_ref, gbn_ref, gc_ref, gcp_ref, ci_ref, cip_ref, pi_ref, pip_ref, dyc_ref, dycn_ref, dyp_ref, dypn_ref,
             cw_ref, pw_ref, ps_ref, dz_ref, dcw_ref, dpw_ref, dps_ref):
        i = pl.program_id(0)
        last = i == nblk - 1

        @pl.when(i == 0)
        def _():
            dcw_ref[...] = jnp.zeros_like(dcw_ref)
            dpw_ref[...] = jnp.zeros_like(dpw_ref)
            dps_ref[...] = jnp.zeros_like(dps_ref)

        gc, ci = gc_ref[...], ci_ref[...]
        u2 = jnp.concatenate([jnp.where(i > 0, gcp_ref[...] * cip_ref[...], 0.0), gc * ci], axis=0)
        um2, um1, u0 = pltpu.roll(u2, 2, 0)[tb:], pltpu.roll(u2, 1, 0)[tb:], u2[tb:]
        conv = cw_ref[0:1, :] * um2 + cw_ref[1:2, :] * um1 + cw_ref[2:3, :] * u0
        dyc = dyc_ref[...]
        dconv = dyc * gb_ref[...]
        dc2 = jnp.concatenate([dconv, jnp.where(last, 0.0, dycn_ref[...] * gbn_ref[...])], axis=0)
        du = (cw_ref[2:3, :] * dconv + cw_ref[1:2, :] * pltpu.roll(dc2, 2 * tb - 1, 0)[:tb]
              + cw_ref[0:1, :] * pltpu.roll(dc2, 2 * tb - 2, 0)[:tb])
        dz_ref[:, 0:512] = (dyc * conv).astype(BF16)
        dz_ref[:, 512:1024] = (du * ci).astype(BF16)
        dz_ref[:, 1024:1536] = (du * gc).astype(BF16)
        dcw_ref[0:1, :] += jnp.sum(dconv * um2, axis=0, keepdims=True)
        dcw_ref[1:2, :] += jnp.sum(dconv * um1, axis=0, keepdims=True)
        dcw_ref[2:3, :] += jnp.sum(dconv * u0, axis=0, keepdims=True)

        pooled, count, grp = _pooled(pip_ref[...], pi_ref[...], i, tb)
        pooled_b = pooled.astype(BF16)
        t = _pool_mm(pooled_b, pw_ref, False)
        dyp, ps = dyp_ref[...], ps_ref[...]
        dps_ref[...] += jnp.sum(dyp * t, axis=0, keepdims=True)
        dt_b = (dyp * ps).astype(BF16)
        for gi in range(ng):
            sl = slice(gi * POOL_GROUP, (gi + 1) * POOL_GROUP)
            dpw_ref[gi] += _dot_tn(pooled_b[:, sl], dt_b[:, sl])
        dpooled = _pool_mm(dt_b, pw_ref, True)
        dpooled_n = _pool_mm((dypn_ref[...] * ps).astype(BF16), pw_ref, True)
        count_n, _ = _pool_counts(i + 1, tb)
        dq2 = jnp.concatenate([dpooled / count, jnp.where(last, 0.0, dpooled_n / count_n)], axis=0)
        lead = _select_group([s[:tb] for s in _anticausal_window_sums(dq2)], grp)
        dz_ref[:, 1536:2048] = (lead - dpooled).astype(BF16)

    cur = lambda c: pl.BlockSpec((tb, 512), lambda i: (i, c))
    prev = lambda c: pl.BlockSpec((tb, 512), lambda i: (jnp.maximum(i - 1, 0), c))
    nxt = lambda c: pl.BlockSpec((tb, 512), lambda i: (jnp.minimum(i + 1, nblk - 1), c))
    full = lambda a: pl.BlockSpec(a.shape, lambda i: (0,) * a.ndim)
    yc_c, yp_c = D_ATTN // 512, D_ATTN // 512 + 1
    return pl.pallas_call(
        body, grid=(nblk,),
        in_specs=[cur(ZC_GB), nxt(ZC_GB), cur(ZC_GC), prev(ZC_GC), cur(ZC_CI), prev(ZC_CI), cur(ZC_PI), prev(ZC_PI),
                  cur(yc_c), nxt(yc_c), cur(yp_c), nxt(yp_c), full(conv_w), full(pool_w), full(pool_scale)],
        out_specs=[pl.BlockSpec((tb, 2048), lambda i: (i, 0)), pl.BlockSpec((3, 512), lambda i: (0, 0)),
                   pl.BlockSpec((ng, POOL_GROUP, POOL_GROUP), lambda i: (0, 0, 0)), pl.BlockSpec((1, 512), lambda i: (0, 0))],
        out_shape=[_sds((S, 2048), BF16), _sds((3, 512), F32), _sds((ng, POOL_GROUP, POOL_GROUP), F32), _sds((1, 512), F32)],
        name=name, compiler_params=_cp(1),
    )(z, z, z, z, z, z, z, z, dycat, dycat, dycat, dycat, conv_w, pool_w, pool_scale)


def assemble_dz(parts_q, parts_k, parts_v, dcp, name):
    S = dcp.shape[0]
    tm = _row_tile(S)
    npat = len(parts_q)

    def body(*refs):
        dz_ref = refs[-1]
        dcp_ref = refs[-2]
        for j in range(3):
            acc = refs[j * npat][...]
            for p in range(1, npat):
                acc = acc + refs[j * npat + p][...]
            dz_ref[:, j * D_ATTN:(j + 1) * D_ATTN] = acc.astype(BF16)
        dz_ref[:, 3 * D_ATTN:] = dcp_ref[...]

    big = pl.BlockSpec((tm, D_ATTN), lambda i: (i, 0))
    return pl.pallas_call(
        body, grid=(S // tm,), in_specs=[big] * (3 * npat) + [pl.BlockSpec((tm, D_IN - 3 * D_ATTN), lambda i: (i, 0))],
        out_specs=pl.BlockSpec((tm, D_IN), lambda i: (i, 0)), out_shape=_sds((S, D_IN), BF16),
        name=name, compiler_params=_cp(1),
    )(*parts_q, *parts_k, *parts_v, dcp)


def adamw(w, g, m, v, name):
    shape = w.shape
    cols = shape[-1]
    rows = w.size // cols
    tr = rows
    for cand in (256, 128, 64, 32, 16, 8):
        if rows % cand == 0:
            tr = cand
            break
    c1 = 1.0 - ADAM_B1 ** ADAM_STEP
    c2 = 1.0 - ADAM_B2 ** ADAM_STEP

    def body(w_ref, g_ref, m_ref, v_ref, d_ref, mo_ref, vo_ref):
        gv = g_ref[...]
        mn = ADAM_B1 * m_ref[...] + (1.0 - ADAM_B1) * gv
        vn = ADAM_B2 * v_ref[...] + (1.0 - ADAM_B2) * (gv * gv)
        d_ref[...] = -ADAM_LR * ((mn / c1) / (jnp.sqrt(vn / c2) + ADAM_EPS) + ADAM_WD * w_ref[...])
        mo_ref[...] = mn
        vo_ref[...] = vn

    blk = pl.BlockSpec((tr, cols), lambda i: (i, 0))
    outs = pl.pallas_call(
        body, grid=(rows // tr,), in_specs=[blk] * 4, out_specs=[blk] * 3,
        out_shape=[_sds((rows, cols), F32)] * 3, name=name, compiler_params=_cp(1),
    )(*(a.reshape(rows, cols) for a in (w, g, m, v)))
    return tuple(o.reshape(shape) for o in outs)


ANY = pl.BlockSpec(memory_space=pl.ANY)


def _place():
    x, y, c = lax.axis_index("x"), lax.axis_index("y"), lax.axis_index("c")
    chips = [(1 - x, y), (x, 1 - y), (1 - x, 1 - y)]
    return x, y, c, chips


def gather_weights(ws):
    nw = len(ws)
    split = [w.ndim == 2 and w.shape[0] % 32 == 0 for w in ws]

    def body(*refs):
        ins, outs = refs[:nw], refs[nw:2 * nw]
        send, recv = refs[2 * nw:]
        x, y, c, chips = _place()
        me, sib = 2 * x + y, (x, y, 1 - c)

        def half(j, k, hc):
            if not split[j]:
                return outs[j].at[k]
            ha = ws[j].shape[0] // 2
            return outs[j].at[k, pl.ds(hc * ha, ha), :]

        def rcopy(j, s, src, dst, to):
            return pltpu.make_async_remote_copy(src_ref=src, dst_ref=dst, send_sem=send.at[j, s], recv_sem=recv.at[j, s],
                                                device_id=to, device_id_type=MESH)

        first = [rcopy(j, 6, ins[j], outs[j].at[me], sib) for j in range(nw)]
        for j in range(nw):
            ha = ws[j].shape[0] // 2
            mine = ins[j].at[pl.ds(c * ha, ha), :] if split[j] else ins[j]
            for s, (px, py) in enumerate(chips):
                first.append(rcopy(j, s, mine, half(j, me, c), (px, py, c)))
        for cp in first:
            cp.start()
        passed = []
        for j in range(nw):
            for s, (px, py) in enumerate(chips):
                blk = half(j, 2 * px + py, c)
                rcopy(j, s, blk, blk, (px, py, c)).wait_recv()
                if split[j]:
                    fwd = rcopy(j, 3 + s, blk, blk, sib)
                    fwd.start()
                    passed.append(fwd)
        for j in range(nw):
            for s, (px, py) in enumerate(chips):
                if split[j]:
                    blk = half(j, 2 * px + py, 1 - c)
                    rcopy(j, 3 + s, blk, blk, sib).wait_recv()
        for j in range(nw):
            rcopy(j, 6, ins[j], outs[j].at[me], sib).wait_recv()
        for cp in first + passed:
            cp.wait_send()

    return pl.pallas_call(
        body, in_specs=[ANY] * nw, out_specs=[ANY] * nw,
        out_shape=[_sds((NSH,) + w.shape, w.dtype) for w in ws],
        scratch_shapes=[pltpu.SemaphoreType.DMA((nw, 7)), pltpu.SemaphoreType.DMA((nw, 7))],
        name="gather_weights",
    )(*ws)


def pair_sum(g, got, half_index, name):
    _, A, B = g.shape
    ha = A // 2
    tr = ha
    for cand in (512, 256, 128, 64):
        if ha % cand == 0:
            tr = cand
            break
    nt = ha // tr

    def body(c_ref, g_ref, r_ref, o_ref):
        del c_ref
        o_ref[...] = (g_ref[...] + r_ref[...]).astype(BF16)

    return pl.pallas_call(
        body,
        grid_spec=pltpu.PrefetchScalarGridSpec(
            num_scalar_prefetch=1, grid=(NSH, nt),
            in_specs=[pl.BlockSpec((None, tr, B), lambda k, t, c: (k, c[0] * nt + t, 0)),
                      pl.BlockSpec((None, tr, B), lambda k, t, c: (k, t, 0))],
            out_specs=pl.BlockSpec((None, tr, B), lambda k, t, c: (k, t, 0))),
        out_shape=_sds((NSH, ha, B), BF16), name=name, compiler_params=_cp(2),
    )(half_index, g, got)


def _half_tile(ha):
    for cand in (512, 256, 128, 64):
        if ha % cand == 0:
            return cand
    return ha


def chip_sum(partials, slots, chip_index, name):
    depth = len(partials)
    _, ha, B = partials[0].shape
    tr = _half_tile(ha)
    nt = ha // tr

    def body(me_ref, *refs):
        p_refs, s_refs, o_ref = refs[:depth], refs[depth:depth + depth * NSH], refs[depth + depth * NSH]
        l = pl.program_id(0)
        for ll in range(depth):
            @pl.when(l == ll)
            def _(ll=ll):
                own = p_refs[ll][...].astype(F32)
                acc = jnp.where(me_ref[0] == 0, own, s_refs[ll * NSH][...].astype(F32))
                for k in range(1, NSH):
                    acc = acc + jnp.where(me_ref[0] == k, own, s_refs[ll * NSH + k][...].astype(F32))
                o_ref[...] = acc

    def frozen(ll):
        return lambda l, t: jnp.where(l == ll, t, jnp.where(l < ll, 0, nt - 1))

    def slot(k):
        return lambda me: jnp.where(me[0] == k, (k + 1) % NSH, k)

    in_specs = [pl.BlockSpec((None, tr, B), lambda l, t, me, ll=ll: (me[0], frozen(ll)(l, t), 0)) for ll in range(depth)]
    in_specs += [pl.BlockSpec((None, tr, B), lambda l, t, me, ll=ll, k=k: (slot(k)(me), frozen(ll)(l, t), 0))
                 for ll in range(depth) for k in range(NSH)]
    return pl.pallas_call(
        body,
        grid_spec=pltpu.PrefetchScalarGridSpec(
            num_scalar_prefetch=1, grid=(depth, nt), in_specs=in_specs,
            out_specs=pl.BlockSpec((None, tr, B), lambda l, t, me: (l, t, 0))),
        out_shape=_sds((depth, ha, B), F32), name=name, compiler_params=_cp(2),
    )(chip_index, *partials, *[s for s in slots for _ in range(NSH)])


def sibling_swap(mine):
    n = len(mine)

    def body(*refs):
        ins, outs = refs[:n], refs[n:2 * n]
        send, recv = refs[2 * n:]
        x, y, c, _ = _place()
        cps = [pltpu.make_async_remote_copy(src_ref=ins[j], dst_ref=outs[j], send_sem=send.at[j], recv_sem=recv.at[j],
                                            device_id=(x, y, 1 - c), device_id_type=MESH) for j in range(n)]
        for cp in cps:
            cp.start()
        for cp in cps:
            cp.wait()

    return pl.pallas_call(
        body, in_specs=[ANY] * n, out_specs=[ANY] * n, out_shape=[_sds(m.shape, m.dtype) for m in mine],
        scratch_shapes=[pltpu.SemaphoreType.DMA((n,)), pltpu.SemaphoreType.DMA((n,))], name="sibling_swap",
    )(*mine)


def adamw_halves(w, mine, theirs, m, v, core_index, name):
    depth, A, B = w.shape
    ha = A // 2
    tr = _half_tile(ha)
    while tr * B * 4 > 2 ** 20 and tr % 16 == 0:
        tr //= 2
    nt = ha // tr
    c1 = 1.0 - ADAM_B1 ** ADAM_STEP
    c2 = 1.0 - ADAM_B2 ** ADAM_STEP

    def body(c_ref, w_ref, a_ref, b_ref, m_ref, v_ref, g_ref, d_ref, mo_ref, vo_ref):
        gv = jnp.where(pl.program_id(1) == c_ref[0], a_ref[...], b_ref[...])
        mn = ADAM_B1 * m_ref[...] + (1.0 - ADAM_B1) * gv
        vn = ADAM_B2 * v_ref[...] + (1.0 - ADAM_B2) * (gv * gv)
        g_ref[...] = gv
        d_ref[...] = -ADAM_LR * ((mn / c1) / (jnp.sqrt(vn / c2) + ADAM_EPS) + ADAM_WD * w_ref[...])
        mo_ref[...] = mn
        vo_ref[...] = vn

    full = pl.BlockSpec((None, tr, B), lambda l, h, t, c: (l, h * nt + t, 0))
    a_spec = pl.BlockSpec((None, tr, B), lambda l, h, t, c: (l, jnp.where(h == c[0], t, 0), 0))
    b_spec = pl.BlockSpec((None, tr, B), lambda l, h, t, c: (l, jnp.where(h == c[0], 0, t), 0))
    return pl.pallas_call(
        body,
        grid_spec=pltpu.PrefetchScalarGridSpec(
            num_scalar_prefetch=1, grid=(depth, 2, nt), in_specs=[full, a_spec, b_spec, full, full], out_specs=[full] * 4),
        out_shape=[_sds(w.shape, F32)] * 4, name=name, compiler_params=_cp(3),
    )(core_index, w, mine, theirs, m, v)


def small_all_reduce(v):
    R = v.shape[0]

    def body(v_ref, o_ref, slots, send, recv):
        x, y, c, _ = _place()
        me = 4 * x + 2 * y + c
        slots[me] = v_ref[...]
        cps = []
        for m in range(1, 8):
            mx, my, mc = (m >> 2) & 1, (m >> 1) & 1, m & 1
            cps.append(pltpu.make_async_remote_copy(
                src_ref=v_ref, dst_ref=slots.at[me], send_sem=send.at[m - 1], recv_sem=recv.at[m - 1],
                device_id=(x ^ mx, y ^ my, c ^ mc), device_id_type=MESH))
        for cp in cps:
            cp.start()
        for cp in cps:
            cp.wait()
        acc = slots[0]
        for d in range(1, 8):
            acc = acc + slots[d]
        o_ref[...] = acc

    vm = pl.BlockSpec(memory_space=pltpu.VMEM)
    return pl.pallas_call(
        body, in_specs=[vm], out_specs=vm, out_shape=_sds((R, 128), F32),
        scratch_shapes=[pltpu.VMEM((8, R, 128), F32), pltpu.SemaphoreType.DMA((7,)), pltpu.SemaphoreType.DMA((7,))],
        name="small_all_reduce",
    )(v)


BIG = ("ffn1_w_gate", "ffn1_w_up", "ffn1_w_down", "w_in", "w_out", "ffn2_w_gate", "ffn2_w_up", "ffn2_w_down")
SMALL = ("ffn1_norm", "mix_norm", "conv_w", "pool_w", "pool_scale", "ffn2_norm", "final_norm")
WEIGHTS = ("ffn1_norm", "ffn1_w_gate", "ffn1_w_up", "ffn1_w_down", "mix_norm", "w_in", "conv_w", "pool_w", "pool_scale",
           "w_out", "ffn2_norm", "ffn2_w_gate", "ffn2_w_up", "ffn2_w_down", "final_norm")


def _tile_for(n, cap=1024):
    best = 128
    for t in range(128, min(n, cap) + 1, 128):
        if n % t == 0:
            best = t
    return n if n <= cap else best


GATHER_BEHIND = {
    (0, "ffn1_up"): [("w_in", 0), ("w_out", 0), ("ffn2_w_gate", 0), ("ffn2_w_up", 0)],
    (0, "ffn1_down"): [("ffn2_w_down", 0)],
    (0, "mix_in"): [("ffn1_w_gate", 1)],
    (0, "mix_out"): [("ffn1_w_up", 1)],
    (0, "ffn2_up"): [("ffn1_w_down", 1), ("w_in", 1), ("w_out", 1)],
    (0, "ffn2_down"): [("ffn2_w_gate", 1)],
    (1, "ffn1_up"): [("ffn2_w_up", 1), ("ffn2_w_down", 1)],
}


class _GatherPlan:
    def __init__(self, local):
        self.local, self.pending, self.ready = local, [], {}

    def comm(self, firsts):
        cm = _Comm()
        self._passing, self._firsts = list(self.pending), list(firsts)
        for _, buf in self._passing:
            cm.gather_pass(buf)
        for key in self._firsts:
            cm.gather_first(self.local[key])
        return cm

    def done(self, couts):
        npass = len(self._passing)
        for (key, _), buf in zip(self._passing, couts[:npass]):
            self.ready[key] = buf
        self.pending = list(zip(self._firsts, couts[npass:]))


def kernel(x, ffn1_norm, ffn1_w_gate, ffn1_w_up, ffn1_w_down, mix_norm, w_in, conv_w, pool_w, pool_scale, w_out, ffn2_norm, ffn2_w_gate, ffn2_w_up, ffn2_w_down, final_norm, loss_target, m_ffn1_norm, m_ffn1_w_gate, m_ffn1_w_up, m_ffn1_w_down, m_mix_norm, m_w_in, m_conv_w, m_pool_w, m_pool_scale, m_w_out, m_ffn2_norm, m_ffn2_w_gate, m_ffn2_w_up, m_ffn2_w_down, m_final_norm, v_ffn1_norm, v_ffn1_w_gate, v_ffn1_w_up, v_ffn1_w_down, v_mix_norm, v_w_in, v_conv_w, v_pool_w, v_pool_scale, v_w_out, v_ffn2_norm, v_ffn2_w_gate, v_ffn2_w_up, v_ffn2_w_down, v_final_norm):
    given = dict(locals())
    W = {n: given[n] for n in WEIGHTS}
    M = {n: given["m_" + n] for n in WEIGHTS}
    V = {n: given["v_" + n] for n in WEIGHTS}
    depth = ffn1_norm.shape[0]
    D = x.shape[-1]
    xs = x[0]
    my_chip = 2 * lax.axis_index("x") + lax.axis_index("y")
    my_core = lax.axis_index("c")

    half_index = my_core.astype(jnp.int32).reshape(1)
    chip_index = my_chip.astype(jnp.int32).reshape(1)
    plan = _GatherPlan({(n, l): W[n][l].astype(BF16) for n in BIG for l in range(depth)})
    head = [("ffn1_w_gate", 0), ("ffn1_w_up", 0), ("ffn1_w_down", 0)]
    gathered = gather_weights([plan.local[key] for key in head] + [conv_w])
    plan.ready.update(zip(head, gathered[:-1]))
    conv_full = jnp.moveaxis(gathered[-1], 0, 2).reshape(depth, 3, D_CONV)
    pool_b = pool_w.astype(BF16)
    G = plan.ready

    def behind(l, stage):
        return plan.comm([(n, ll) for n, ll in GATHER_BEHIND.get((l, stage), []) if ll < depth])

    saved = []
    cur = xs
    for l in range(depth):
        x0 = cur
        h = rms_fwd(x0, ffn1_norm[l], "ffn1_norm")
        (g, u, a), got = ffn_up(h, G["ffn1_w_gate", l], G["ffn1_w_up", l], "ffn1_up", behind(l, "ffn1_up"))
        plan.done(got)
        cur, got = resid_mm(a, G["ffn1_w_down", l], x0, 0.5, "ffn1_down", behind(l, "ffn1_down"))
        plan.done(got)
        s1 = (x0, h, g, u, a)

        x1 = cur
        hm = rms_fwd(x1, mix_norm[l], "mix_norm")
        z, got = col_mm(hm, G["w_in", l], "mix_in", behind(l, "mix_in"))
        plan.done(got)
        pats = [attn_fwd(z, d, f"attn_fwd_d{d}") for d in DILATIONS]
        y, yb, lse_tot = attn_merge([p[0] for p in pats], [p[1] for p in pats], "attn_merge")
        y_conv, y_pool = conv_pool_fwd(z, conv_full[l], pool_b[l], pool_scale[l].reshape(1, D_POOL), "conv_pool_fwd")
        ycat = jnp.concatenate([yb, y_conv, y_pool], axis=1)
        cur, got = resid_mm(ycat, G["w_out", l], x1, 1.0, "mix_out", behind(l, "mix_out"))
        plan.done(got)
        sm = (x1, hm, z, y, lse_tot, ycat)

        x2 = cur
        h = rms_fwd(x2, ffn2_norm[l], "ffn2_norm")
        (g, u, a), got = ffn_up(h, G["ffn2_w_gate", l], G["ffn2_w_up", l], "ffn2_up", behind(l, "ffn2_up"))
        plan.done(got)
        cur, got = resid_mm(a, G["ffn2_w_down", l], x2, 0.5, "ffn2_down", behind(l, "ffn2_down"))
        plan.done(got)
        saved.append((s1, sm, (x2, h, g, u, a)))
    assert not plan.pending and len(G) == len(BIG) * depth

    loss11, dx, dxb, d_final = final_loss(cur, final_norm, loss_target[0], "final_loss")

    small_grads = {n: [None] * depth for n in SMALL if n != "final_norm"}
    partials = {n: [None] * depth for n in BIG}
    slots = {n: [None] * depth for n in BIG}

    def halves(grad):
        cm = _Comm()
        cm.sibling_half(grad)
        return cm

    def ffn_backward(sv, dx, dxb, norm, tag, l):
        x_in, h, g, u, a = sv
        names = [f"{tag}_w_down", f"{tag}_w_gate", f"{tag}_w_up"]
        wd, wg, wu = (G[n, l] for n in names)
        FS = wg.shape[-1]
        dg, du = ffn_bwd_act(dxb, wd, g, u, f"{tag}_bwd_act")
        g_d, _ = wgrad(a, dxb, True, False, 0.5, FS, _tile_for(D), f"{tag}_dwd")
        g_g, (got_d,) = wgrad(h, dg, False, True, 1.0, _tile_for(D), FS, f"{tag}_dwg", halves(g_d))
        g_u, (got_g,) = wgrad(h, du, False, True, 1.0, _tile_for(D), FS, f"{tag}_dwu", halves(g_g))
        p_d = pair_sum(g_d, got_d, half_index, "pair_sum")
        p_g = pair_sum(g_g, got_g, half_index, "pair_sum")
        cm = halves(g_u)
        cm.exchange(p_d)
        cm.exchange(p_g)
        last = {}

        def after(got):
            last["p"] = pair_sum(g_u, got[0], half_index, "pair_sum")
            cm2 = _Comm()
            cm2.exchange(last["p"])
            return cm2

        dx, dxb, dnorm, got, got2 = bwd_dh([dg, du], [wg, wu], x_in, norm[l], dx, f"{tag}_bwd_dh", cm, after)
        for n, p, s in zip(names, [p_d, p_g, last["p"]], [got[1], got[2], got2[0]]):
            partials[n][l], slots[n][l] = p, s
        return dx, dxb, dnorm

    for l in reversed(range(depth)):
        s1, sm, s2 = saved[l]
        dx, dxb, small_grads["ffn2_norm"][l] = ffn_backward(s2, dx, dxb, ffn2_norm, "ffn2", l)

        x1, hm, z, y, lse_tot, ycat = sm
        dycat = nt_col_mm(dxb, G["w_out", l], "mix_out_bwd")
        g_wout, _ = wgrad(ycat, dxb, True, False, 1.0, D_MIX // NSH, _tile_for(D, 2048), "mix_dwout")
        parts = [attn_bwd(z, dycat, y, lse_tot, d, f"attn_bwd_d{d}") for d in DILATIONS]
        dcp, dcw, dpw, dps = conv_pool_bwd(z, dycat, conv_full[l], pool_b[l], pool_scale[l].reshape(1, D_POOL), "conv_pool_bwd")
        dz = assemble_dz([p[0] for p in parts], [p[1] for p in parts], [p[2] for p in parts], dcp, "assemble_dz")
        g_win, (got_out,) = wgrad(hm, dz, False, True, 1.0, _tile_for(D), D_IN // NSH, "mix_dwin", halves(g_wout))
        p_out = pair_sum(g_wout, got_out, half_index, "pair_sum")
        cm = halves(g_win)
        cm.exchange(p_out)
        last = {}

        def after(got, g_win=g_win, last=last):
            last["p"] = pair_sum(g_win, got[0], half_index, "pair_sum")
            cm2 = _Comm()
            cm2.exchange(last["p"])
            return cm2

        dx, dxb, dnm, got, got2 = bwd_dh([dz], [G["w_in", l]], x1, mix_norm[l], dx, "mix_bwd_dh", cm, after)
        for n, p, s in zip(["w_out", "w_in"], [p_out, last["p"]], [got[1], got2[0]]):
            partials[n][l], slots[n][l] = p, s
        small_grads["mix_norm"][l], small_grads["conv_w"][l] = dnm, dcw
        small_grads["pool_w"][l], small_grads["pool_scale"][l] = dpw, dps

        dx, dxb, small_grads["ffn1_norm"][l] = ffn_backward(s1, dx, dxb, ffn1_norm, "ffn1", l)

    mine = [chip_sum(partials[n], slots[n], chip_index, "chip_sum") for n in BIG]
    theirs = sibling_swap(mine)
    grads, delta, new_m, new_v = {}, {}, {}, {}
    for n, a, b in zip(BIG, mine, theirs):
        grads[n], delta[n], new_m[n], new_v[n] = adamw_halves(W[n], a, b, M[n], V[n], half_index, "adamw")

    small_full = {n: jnp.stack([a.reshape(W[n].shape[1:] if n != "conv_w" else (3, D_CONV)) for a in small_grads[n]])
                  for n in small_grads}
    small_full["final_norm"] = d_final.reshape(D)
    order = list(SMALL)
    packed = jnp.concatenate([small_full[n].reshape(-1) for n in order])
    pad = (-packed.shape[0]) % (8 * 128)
    packed = jnp.pad(packed, (0, pad)).reshape(-1, 128)
    summed = small_all_reduce(packed).reshape(-1)
    off = 0
    for n in order:
        size = small_full[n].size
        grads[n] = summed[off:off + size].reshape(small_full[n].shape)
        off += size
    grads["conv_w"] = lax.dynamic_slice_in_dim(grads["conv_w"], my_chip * (D_CONV // NSH), D_CONV // NSH, axis=2)

    def pack(src):
        flat_ = jnp.concatenate([src[n].reshape(-1) for n in order])
        return jnp.pad(flat_, (0, (-flat_.shape[0]) % (8 * 128))).reshape(-1, 128)

    ds, ms, vs = adamw(pack(W), pack(grads), pack(M), pack(V), "adamw_small")
    off = 0
    for n in order:
        size = W[n].size
        for dst, src in ((delta, ds), (new_m, ms), (new_v, vs)):
            dst[n] = src.reshape(-1)[off:off + size].reshape(W[n].shape)
        off += size

    loss = lax.psum(loss11[0, 0], ("x", "y", "c"))
    return (loss, dx.reshape(x.shape), *[grads[n] for n in WEIGHTS], *[delta[n] for n in WEIGHTS],
            *[new_m[n] for n in WEIGHTS], *[new_v[n] for n in WEIGHTS])
```

```python
import functools

import jax
import jax.numpy as jnp
from jax import lax
from jax.experimental import pallas as pl
from jax.experimental.pallas import tpu as pltpu

F32 = jnp.float32
BF16 = jnp.bfloat16
MESH = pl.DeviceIdType.MESH

RMS_EPS = 1e-6
NEG_INF = -1e30
HEAD_DIM = 64
BLK = 128
SPAN = 128
DILATIONS = (1, 4, 16)
D_ATTN = 1024
D_CONV = 512
D_POOL = 512
POOL_WINDOWS = (2, 4, 8, 16)
POOL_GROUP = 128
D_IN = 3 * D_ATTN + 3 * D_CONV + D_POOL
D_MIX = D_ATTN + D_CONV + D_POOL
N_PAIR = D_ATTN // 128
ATTN_SCALE = HEAD_DIM ** -0.5
NSH = 4
ADAM_LR, ADAM_B1, ADAM_B2, ADAM_EPS, ADAM_WD, ADAM_STEP = 0.001, 0.9, 0.999, 1e-08, 0.01, 10

VMEM_LIMIT = 56 * 2 ** 20


def _cp(n_axes):
    return pltpu.CompilerParams(dimension_semantics=("arbitrary",) * n_axes, vmem_limit_bytes=VMEM_LIMIT)


def _sds(shape, dtype):
    return jax.ShapeDtypeStruct(shape, dtype)


def _dot(a, b):
    return jnp.dot(a, b, preferred_element_type=F32)


def _dot_nt(a, b):
    return lax.dot_general(a, b, (((1,), (1,)), ((), ())), preferred_element_type=F32)


def _dot_tn(a, b):
    return lax.dot_general(a, b, (((0,), (0,)), ((), ())), preferred_element_type=F32)


def _row_tile(s):
    return min(512, s)


def _rms_bwd_tile(xv, gv, dh):
    r = lax.rsqrt(jnp.mean(xv * xv, axis=-1, keepdims=True) + RMS_EPS)
    xhat = xv * r
    dg = jnp.sum(dh * xhat, axis=0, keepdims=True)
    dxhat = dh * gv
    dx = r * (dxhat - xhat * jnp.mean(dxhat * xhat, axis=-1, keepdims=True))
    return dx, dg


def final_loss(x, g, target, name):
    S, D = x.shape
    tm = _row_tile(S)

    def body(x_ref, g_ref, t_ref, loss_ref, dx_ref, dxb_ref, dg_ref):
        i = pl.program_id(0)
        xv, gv = x_ref[...], g_ref[...]
        r = lax.rsqrt(jnp.mean(xv * xv, axis=-1, keepdims=True) + RMS_EPS)
        err = xv * r * gv - t_ref[...]
        part = 0.5 * jnp.sum(jnp.mean(err * err, axis=-1, keepdims=True), axis=0, keepdims=True)
        dx, dg = _rms_bwd_tile(xv, gv, err * (1.0 / D))

        @pl.when(i == 0)
        def _():
            loss_ref[...] = jnp.zeros_like(loss_ref)
            dg_ref[...] = jnp.zeros_like(dg_ref)

        loss_ref[...] += part
        dg_ref[...] += dg
        dx_ref[...] = dx
        dxb_ref[...] = dx.astype(BF16)

    row = pl.BlockSpec((tm, D), lambda i: (i, 0))
    vec = pl.BlockSpec((1, D), lambda i: (0, 0))
    return pl.pallas_call(
        body, grid=(S // tm,), in_specs=[row, vec, row],
        out_specs=[pl.BlockSpec((1, 1), lambda i: (0, 0)), row, row, vec],
        out_shape=[_sds((1, 1), F32), _sds((S, D), F32), _sds((S, D), BF16), _sds((1, D), F32)],
        name=name, compiler_params=_cp(1),
    )(x, g.reshape(1, D), target)


def _wspec(w, imap):
    _, a, b = w.shape
    return pl.BlockSpec((None, a, b), lambda *ids: (imap(*ids), 0, 0))


class _Comm:
    def __init__(self):
        self.ins, self.out_shapes, self.aliases, self.items = [], [], {}, []

    def _add(self, kind, operand, out_shape, alias):
        if alias:
            self.aliases[len(self.ins)] = len(self.out_shapes)
        self.items.append((kind, len(self.ins), len(self.out_shapes)))
        self.ins.append(operand)
        self.out_shapes.append(out_shape)

    def gather_first(self, src):
        self._add("first", src, _sds((NSH,) + src.shape, src.dtype), False)

    def gather_pass(self, buf):
        self._add("pass", buf, _sds(buf.shape, buf.dtype), True)

    def sibling_half(self, grad):
        self._add("half", grad, _sds((NSH, grad.shape[1] // 2, grad.shape[2]), grad.dtype), False)

    def exchange(self, partial):
        self._add("xchg", partial, _sds(partial.shape, partial.dtype), False)

    def run(self, cins, couts, send, recv, start):
        x, y, c = lax.axis_index("x"), lax.axis_index("y"), lax.axis_index("c")
        chips = [(1 - x, y), (x, 1 - y), (1 - x, 1 - y)]
        me, sib = 2 * x + y, (x, y, 1 - c)
        for it, (kind, i, o) in enumerate(self.items):
            def rc(s, src, dst, to, it=it):
                return pltpu.make_async_remote_copy(src_ref=src, dst_ref=dst, send_sem=send.at[it, s], recv_sem=recv.at[it, s],
                                                    device_id=to, device_id_type=MESH)
            src, buf = cins[i], couts[o]
            if kind == "first":
                ha = src.shape[0] // 2
                rows = pl.ds(c * ha, ha)
                cps = [rc(s, src.at[rows], buf.at[me, rows], (px, py, c)) for s, (px, py) in enumerate(chips)]
                cps.append(rc(3, src, buf.at[me], sib))
                landing = [buf.at[2 * px + py, rows] for px, py in chips] + [buf.at[me]]
            elif kind == "pass":
                ha = buf.shape[1] // 2
                rows, other = pl.ds(c * ha, ha), pl.ds((1 - c) * ha, ha)
                cps = [rc(s, buf.at[2 * px + py, rows], buf.at[2 * px + py, rows], sib) for s, (px, py) in enumerate(chips)]
                landing = [buf.at[2 * px + py, other] for px, py in chips]
            elif kind == "half":
                ha = src.shape[1] // 2
                cps = [rc(0, src.at[:, pl.ds((1 - c) * ha, ha), :], buf, sib)]
                landing = [buf]
            else:
                cps = [rc(s, src.at[2 * px + py], buf.at[me], (px, py, c)) for s, (px, py) in enumerate(chips)]
                landing = [buf.at[2 * px + py] for px, py in chips]
            if start:
                for cp in cps:
                    cp.start()
            else:
                for s, dst in enumerate(landing):
                    rc(s, dst, dst, sib).wait_recv()
                for cp in cps:
                    cp.wait_send()


def _call(body, *, grid, in_specs, out_specs, out_shape, name, args, scratch=(), comm=None):
    multi = isinstance(out_shape, (list, tuple))
    oshape = list(out_shape) if multi else [out_shape]
    ospecs = list(out_specs) if multi else [out_specs]
    if comm is None or not comm.items:
        res = pl.pallas_call(body, grid=grid, in_specs=list(in_specs), out_specs=ospecs, out_shape=oshape,
                             scratch_shapes=list(scratch), name=name, compiler_params=_cp(len(grid)))(*args)
        return (list(res) if multi else res[0]), []
    nin, nout, nci, nco, nscr = len(in_specs), len(oshape), len(comm.ins), len(comm.out_shapes), len(scratch)

    def full(*refs):
        ins, cins = refs[:nin], refs[nin:nin + nci]
        outs, couts = refs[nin + nci:nin + nci + nout], refs[nin + nci + nout:nin + nci + nout + nco]
        scr = refs[nin + nci + nout + nco:nin + nci + nout + nco + nscr]
        send, recv = refs[-2:]
        ids = [pl.program_id(a) for a in range(len(grid))]
        first = functools.reduce(jnp.logical_and, [i == 0 for i in ids])
        last = functools.reduce(jnp.logical_and, [i == g - 1 for i, g in zip(ids, grid)])

        @pl.when(first)
        def _():
            comm.run(cins, couts, send, recv, True)

        body(*ins, *outs, *scr)

        @pl.when(last)
        def _():
            comm.run(cins, couts, send, recv, False)

    sems = pltpu.SemaphoreType.DMA((len(comm.items), 4))
    res = pl.pallas_call(
        full, grid=grid, in_specs=list(in_specs) + [ANY] * nci, out_specs=ospecs + [ANY] * nco,
        out_shape=oshape + comm.out_shapes, scratch_shapes=list(scratch) + [sems, sems],
        input_output_aliases={nin + i: nout + o for i, o in comm.aliases.items()},
        name=name, compiler_params=_cp(len(grid)),
    )(*args, *comm.ins)
    main = list(res[:nout])
    return (main if multi else main[0]), list(res[nout:])


def _rms_tile(x_ref, gain_ref, h_ref):
    xv = x_ref[...]
    r = lax.rsqrt(jnp.mean(xv * xv, axis=-1, keepdims=True) + RMS_EPS)
    hv = (xv * r * gain_ref[...]).astype(BF16)

    @pl.when(pl.program_id(0) == 0)
    def _():
        h_ref[...] = hv

    return hv


def _h_spec(tm, D, n_tiles):
    return pl.BlockSpec((tm, D), lambda k, i: (jnp.where(k == 0, i, n_tiles - 1), 0))


def ffn_up(x, gain, wg, wu, name, comm=None):
    S, D = x.shape
    FS = wg.shape[-1]
    tm = _row_tile(S)

    def body(x_ref, gain_ref, wg_ref, wu_ref, h_ref, g_ref, u_ref, a_ref):
        hv = _rms_tile(x_ref, gain_ref, h_ref)
        g = _dot(hv, wg_ref[...])
        u = _dot(hv, wu_ref[...])
        g_ref[...] = g.astype(BF16)
        u_ref[...] = u.astype(BF16)
        a_ref[...] = (g * jax.nn.sigmoid(g) * u).astype(BF16)

    row = pl.BlockSpec((tm, D), lambda k, i: (i, 0))
    out = pl.BlockSpec((tm, FS), lambda k, i: (i, k))
    shard = lambda k, i: k
    return _call(
        body, grid=(NSH, S // tm),
        in_specs=[row, pl.BlockSpec((1, D), lambda k, i: (0, 0)), _wspec(wg, shard), _wspec(wu, shard)],
        out_specs=[_h_spec(tm, D, S // tm)] + [out] * 3, out_shape=[_sds((S, D), BF16)] + [_sds((S, NSH * FS), BF16)] * 3,
        name=name, args=(x, gain.reshape(1, D), wg, wu), comm=comm)


def col_mm(x, gain, w, name, comm=None):
    S, D = x.shape
    NS = w.shape[-1]
    tm = _row_tile(S)

    def body(x_ref, gain_ref, w_ref, h_ref, z_ref):
        z_ref[...] = _dot(_rms_tile(x_ref, gain_ref, h_ref), w_ref[...])

    row = pl.BlockSpec((tm, D), lambda k, i: (i, 0))
    return _call(
        body, grid=(NSH, S // tm),
        in_specs=[row, pl.BlockSpec((1, D), lambda k, i: (0, 0)), _wspec(w, lambda k, i: k)],
        out_specs=[_h_spec(tm, D, S // tm), pl.BlockSpec((tm, NS), lambda k, i: (i, k))],
        out_shape=[_sds((S, D), BF16), _sds((S, NSH * NS), F32)],
        name=name, args=(x, gain.reshape(1, D), w), comm=comm)


def resid_mm(a, w, x, scale, name, comm=None):
    S, K = a.shape
    D = w.shape[-1]
    tm = _row_tile(S)
    tn = D // 2 if D % 256 == 0 else D

    def body(a_ref, w_ref, x_ref, o_ref):
        o_ref[...] = x_ref[...] + scale * _dot(a_ref[...], w_ref[...])

    out = pl.BlockSpec((tm, tn), lambda j, i: (i, j))
    return _call(
        body, grid=(D // tn, S // tm),
        in_specs=[pl.BlockSpec((tm, K), lambda j, i: (i, 0)), pl.BlockSpec((K, tn), lambda j, i: (0, j)), out],
        out_specs=out, out_shape=_sds((S, D), F32), name=name, args=(a, w.reshape(K, D), x), comm=comm)


def ffn_bwd_act(dxb, wd, g, u, name, comm=None):
    S, D = dxb.shape
    FS = wd.shape[-2]
    tm = _row_tile(S)

    def body(dx_ref, w_ref, g_ref, u_ref, dg_ref, du_ref):
        da = 0.5 * _dot_nt(dx_ref[...], w_ref[...])
        gv = g_ref[...].astype(F32)
        uv = u_ref[...].astype(F32)
        s = jax.nn.sigmoid(gv)
        du_ref[...] = (da * gv * s).astype(BF16)
        dg_ref[...] = (da * uv * s * (1.0 + gv * (1.0 - s))).astype(BF16)

    act = pl.BlockSpec((tm, FS), lambda k, i: (i, k))
    return _call(
        body, grid=(NSH, S // tm),
        in_specs=[pl.BlockSpec((tm, D), lambda k, i: (i, 0)), _wspec(wd, lambda k, i: k), act, act],
        out_specs=[act, act], out_shape=[_sds((S, NSH * FS), BF16)] * 2,
        name=name, args=(dxb, wd, g, u), comm=comm)


def nt_col_mm(dxb, w, name, comm=None):
    S, D = dxb.shape
    KS = w.shape[-2]
    tm = _row_tile(S)

    def body(dx_ref, w_ref, o_ref):
        o_ref[...] = _dot_nt(dx_ref[...], w_ref[...])

    return _call(
        body, grid=(NSH, S // tm),
        in_specs=[pl.BlockSpec((tm, D), lambda k, i: (i, 0)), _wspec(w, lambda k, i: k)],
        out_specs=pl.BlockSpec((tm, KS), lambda k, i: (i, k)), out_shape=_sds((S, NSH * KS), F32),
        name=name, args=(dxb, w), comm=comm)


def wgrad(lhs, rhs, lhs_sharded, rhs_sharded, scale, tr, tc, name, comm=None):
    S = lhs.shape[0]
    R = lhs.shape[1] // (NSH if lhs_sharded else 1)
    C = rhs.shape[1] // (NSH if rhs_sharded else 1)
    ts = min(2048, S)
    nr, nc = R // tr, C // tc

    def body(l_ref, r_ref, o_ref):
        @pl.when(pl.program_id(3) == 0)
        def _():
            o_ref[...] = jnp.zeros_like(o_ref)

        o_ref[...] += scale * _dot_tn(l_ref[...], r_ref[...])

    lmap = (lambda k, a, b, s: (s, k * nr + a)) if lhs_sharded else (lambda k, a, b, s: (s, a))
    rmap = (lambda k, a, b, s: (s, k * nc + b)) if rhs_sharded else (lambda k, a, b, s: (s, b))
    return _call(
        body, grid=(NSH, nr, nc, S // ts),
        in_specs=[pl.BlockSpec((ts, tr), lmap), pl.BlockSpec((ts, tc), rmap)],
        out_specs=pl.BlockSpec((None, tr, tc), lambda k, a, b, s: (k, a, b)),
        out_shape=_sds((NSH, R, C), F32), name=name, args=(lhs, rhs), comm=comm)


def bwd_dh(dys, ws, x, g, dxin, name, comm=None, after=None):
    S, D = x.shape
    NS = ws[0].shape[-1]
    tm = _row_tile(S)
    nj = len(dys)

    def mm_body(*refs):
        dy_refs, w_refs, dh_ref = refs[:nj], refs[nj:2 * nj], refs[2 * nj]

        @pl.when(pl.program_id(1) == 0)
        def _():
            dh_ref[...] = jnp.zeros_like(dh_ref)

        for dy_ref, w_ref in zip(dy_refs, w_refs):
            dh_ref[...] += _dot_nt(dy_ref[...], w_ref[...])

    dh, comm_out = _call(
        mm_body, grid=(S // tm, NSH),
        in_specs=[pl.BlockSpec((tm, NS), lambda i, k: (i, k))] * nj + [_wspec(w, lambda i, k: k) for w in ws],
        out_specs=pl.BlockSpec((tm, D), lambda i, k: (i, 0)), out_shape=_sds((S, D), F32),
        name=name + "_mm", args=(*dys, *ws), comm=comm)

    def norm_body(dh_ref, x_ref, g_ref, dxin_ref, dx_ref, dxb_ref, dg_ref):
        dx, dg = _rms_bwd_tile(x_ref[...], g_ref[...], dh_ref[...])
        tot = dxin_ref[...] + dx
        dx_ref[...] = tot
        dxb_ref[...] = tot.astype(BF16)

        @pl.when(pl.program_id(0) == 0)
        def _():
            dg_ref[...] = jnp.zeros_like(dg_ref)

        dg_ref[...] += dg

    row = pl.BlockSpec((tm, D), lambda i: (i, 0))
    vec = pl.BlockSpec((1, D), lambda i: (0, 0))
    comm2 = after(comm_out) if after is not None else None
    (dx, dxb, dgain), comm2_out = _call(
        norm_body, grid=(S // tm,), in_specs=[row, row, vec, row], out_specs=[row, row, vec],
        out_shape=[_sds((S, D), F32), _sds((S, D), BF16), _sds((1, D), F32)],
        name=name + "_norm", args=(dh, x, g.reshape(1, D), dxin), comm=comm2)
    return dx, dxb, dgain, comm_out, comm2_out


PAIRS_PER_STEP = {1: 8, 4: 1, 16: 1}
ZQ, ZK, ZV = 0, D_ATTN // 128, 2 * D_ATTN // 128


def _band_valid(n):
    qi = lax.broadcasted_iota(jnp.int32, (BLK, 2 * BLK), 0)
    kj = lax.broadcasted_iota(jnp.int32, (BLK, 2 * BLK), 1)
    dist = qi + BLK - kj
    return (dist >= 0) & (dist <= SPAN) & ((kj >= BLK) | (n > 0))


PERM_FROM = 8


def _residue_perm(d, transpose):
    q = 128 // d
    a = lax.broadcasted_iota(jnp.int32, (128, 128), 1 if transpose else 0)
    b = lax.broadcasted_iota(jnp.int32, (128, 128), 0 if transpose else 1)
    return (b == (a % q) * d + a // q).astype(BF16)


def _perm_apply(perm, x, n_terms):
    out, rest = None, x
    for t in range(n_terms):
        term = rest.astype(BF16)
        out = _dot(perm, term) if out is None else out + _dot(perm, term)
        if t + 1 < n_terms:
            rest = rest - term.astype(F32)
    return out


def _split_residues(src_ref, dst, d, pp, row0=0, n_terms=1):
    if d < PERM_FROM:
        for r in range(d):
            for p in range(pp):
                dst[r * pp + p, row0:row0 + BLK, :] = src_ref[pl.ds(r, BLK, stride=d), p * 128:(p + 1) * 128]
        return
    perm, q = _residue_perm(d, False), 128 // d
    for c in range(d):
        t = _perm_apply(perm, src_ref[c * 128:(c + 1) * 128, :], n_terms)
        for r in range(d):
            dst[r, row0 + c * q:row0 + (c + 1) * q, :] = t[r * q:(r + 1) * q]


def _merge_residues(bufs, d, pp, c):
    q = 128 // d
    t = jnp.concatenate([bufs[r, c * q:(c + 1) * q, :] for r in range(d)], axis=0)
    return _perm_apply(_residue_perm(d, True), t, 3)


def _lane_pick(stat, lane, idx):
    return jnp.sum(jnp.where(lane == idx, stat, 0.0), axis=-1, keepdims=True)


def _residue_view(dilation):
    return 1, dilation


def _view_shape(shape, dv):
    return (shape[0] // dv, dv, shape[1]) if dv > 1 else tuple(shape)


def _viewed(a, dv):
    return a.reshape(_view_shape(a.shape, dv))


def _view_spec(dv, rows, width, row_of, col_of):
    if dv > 1:
        return pl.BlockSpec((rows, None, width), lambda r, a, b: (row_of(a, b), r, col_of(a, b)))
    return pl.BlockSpec((rows, width), lambda r, a, b: (row_of(a, b), col_of(a, b)))


def attn_fwd(z, dilation, name):
    S = z.shape[0]
    dv, d = _residue_view(dilation)
    R = BLK * d
    nb = S // (BLK * dilation)
    pp = PAIRS_PER_STEP[d]
    G = d * pp
    W = 128 * pp

    def body(q_ref, kc_ref, kp_ref, vc_ref, vp_ref, o_ref, lse_ref, qs, ks, vs, os_, ls):
        n, hb = pl.program_id(1), pl.program_id(2)
        valid = _band_valid(n)
        lane = lax.broadcasted_iota(jnp.int32, (1, 128), 1)
        _split_residues(q_ref, qs, d, pp)
        _split_residues(kp_ref, ks, d, pp)
        _split_residues(kc_ref, ks, d, pp, BLK)
        _split_residues(vp_ref, vs, d, pp)
        _split_residues(vc_ref, vs, d, pp, BLK)

        def step(j, carry):
            q2, k2, v2 = qs[j].astype(BF16), ks[j].astype(BF16), vs[j].astype(BF16)
            pair = hb * pp + j % pp
            o2 = jnp.zeros((BLK, 128), F32)
            stat = jnp.zeros((BLK, 128), F32)
            for h in range(2):
                hm = (lane < HEAD_DIM) if h == 0 else (lane >= HEAD_DIM)
                qm = jnp.where(hm, q2, jnp.zeros_like(q2))
                vm = jnp.where(hm, v2, jnp.zeros_like(v2))
                s = jnp.where(valid, _dot_nt(qm, k2) * ATTN_SCALE, NEG_INF)
                m = jnp.max(s, axis=-1, keepdims=True)
                p = jnp.exp(s - m)
                lsum = jnp.sum(p, axis=-1, keepdims=True)
                o2 = o2 + _dot(p.astype(BF16), vm) / lsum
                stat = jnp.where(lane == 2 * pair + h, m + jnp.log(lsum), stat)
            os_[j] = o2
            ls[j] = stat
            return carry

        lax.fori_loop(0, G, step, 0, unroll=min(G, 8))

        @pl.when(hb == 0)
        def _():
            lse_ref[...] = jnp.zeros_like(lse_ref)

        if d < PERM_FROM:
            for r in range(d):
                rows = pl.ds(r, BLK, stride=d)
                acc = lse_ref[rows, :]
                for p in range(pp):
                    o_ref[rows, p * 128:(p + 1) * 128] = os_[r * pp + p]
                    acc = acc + ls[r * pp + p]
                lse_ref[rows, :] = acc
        else:
            for c in range(d):
                rows = slice(c * 128, (c + 1) * 128)
                o_ref[rows, :] = _merge_residues(os_, d, pp, c)
                lse_ref[rows, :] += _merge_residues(ls, d, pp, c)

    cur = lambda c: _view_spec(dv, R, W, lambda n, hb: n, lambda n, hb: c // pp + hb)
    prev = lambda c: _view_spec(dv, R, W, lambda n, hb: jnp.maximum(n - 1, 0), lambda n, hb: c // pp + hb)
    zv = _viewed(z, dv)
    o, lse = pl.pallas_call(
        body, grid=(dv, nb, N_PAIR // pp),
        in_specs=[cur(ZQ), cur(ZK), prev(ZK), cur(ZV), prev(ZV)],
        out_specs=[_view_spec(dv, R, W, lambda n, hb: n, lambda n, hb: hb), _view_spec(dv, R, 128, lambda n, hb: n, lambda n, hb: 0)],
        out_shape=[_sds(_view_shape((S, D_ATTN), dv), F32), _sds(_view_shape((S, 128), dv), F32)],
        scratch_shapes=[pltpu.VMEM((G, BLK, 128), F32), pltpu.VMEM((G, 2 * BLK, 128), F32), pltpu.VMEM((G, 2 * BLK, 128), F32),
                        pltpu.VMEM((G, BLK, 128), F32), pltpu.VMEM((G, BLK, 128), F32)],
        name=name, compiler_params=_cp(3),
    )(zv, zv, zv, zv, zv)
    return o.reshape(S, D_ATTN), lse.reshape(S, 128)


def _pair_weights(w, lane):
    return [jnp.where(lane < HEAD_DIM, w[:, 2 * hp:2 * hp + 1], w[:, 2 * hp + 1:2 * hp + 2]) for hp in range(N_PAIR)]


def attn_merge(os_, lses, name):
    S = os_[0].shape[0]
    tm = _row_tile(S)
    npat = len(os_)

    def body(*refs):
        o_refs, l_refs = refs[:npat], refs[npat:2 * npat]
        y_ref, yb_ref, lt_ref = refs[2 * npat:]
        lane = lax.broadcasted_iota(jnp.int32, (1, 128), 1)
        ls = [r[...] for r in l_refs]
        mx = functools.reduce(jnp.maximum, ls)
        es = [jnp.exp(v - mx) for v in ls]
        den = functools.reduce(jnp.add, es)
        lt_ref[...] = mx + jnp.log(den)
        ws = [_pair_weights(e / den, lane) for e in es]
        for hp in range(N_PAIR):
            sl = slice(hp * 128, (hp + 1) * 128)
            y = ws[0][hp] * o_refs[0][:, sl]
            for p in range(1, npat):
                y = y + ws[p][hp] * o_refs[p][:, sl]
            y_ref[:, sl] = y
            yb_ref[:, sl] = y.astype(BF16)

    big = pl.BlockSpec((tm, D_ATTN), lambda i: (i, 0))
    st = pl.BlockSpec((tm, 128), lambda i: (i, 0))
    return pl.pallas_call(
        body, grid=(S // tm,), in_specs=[big] * npat + [st] * npat, out_specs=[big, big, st],
        out_shape=[_sds((S, D_ATTN), F32), _sds((S, D_ATTN), BF16), _sds((S, 128), F32)],
        name=name, compiler_params=_cp(1),
    )(*os_, *lses)


def attn_bwd(z, dycat, y, lse_tot, dilation, name):
    S = z.shape[0]
    dv, d = _residue_view(dilation)
    R = BLK * d
    nb = S // (BLK * dilation)
    pp = PAIRS_PER_STEP[d]
    G = d * pp
    W = 128 * pp

    def body(q_ref, kc_ref, kp_ref, vc_ref, vp_ref, dy_ref, y_ref, l_ref, dq_ref, dk_ref, dv_ref,
             qs, ks, vs, dys, ys, ls, dqs, dks, dvs, ck, cv):
        hb, n = pl.program_id(1), pl.program_id(2)

        def store_rows(ref, buf):
            if d < PERM_FROM:
                for r in range(d):
                    for p in range(pp):
                        ref[pl.ds(r, BLK, stride=d), p * 128:(p + 1) * 128] = buf[r * pp + p]
            else:
                for c in range(d):
                    ref[c * 128:(c + 1) * 128, :] = _merge_residues(buf, d, pp, c)

        @pl.when(n < nb)
        def _():
            valid = _band_valid(n)
            first = n == 0
            lane = lax.broadcasted_iota(jnp.int32, (1, 128), 1)
            _split_residues(q_ref, qs, d, pp)
            _split_residues(kp_ref, ks, d, pp)
            _split_residues(kc_ref, ks, d, pp, BLK)
            _split_residues(vp_ref, vs, d, pp)
            _split_residues(vc_ref, vs, d, pp, BLK)
            _split_residues(dy_ref, dys, d, pp, n_terms=3)
            _split_residues(y_ref, ys, d, pp, n_terms=3)
            _split_residues(l_ref, ls, d, 1, n_terms=3)

            def step(j, carry):
                q2, k2, v2, dy2, y2 = qs[j].astype(BF16), ks[j].astype(BF16), vs[j].astype(BF16), dys[j], ys[j]
                stat = ls[j // pp]
                pair = hb * pp + j % pp
                dq2 = jnp.zeros((BLK, 128), F32)
                dk2 = jnp.zeros((2 * BLK, 128), F32)
                dv2 = jnp.zeros((2 * BLK, 128), F32)
                for h in range(2):
                    hm = (lane < HEAD_DIM) if h == 0 else (lane >= HEAD_DIM)
                    qm = jnp.where(hm, q2, jnp.zeros_like(q2))
                    km = jnp.where(hm, k2, jnp.zeros_like(k2))
                    dym = jnp.where(hm, dy2, 0.0)
                    dymb = dym.astype(BF16)
                    s = _dot_nt(qm, k2) * ATTN_SCALE
                    p = jnp.where(valid, jnp.exp(s - _lane_pick(stat, lane, 2 * pair + h)), 0.0)
                    dp = _dot_nt(dymb, v2)
                    delta = jnp.sum(dym * y2, axis=-1, keepdims=True)
                    ds = (p * (dp - delta) * ATTN_SCALE).astype(BF16)
                    dq2 = dq2 + _dot(ds, km)
                    dk2 = dk2 + _dot_tn(ds, qm)
                    dv2 = dv2 + _dot_tn(p.astype(BF16), dymb)
                dqs[j] = dq2
                dks[j] = jnp.where(first, 0.0, ck[j]) + dk2[:BLK]
                dvs[j] = jnp.where(first, 0.0, cv[j]) + dv2[:BLK]
                ck[j] = dk2[BLK:]
                cv[j] = dv2[BLK:]
                return carry

            lax.fori_loop(0, G, step, 0, unroll=min(G, 8))
            store_rows(dq_ref, dqs)
            store_rows(dk_ref, dks)
            store_rows(dv_ref, dvs)

        @pl.when(n == nb)
        def _():
            store_rows(dk_ref, ck)
            store_rows(dv_ref, cv)

    cur = lambda n: jnp.minimum(n, nb - 1)
    prev = lambda n: jnp.clip(n - 1, 0, nb - 1)
    zb = lambda c, p: _view_spec(dv, R, W, lambda hb, n: (prev(n) if p else cur(n)), lambda hb, n: c // pp + hb)
    big_cur = _view_spec(dv, R, W, lambda hb, n: cur(n), lambda hb, n: hb)
    big_lag = _view_spec(dv, R, W, lambda hb, n: jnp.maximum(n - 1, 0), lambda hb, n: hb)
    buf = lambda rows, dt: pltpu.VMEM((G, rows, 128), dt)
    zv = _viewed(z, dv)
    outs = pl.pallas_call(
        body, grid=(dv, N_PAIR // pp, nb + 1),
        in_specs=[zb(ZQ, False), zb(ZK, False), zb(ZK, True), zb(ZV, False), zb(ZV, True),
                  big_cur, big_cur, _view_spec(dv, R, 128, lambda hb, n: cur(n), lambda hb, n: 0)],
        out_specs=[big_cur, big_lag, big_lag],
        out_shape=[_sds(_view_shape((S, D_ATTN), dv), F32)] * 3,
        scratch_shapes=[buf(BLK, F32), buf(2 * BLK, F32), buf(2 * BLK, F32), buf(BLK, F32), buf(BLK, F32),
                        pltpu.VMEM((d, BLK, 128), F32), buf(BLK, F32), buf(BLK, F32), buf(BLK, F32), buf(BLK, F32), buf(BLK, F32)],
        name=name, compiler_params=_cp(3),
    )(zv, zv, zv, zv, zv, _viewed(dycat, dv), _viewed(y, dv), _viewed(lse_tot, dv))
    return tuple(o.reshape(S, D_ATTN) for o in outs)


ZC_GB, ZC_GC, ZC_CI, ZC_PI = 6, 7, 8, 9


def _pool_counts(i, tb):
    pos = lax.broadcasted_iota(jnp.int32, (tb, D_POOL), 0) + i * tb + 1
    grp = lax.broadcasted_iota(jnp.int32, (tb, D_POOL), 1) // POOL_GROUP
    win = jnp.where(grp == 0, POOL_WINDOWS[0], jnp.where(grp == 1, POOL_WINDOWS[1],
                    jnp.where(grp == 2, POOL_WINDOWS[2], POOL_WINDOWS[3])))
    return jnp.minimum(pos, win).astype(F32), grp


def _select_group(stages, grp):
    return jnp.where(grp == 0, stages[0], jnp.where(grp == 1, stages[1], jnp.where(grp == 2, stages[2], stages[3])))


def _causal_window_sums(x2):
    s1 = x2 + pltpu.roll(x2, 1, 0)
    s2 = s1 + pltpu.roll(s1, 2, 0)
    s3 = s2 + pltpu.roll(s2, 4, 0)
    s4 = s3 + pltpu.roll(s3, 8, 0)
    return [s1, s2, s3, s4]


def _anticausal_window_sums(x2):
    n = x2.shape[0]
    s1 = x2 + pltpu.roll(x2, n - 1, 0)
    s2 = s1 + pltpu.roll(s1, n - 2, 0)
    s3 = s2 + pltpu.roll(s2, n - 4, 0)
    s4 = s3 + pltpu.roll(s3, n - 8, 0)
    return [s1, s2, s3, s4]


def _pooled(p_prev, p_cur, i, tb):
    x2 = jnp.concatenate([jnp.where(i > 0, p_prev, 0.0), p_cur], axis=0)
    count, grp = _pool_counts(i, tb)
    win_sum = _select_group([s[tb:] for s in _causal_window_sums(x2)], grp)
    return win_sum / count - p_cur, count, grp


def _pool_mm(v, pw_ref, nt):
    outs = []
    for gi in range(len(POOL_WINDOWS)):
        sl = slice(gi * POOL_GROUP, (gi + 1) * POOL_GROUP)
        outs.append(_dot_nt(v[:, sl], pw_ref[gi]) if nt else _dot(v[:, sl], pw_ref[gi]))
    return jnp.concatenate(outs, axis=1)


def conv_pool_fwd(z, conv_w, pool_w, pool_scale, name):
    S = z.shape[0]
    tb = min(256, S)

    def body(gb_ref, gc_ref, gcp_ref, ci_ref, cip_ref, pi_ref, pip_ref, cw_ref, pw_ref, ps_ref, yc_ref, yp_ref):
        i = pl.program_id(0)
        u2 = jnp.concatenate([jnp.where(i > 0, gcp_ref[...] * cip_ref[...], 0.0), gc_ref[...] * ci_ref[...]], axis=0)
        conv = cw_ref[0:1, :] * pltpu.roll(u2, 2, 0) + cw_ref[1:2, :] * pltpu.roll(u2, 1, 0) + cw_ref[2:3, :] * u2
        yc_ref[...] = (gb_ref[...] * conv[tb:]).astype(BF16)
        pooled, _, _ = _pooled(pip_ref[...], pi_ref[...], i, tb)
        yp_ref[...] = (_pool_mm(pooled.astype(BF16), pw_ref, False) * ps_ref[...]).astype(BF16)

    cur = lambda c: pl.BlockSpec((tb, 512), lambda i: (i, c))
    prev = lambda c: pl.BlockSpec((tb, 512), lambda i: (jnp.maximum(i - 1, 0), c))
    full = lambda a: pl.BlockSpec(a.shape, lambda i: (0,) * a.ndim)
    out = pl.BlockSpec((tb, 512), lambda i: (i, 0))
    return pl.pallas_call(
        body, grid=(S // tb,),
        in_specs=[cur(ZC_GB), cur(ZC_GC), prev(ZC_GC), cur(ZC_CI), prev(ZC_CI), cur(ZC_PI), prev(ZC_PI),
                  full(conv_w), full(pool_w), full(pool_scale)],
        out_specs=[out, out], out_shape=[_sds((S, 512), BF16)] * 2, name=name, compiler_params=_cp(1),
    )(z, z, z, z, z, z, z, conv_w, pool_w, pool_scale)


def conv_pool_bwd(z, dycat, conv_w, pool_w, pool_scale, name):
    S = z.shape[0]
    tb = min(256, S)
    nblk = S // tb
    ng = len(POOL_WINDOWS)

    def body(gb_ref, gbn_ref, gc_ref, gcp_ref, ci_ref, cip_ref, pi_ref, pip_ref, dyc_ref, dycn_ref, dyp_ref, dypn_ref,
             cw_ref, pw_ref, ps_ref, dz_ref, dcw_ref, dpw_ref, dps_ref):
        i = pl.program_id(0)
        last = i == nblk - 1

        @pl.when(i == 0)
        def _():
            dcw_ref[...] = jnp.zeros_like(dcw_ref)
            dpw_ref[...] = jnp.zeros_like(dpw_ref)
            dps_ref[...] = jnp.zeros_like(dps_ref)

        gc, ci = gc_ref[...], ci_ref[...]
        u2 = jnp.concatenate([jnp.where(i > 0, gcp_ref[...] * cip_ref[...], 0.0), gc * ci], axis=0)
        um2, um1, u0 = pltpu.roll(u2, 2, 0)[tb:], pltpu.roll(u2, 1, 0)[tb:], u2[tb:]
        conv = cw_ref[0:1, :] * um2 + cw_ref[1:2, :] * um1 + cw_ref[2:3, :] * u0
        dyc = dyc_ref[...]
        dconv = dyc * gb_ref[...]
        dc2 = jnp.concatenate([dconv, jnp.where(last, 0.0, dycn_ref[...] * gbn_ref[...])], axis=0)
        du = (cw_ref[2:3, :] * dconv + cw_ref[1:2, :] * pltpu.roll(dc2, 2 * tb - 1, 0)[:tb]
              + cw_ref[0:1, :] * pltpu.roll(dc2, 2 * tb - 2, 0)[:tb])
        dz_ref[:, 0:512] = (dyc * conv).astype(BF16)
        dz_ref[:, 512:1024] = (du * ci).astype(BF16)
        dz_ref[:, 1024:1536] = (du * gc).astype(BF16)
        dcw_ref[0:1, :] += jnp.sum(dconv * um2, axis=0, keepdims=True)
        dcw_ref[1:2, :] += jnp.sum(dconv * um1, axis=0, keepdims=True)
        dcw_ref[2:3, :] += jnp.sum(dconv * u0, axis=0, keepdims=True)

        pooled, count, grp = _pooled(pip_ref[...], pi_ref[...], i, tb)
        pooled_b = pooled.astype(BF16)
        t = _pool_mm(pooled_b, pw_ref, False)
        dyp, ps = dyp_ref[...], ps_ref[...]
        dps_ref[...] += jnp.sum(dyp * t, axis=0, keepdims=True)
        dt_b = (dyp * ps).astype(BF16)
        for gi in range(ng):
            sl = slice(gi * POOL_GROUP, (gi + 1) * POOL_GROUP)
            dpw_ref[gi] += _dot_tn(pooled_b[:, sl], dt_b[:, sl])
        dpooled = _pool_mm(dt_b, pw_ref, True)
        dpooled_n = _pool_mm((dypn_ref[...] * ps).astype(BF16), pw_ref, True)
        count_n, _ = _pool_counts(i + 1, tb)
        dq2 = jnp.concatenate([dpooled / count, jnp.where(last, 0.0, dpooled_n / count_n)], axis=0)
        lead = _select_group([s[:tb] for s in _anticausal_window_sums(dq2)], grp)
        dz_ref[:, 1536:2048] = (lead - dpooled).astype(BF16)

    cur = lambda c: pl.BlockSpec((tb, 512), lambda i: (i, c))
    prev = lambda c: pl.BlockSpec((tb, 512), lambda i: (jnp.maximum(i - 1, 0), c))
    nxt = lambda c: pl.BlockSpec((tb, 512), lambda i: (jnp.minimum(i + 1, nblk - 1), c))
    full = lambda a: pl.BlockSpec(a.shape, lambda i: (0,) * a.ndim)
    yc_c, yp_c = D_ATTN // 512, D_ATTN // 512 + 1
    return pl.pallas_call(
        body, grid=(nblk,),
        in_specs=[cur(ZC_GB), nxt(ZC_GB), cur(ZC_GC), prev(ZC_GC), cur(ZC_CI), prev(ZC_CI), cur(ZC_PI), prev(ZC_PI),
                  cur(yc_c), nxt(yc_c), cur(yp_c), nxt(yp_c), full(conv_w), full(pool_w), full(pool_scale)],
        out_specs=[pl.BlockSpec((tb, 2048), lambda i: (i, 0)), pl.BlockSpec((3, 512), lambda i: (0, 0)),
                   pl.BlockSpec((ng, POOL_GROUP, POOL_GROUP), lambda i: (0, 0, 0)), pl.BlockSpec((1, 512), lambda i: (0, 0))],
        out_shape=[_sds((S, 2048), BF16), _sds((3, 512), F32), _sds((ng, POOL_GROUP, POOL_GROUP), F32), _sds((1, 512), F32)],
        name=name, compiler_params=_cp(1),
    )(z, z, z, z, z, z, z, z, dycat, dycat, dycat, dycat, conv_w, pool_w, pool_scale)


def assemble_dz(parts_q, parts_k, parts_v, dcp, name):
    S = dcp.shape[0]
    tm = _row_tile(S)
    npat = len(parts_q)

    def body(*refs):
        dz_ref = refs[-1]
        dcp_ref = refs[-2]
        for j in range(3):
            acc = refs[j * npat][...]
            for p in range(1, npat):
                acc = acc + refs[j * npat + p][...]
            dz_ref[:, j * D_ATTN:(j + 1) * D_ATTN] = acc.astype(BF16)
        dz_ref[:, 3 * D_ATTN:] = dcp_ref[...]

    big = pl.BlockSpec((tm, D_ATTN), lambda i: (i, 0))
    return pl.pallas_call(
        body, grid=(S // tm,), in_specs=[big] * (3 * npat) + [pl.BlockSpec((tm, D_IN - 3 * D_ATTN), lambda i: (i, 0))],
        out_specs=pl.BlockSpec((tm, D_IN), lambda i: (i, 0)), out_shape=_sds((S, D_IN), BF16),
        name=name, compiler_params=_cp(1),
    )(*parts_q, *parts_k, *parts_v, dcp)


def adamw(w, g, m, v, name):
    shape = w.shape
    cols = shape[-1]
    rows = w.size // cols
    tr = rows
    for cand in (256, 128, 64, 32, 16, 8):
        if rows % cand == 0:
            tr = cand
            break
    c1 = 1.0 - ADAM_B1 ** ADAM_STEP
    c2 = 1.0 - ADAM_B2 ** ADAM_STEP

    def body(w_ref, g_ref, m_ref, v_ref, d_ref, mo_ref, vo_ref):
        gv = g_ref[...]
        mn = ADAM_B1 * m_ref[...] + (1.0 - ADAM_B1) * gv
        vn = ADAM_B2 * v_ref[...] + (1.0 - ADAM_B2) * (gv * gv)
        d_ref[...] = -ADAM_LR * ((mn / c1) / (jnp.sqrt(vn / c2) + ADAM_EPS) + ADAM_WD * w_ref[...])
        mo_ref[...] = mn
        vo_ref[...] = vn

    blk = pl.BlockSpec((tr, cols), lambda i: (i, 0))
    outs = pl.pallas_call(
        body, grid=(rows // tr,), in_specs=[blk] * 4, out_specs=[blk] * 3,
        out_shape=[_sds((rows, cols), F32)] * 3, name=name, compiler_params=_cp(1),
    )(*(a.reshape(rows, cols) for a in (w, g, m, v)))
    return tuple(o.reshape(shape) for o in outs)


ANY = pl.BlockSpec(memory_space=pl.ANY)


def _place():
    x, y, c = lax.axis_index("x"), lax.axis_index("y"), lax.axis_index("c")
    chips = [(1 - x, y), (x, 1 - y), (1 - x, 1 - y)]
    return x, y, c, chips


def gather_weights(ws):
    nw = len(ws)
    split = [w.ndim == 2 and w.shape[0] % 32 == 0 for w in ws]

    def body(*refs):
        ins, outs = refs[:nw], refs[nw:2 * nw]
        send, recv = refs[2 * nw:]
        x, y, c, chips = _place()
        me, sib = 2 * x + y, (x, y, 1 - c)

        def half(j, k, hc):
            if not split[j]:
                return outs[j].at[k]
            ha = ws[j].shape[0] // 2
            return outs[j].at[k, pl.ds(hc * ha, ha), :]

        def rcopy(j, s, src, dst, to):
            return pltpu.make_async_remote_copy(src_ref=src, dst_ref=dst, send_sem=send.at[j, s], recv_sem=recv.at[j, s],
                                                device_id=to, device_id_type=MESH)

        first = [rcopy(j, 6, ins[j], outs[j].at[me], sib) for j in range(nw)]
        for j in range(nw):
            ha = ws[j].shape[0] // 2
            mine = ins[j].at[pl.ds(c * ha, ha), :] if split[j] else ins[j]
            for s, (px, py) in enumerate(chips):
                first.append(rcopy(j, s, mine, half(j, me, c), (px, py, c)))
        for cp in first:
            cp.start()
        passed = []
        for j in range(nw):
            for s, (px, py) in enumerate(chips):
                blk = half(j, 2 * px + py, c)
                rcopy(j, s, blk, blk, (px, py, c)).wait_recv()
                if split[j]:
                    fwd = rcopy(j, 3 + s, blk, blk, sib)
                    fwd.start()
                    passed.append(fwd)
        for j in range(nw):
            for s, (px, py) in enumerate(chips):
                if split[j]:
                    blk = half(j, 2 * px + py, 1 - c)
                    rcopy(j, 3 + s, blk, blk, sib).wait_recv()
        for j in range(nw):
            rcopy(j, 6, ins[j], outs[j].at[me], sib).wait_recv()
        for cp in first + passed:
            cp.wait_send()

    return pl.pallas_call(
        body, in_specs=[ANY] * nw, out_specs=[ANY] * nw,
        out_shape=[_sds((NSH,) + w.shape, w.dtype) for w in ws],
        scratch_shapes=[pltpu.SemaphoreType.DMA((nw, 7)), pltpu.SemaphoreType.DMA((nw, 7))],
        name="gather_weights",
    )(*ws)


def pair_sum(g, got, half_index, name):
    _, A, B = g.shape
    ha = A // 2
    tr = ha
    for cand in (512, 256, 128, 64):
        if ha % cand == 0:
            tr = cand
            break
    nt = ha // tr

    def body(c_ref, g_ref, r_ref, o_ref):
        del c_ref
        o_ref[...] = (g_ref[...] + r_ref[...]).astype(BF16)

    return pl.pallas_call(
        body,
        grid_spec=pltpu.PrefetchScalarGridSpec(
            num_scalar_prefetch=1, grid=(NSH, nt),
            in_specs=[pl.BlockSpec((None, tr, B), lambda k, t, c: (k, c[0] * nt + t, 0)),
                      pl.BlockSpec((None, tr, B), lambda k, t, c: (k, t, 0))],
            out_specs=pl.BlockSpec((None, tr, B), lambda k, t, c: (k, t, 0))),
        out_shape=_sds((NSH, ha, B), BF16), name=name, compiler_params=_cp(2),
    )(half_index, g, got)


def _half_tile(ha):
    for cand in (512, 256, 128, 64):
        if ha % cand == 0:
            return cand
    return ha


def chip_sum(partials, slots, chip_index, name):
    depth = len(partials)
    _, ha, B = partials[0].shape
    tr = _half_tile(ha)
    nt = ha // tr

    def body(me_ref, *refs):
        p_refs, s_refs, o_ref = refs[:depth], refs[depth:depth + depth * NSH], refs[depth + depth * NSH]
        l = pl.program_id(0)
        for ll in range(depth):
            @pl.when(l == ll)
            def _(ll=ll):
                own = p_refs[ll][...].astype(F32)
                acc = jnp.where(me_ref[0] == 0, own, s_refs[ll * NSH][...].astype(F32))
                for k in range(1, NSH):
                    acc = acc + jnp.where(me_ref[0] == k, own, s_refs[ll * NSH + k][...].astype(F32))
                o_ref[...] = acc

    def frozen(ll):
        return lambda l, t: jnp.where(l == ll, t, jnp.where(l < ll, 0, nt - 1))

    def slot(k):
        return lambda me: jnp.where(me[0] == k, (k + 1) % NSH, k)

    in_specs = [pl.BlockSpec((None, tr, B), lambda l, t, me, ll=ll: (me[0], frozen(ll)(l, t), 0)) for ll in range(depth)]
    in_specs += [pl.BlockSpec((None, tr, B), lambda l, t, me, ll=ll, k=k: (slot(k)(me), frozen(ll)(l, t), 0))
                 for ll in range(depth) for k in range(NSH)]
    return pl.pallas_call(
        body,
        grid_spec=pltpu.PrefetchScalarGridSpec(
            num_scalar_prefetch=1, grid=(depth, nt), in_specs=in_specs,
            out_specs=pl.BlockSpec((None, tr, B), lambda l, t, me: (l, t, 0))),
        out_shape=_sds((depth, ha, B), F32), name=name, compiler_params=_cp(2),
    )(chip_index, *partials, *[s for s in slots for _ in range(NSH)])


def sibling_swap(mine):
    n = len(mine)

    def body(*refs):
        ins, outs = refs[:n], refs[n:2 * n]
        send, recv = refs[2 * n:]
        x, y, c, _ = _place()
        cps = [pltpu.make_async_remote_copy(src_ref=ins[j], dst_ref=outs[j], send_sem=send.at[j], recv_sem=recv.at[j],
                                            device_id=(x, y, 1 - c), device_id_type=MESH) for j in range(n)]
        for cp in cps:
            cp.start()
        for cp in cps:
            cp.wait()

    return pl.pallas_call(
        body, in_specs=[ANY] * n, out_specs=[ANY] * n, out_shape=[_sds(m.shape, m.dtype) for m in mine],
        scratch_shapes=[pltpu.SemaphoreType.DMA((n,)), pltpu.SemaphoreType.DMA((n,))], name="sibling_swap",
    )(*mine)


def adamw_halves(w, mine, theirs, m, v, core_index, name):
    depth, A, B = w.shape
    ha = A // 2
    tr = _half_tile(ha)
    while tr * B * 4 > 2 ** 20 and tr % 16 == 0:
        tr //= 2
    nt = ha // tr
    c1 = 1.0 - ADAM_B1 ** ADAM_STEP
    c2 = 1.0 - ADAM_B2 ** ADAM_STEP

    def body(c_ref, w_ref, a_ref, b_ref, m_ref, v_ref, g_ref, d_ref, mo_ref, vo_ref):
        gv = jnp.where(pl.program_id(1) == c_ref[0], a_ref[...], b_ref[...])
        mn = ADAM_B1 * m_ref[...] + (1.0 - ADAM_B1) * gv
        vn = ADAM_B2 * v_ref[...] + (1.0 - ADAM_B2) * (gv * gv)
        g_ref[...] = gv
        d_ref[...] = -ADAM_LR * ((mn / c1) / (jnp.sqrt(vn / c2) + ADAM_EPS) + ADAM_WD * w_ref[...])
        mo_ref[...] = mn
        vo_ref[...] = vn

    full = pl.BlockSpec((None, tr, B), lambda l, h, t, c: (l, h * nt + t, 0))
    a_spec = pl.BlockSpec((None, tr, B), lambda l, h, t, c: (l, jnp.where(h == c[0], t, 0), 0))
    b_spec = pl.BlockSpec((None, tr, B), lambda l, h, t, c: (l, jnp.where(h == c[0], 0, t), 0))
    return pl.pallas_call(
        body,
        grid_spec=pltpu.PrefetchScalarGridSpec(
            num_scalar_prefetch=1, grid=(depth, 2, nt), in_specs=[full, a_spec, b_spec, full, full], out_specs=[full] * 4),
        out_shape=[_sds(w.shape, F32)] * 4, name=name, compiler_params=_cp(3),
    )(core_index, w, mine, theirs, m, v)


def small_all_reduce(v):
    R = v.shape[0]

    def body(v_ref, o_ref, slots, send, recv):
        x, y, c, _ = _place()
        me = 4 * x + 2 * y + c
        slots[me] = v_ref[...]
        cps = []
        for m in range(1, 8):
            mx, my, mc = (m >> 2) & 1, (m >> 1) & 1, m & 1
            cps.append(pltpu.make_async_remote_copy(
                src_ref=v_ref, dst_ref=slots.at[me], send_sem=send.at[m - 1], recv_sem=recv.at[m - 1],
                device_id=(x ^ mx, y ^ my, c ^ mc), device_id_type=MESH))
        for cp in cps:
            cp.start()
        for cp in cps:
            cp.wait()
        acc = slots[0]
        for d in range(1, 8):
            acc = acc + slots[d]
        o_ref[...] = acc

    vm = pl.BlockSpec(memory_space=pltpu.VMEM)
    return pl.pallas_call(
        body, in_specs=[vm], out_specs=vm, out_shape=_sds((R, 128), F32),
        scratch_shapes=[pltpu.VMEM((8, R, 128), F32), pltpu.SemaphoreType.DMA((7,)), pltpu.SemaphoreType.DMA((7,))],
        name="small_all_reduce",
    )(v)


BIG = ("ffn1_w_gate", "ffn1_w_up", "ffn1_w_down", "w_in", "w_out", "ffn2_w_gate", "ffn2_w_up", "ffn2_w_down")
SMALL = ("ffn1_norm", "mix_norm", "conv_w", "pool_w", "pool_scale", "ffn2_norm", "final_norm")
WEIGHTS = ("ffn1_norm", "ffn1_w_gate", "ffn1_w_up", "ffn1_w_down", "mix_norm", "w_in", "conv_w", "pool_w", "pool_scale",
           "w_out", "ffn2_norm", "ffn2_w_gate", "ffn2_w_up", "ffn2_w_down", "final_norm")


def _tile_for(n, cap=1024):
    best = 128
    for t in range(128, min(n, cap) + 1, 128):
        if n % t == 0:
            best = t
    return n if n <= cap else best


GATHER_BEHIND = {
    (0, "ffn1_up"): [("w_in", 0), ("w_out", 0), ("ffn2_w_gate", 0), ("ffn2_w_up", 0)],
    (0, "ffn1_down"): [("ffn2_w_down", 0)],
    (0, "mix_in"): [("ffn1_w_gate", 1)],
    (0, "mix_out"): [("ffn1_w_up", 1)],
    (0, "ffn2_up"): [("ffn1_w_down", 1), ("w_in", 1), ("w_out", 1)],
    (0, "ffn2_down"): [("ffn2_w_gate", 1)],
    (1, "ffn1_up"): [("ffn2_w_up", 1), ("ffn2_w_down", 1)],
}


class _GatherPlan:
    def __init__(self, local):
        self.local, self.pending, self.ready = local, [], {}

    def comm(self, firsts):
        cm = _Comm()
        self._passing, self._firsts = list(self.pending), list(firsts)
        for _, buf in self._passing:
            cm.gather_pass(buf)
        for key in self._firsts:
            cm.gather_first(self.local[key])
        return cm

    def done(self, couts):
        npass = len(self._passing)
        for (key, _), buf in zip(self._passing, couts[:npass]):
            self.ready[key] = buf
        self.pending = list(zip(self._firsts, couts[npass:]))


def kernel(x, ffn1_norm, ffn1_w_gate, ffn1_w_up, ffn1_w_down, mix_norm, w_in, conv_w, pool_w, pool_scale, w_out, ffn2_norm, ffn2_w_gate, ffn2_w_up, ffn2_w_down, final_norm, loss_target, m_ffn1_norm, m_ffn1_w_gate, m_ffn1_w_up, m_ffn1_w_down, m_mix_norm, m_w_in, m_conv_w, m_pool_w, m_pool_scale, m_w_out, m_ffn2_norm, m_ffn2_w_gate, m_ffn2_w_up, m_ffn2_w_down, m_final_norm, v_ffn1_norm, v_ffn1_w_gate, v_ffn1_w_up, v_ffn1_w_down, v_mix_norm, v_w_in, v_conv_w, v_pool_w, v_pool_scale, v_w_out, v_ffn2_norm, v_ffn2_w_gate, v_ffn2_w_up, v_ffn2_w_down, v_final_norm):
    given = dict(locals())
    W = {n: given[n] for n in WEIGHTS}
    M = {n: given["m_" + n] for n in WEIGHTS}
    V = {n: given["v_" + n] for n in WEIGHTS}
    depth = ffn1_norm.shape[0]
    D = x.shape[-1]
    xs = x[0]
    my_chip = 2 * lax.axis_index("x") + lax.axis_index("y")
    my_core = lax.axis_index("c")

    half_index = my_core.astype(jnp.int32).reshape(1)
    chip_index = my_chip.astype(jnp.int32).reshape(1)
    plan = _GatherPlan({(n, l): W[n][l].astype(BF16) for n in BIG for l in range(depth)})
    head = [("ffn1_w_gate", 0), ("ffn1_w_up", 0), ("ffn1_w_down", 0)]
    gathered = gather_weights([plan.local[key] for key in head] + [conv_w])
    plan.ready.update(zip(head, gathered[:-1]))
    conv_full = jnp.moveaxis(gathered[-1], 0, 2).reshape(depth, 3, D_CONV)
    pool_b = pool_w.astype(BF16)
    G = plan.ready

    def behind(l, stage):
        return plan.comm([(n, ll) for n, ll in GATHER_BEHIND.get((l, stage), []) if ll < depth])

    saved = []
    cur = xs
    for l in range(depth):
        x0 = cur
        (h, g, u, a), got = ffn_up(x0, ffn1_norm[l], G["ffn1_w_gate", l], G["ffn1_w_up", l], "ffn1_up", behind(l, "ffn1_up"))
        plan.done(got)
        cur, got = resid_mm(a, G["ffn1_w_down", l], x0, 0.5, "ffn1_down", behind(l, "ffn1_down"))
        plan.done(got)
        s1 = (x0, h, g, u, a)

        x1 = cur
        (hm, z), got = col_mm(x1, mix_norm[l], G["w_in", l], "mix_in", behind(l, "mix_in"))
        plan.done(got)
        pats = [attn_fwd(z, d, f"attn_fwd_d{d}") for d in DILATIONS]
        y, yb, lse_tot = attn_merge([p[0] for p in pats], [p[1] for p in pats], "attn_merge")
        y_conv, y_pool = conv_pool_fwd(z, conv_full[l], pool_b[l], pool_scale[l].reshape(1, D_POOL), "conv_pool_fwd")
        ycat = jnp.concatenate([yb, y_conv, y_pool], axis=1)
        cur, got = resid_mm(ycat, G["w_out", l], x1, 1.0, "mix_out", behind(l, "mix_out"))
        plan.done(got)
        sm = (x1, hm, z, y, lse_tot, ycat)

        x2 = cur
        (h, g, u, a), got = ffn_up(x2, ffn2_norm[l], G["ffn2_w_gate", l], G["ffn2_w_up", l], "ffn2_up", behind(l, "ffn2_up"))
        plan.done(got)
        cur, got = resid_mm(a, G["ffn2_w_down", l], x2, 0.5, "ffn2_down", behind(l, "ffn2_down"))
        plan.done(got)
        saved.append((s1, sm, (x2, h, g, u, a)))
    assert not plan.pending and len(G) == len(BIG) * depth

    loss11, dx, dxb, d_final = final_loss(cur, final_norm, loss_target[0], "final_loss")

    small_grads = {n: [None] * depth for n in SMALL if n != "final_norm"}
    partials = {n: [None] * depth for n in BIG}
    slots = {n: [None] * depth for n in BIG}

    def halves(grad):
        cm = _Comm()
        cm.sibling_half(grad)
        return cm

    carry = []

    def carried_comm():
        cm = _Comm()
        taken = list(carry)
        carry.clear()
        for _, _, p in taken:
            cm.exchange(p)
        return cm, taken

    def settle(taken, got):
        for (n, ll, p), s in zip(taken, got):
            partials[n][ll], slots[n][ll] = p, s

    def finish_sublayer(name, l, grad, dys, ws, x_in, gain, dx, tag, cm, final):
        if not final:
            dx, dxb, dnorm, got, _ = bwd_dh(dys, ws, x_in, gain, dx, tag, cm)
            carry.append((name, l, pair_sum(grad, got[0], half_index, "pair_sum")))
            return dx, dxb, dnorm, got
        last = {}

        def after(got):
            last["p"] = pair_sum(grad, got[0], half_index, "pair_sum")
            cm2 = _Comm()
            cm2.exchange(last["p"])
            return cm2

        dx, dxb, dnorm, got, got2 = bwd_dh(dys, ws, x_in, gain, dx, tag, cm, after)
        partials[name][l], slots[name][l] = last["p"], got2[0]
        return dx, dxb, dnorm, got

    def ffn_backward(sv, dx, dxb, norm, tag, l, final=False):
        x_in, h, g, u, a = sv
        names = [f"{tag}_w_down", f"{tag}_w_gate", f"{tag}_w_up"]
        wd, wg, wu = (G[n, l] for n in names)
        FS = wg.shape[-1]
        cm0, taken = carried_comm()
        (dg, du), got0 = ffn_bwd_act(dxb, wd, g, u, f"{tag}_bwd_act", cm0)
        settle(taken, got0)
        g_d, _ = wgrad(a, dxb, True, False, 0.5, FS, _tile_for(D), f"{tag}_dwd")
        g_g, (got_d,) = wgrad(h, dg, False, True, 1.0, _tile_for(D), FS, f"{tag}_dwg", halves(g_d))
        g_u, (got_g,) = wgrad(h, du, False, True, 1.0, _tile_for(D), FS, f"{tag}_dwu", halves(g_g))
        p_d = pair_sum(g_d, got_d, half_index, "pair_sum")
        p_g = pair_sum(g_g, got_g, half_index, "pair_sum")
        cm = halves(g_u)
        cm.exchange(p_d)
        cm.exchange(p_g)
        dx, dxb, dnorm, got = finish_sublayer(names[2], l, g_u, [dg, du], [wg, wu], x_in, norm[l], dx, f"{tag}_bwd_dh", cm, final)
        settle([(names[0], l, p_d), (names[1], l, p_g)], got[1:])
        return dx, dxb, dnorm

    for l in reversed(range(depth)):
        s1, sm, s2 = saved[l]
        dx, dxb, small_grads["ffn2_norm"][l] = ffn_backward(s2, dx, dxb, ffn2_norm, "ffn2", l)

        x1, hm, z, y, lse_tot, ycat = sm
        cm0, taken = carried_comm()
        dycat, got0 = nt_col_mm(dxb, G["w_out", l], "mix_out_bwd", cm0)
        settle(taken, got0)
        g_wout, _ = wgrad(ycat, dxb, True, False, 1.0, D_MIX // NSH, _tile_for(D, 2048), "mix_dwout")
        parts = [attn_bwd(z, dycat, y, lse_tot, d, f"attn_bwd_d{d}") for d in DILATIONS]
        dcp, dcw, dpw, dps = conv_pool_bwd(z, dycat, conv_full[l], pool_b[l], pool_scale[l].reshape(1, D_POOL), "conv_pool_bwd")
        dz = assemble_dz([p[0] for p in parts], [p[1] for p in parts], [p[2] for p in parts], dcp, "assemble_dz")
        g_win, (got_out,) = wgrad(hm, dz, False, True, 1.0, _tile_for(D), D_IN // NSH, "mix_dwin", halves(g_wout))
        p_out = pair_sum(g_wout, got_out, half_index, "pair_sum")
        cm = halves(g_win)
        cm.exchange(p_out)
        dx, dxb, dnm, got = finish_sublayer("w_in", l, g_win, [dz], [G["w_in", l]], x1, mix_norm[l], dx, "mix_bwd_dh", cm, False)
        settle([("w_out", l, p_out)], got[1:])
        small_grads["mix_norm"][l], small_grads["conv_w"][l] = dnm, dcw
        small_grads["pool_w"][l], small_grads["pool_scale"][l] = dpw, dps

        dx, dxb, small_grads["ffn1_norm"][l] = ffn_backward(s1, dx, dxb, ffn1_norm, "ffn1", l, final=(l == 0))
    assert not carry

    mine = [chip_sum(partials[n], slots[n], chip_index, "chip_sum") for n in BIG]
    theirs = sibling_swap(mine)
    grads, delta, new_m, new_v = {}, {}, {}, {}
    for n, a, b in zip(BIG, mine, theirs):
        grads[n], delta[n], new_m[n], new_v[n] = adamw_halves(W[n], a, b, M[n], V[n], half_index, "adamw")

    small_full = {n: jnp.stack([a.reshape(W[n].shape[1:] if n != "conv_w" else (3, D_CONV)) for a in small_grads[n]])
                  for n in small_grads}
    small_full["final_norm"] = d_final.reshape(D)
    order = list(SMALL)
    packed = jnp.concatenate([small_full[n].reshape(-1) for n in order])
    pad = (-packed.shape[0]) % (8 * 128)
    packed = jnp.pad(packed, (0, pad)).reshape(-1, 128)
    summed = small_all_reduce(packed).reshape(-1)
    off = 0
    for n in order:
        size = small_full[n].size
        grads[n] = summed[off:off + size].reshape(small_full[n].shape)
        off += size
    grads["conv_w"] = lax.dynamic_slice_in_dim(grads["conv_w"], my_chip * (D_CONV // NSH), D_CONV // NSH, axis=2)

    def pack(src):
        flat_ = jnp.concatenate([src[n].reshape(-1) for n in order])
        return jnp.pad(flat_, (0, (-flat_.shape[0]) % (8 * 128))).reshape(-1, 128)

    ds, ms, vs = adamw(pack(W), pack(grads), pack(M), pack(V), "adamw_small")
    off = 0
    for n in order:
        size = W[n].size
        for dst, src in ((delta, ds), (new_m, ms), (new_v, vs)):
            dst[n] = src.reshape(-1)[off:off + size].reshape(W[n].shape)
        off += size

    loss = lax.psum(loss11[0, 0], ("x", "y", "c"))
    return (loss, dx.reshape(x.shape), *[grads[n] for n in WEIGHTS], *[delta[n] for n in WEIGHTS],
            *[new_m[n] for n in WEIGHTS], *[new_v[n] for n in WEIGHTS])
```

```python
import functools

import jax
import jax.numpy as jnp
from jax import lax
from jax.experimental import pallas as pl
from jax.experimental.pallas import tpu as pltpu

F32 = jnp.float32
BF16 = jnp.bfloat16
MESH = pl.DeviceIdType.MESH

RMS_EPS = 1e-6
NEG_INF = -1e30
HEAD_DIM = 64
BLK = 128
SPAN = 128
DILATIONS = (1, 4, 16)
D_ATTN = 1024
D_CONV = 512
D_POOL = 512
POOL_WINDOWS = (2, 4, 8, 16)
POOL_GROUP = 128
D_IN = 3 * D_ATTN + 3 * D_CONV + D_POOL
D_MIX = D_ATTN + D_CONV + D_POOL
N_PAIR = D_ATTN // 128
ATTN_SCALE = HEAD_DIM ** -0.5
NSH = 4
ADAM_LR, ADAM_B1, ADAM_B2, ADAM_EPS, ADAM_WD, ADAM_STEP = 0.001, 0.9, 0.999, 1e-08, 0.01, 10

VMEM_LIMIT = 56 * 2 ** 20


def _cp(n_axes):
    return pltpu.CompilerParams(dimension_semantics=("arbitrary",) * n_axes, vmem_limit_bytes=VMEM_LIMIT)


def _sds(shape, dtype):
    return jax.ShapeDtypeStruct(shape, dtype)


def _dot(a, b):
    return jnp.dot(a, b, preferred_element_type=F32)


def _dot_nt(a, b):
    return lax.dot_general(a, b, (((1,), (1,)), ((), ())), preferred_element_type=F32)


def _dot_tn(a, b):
    return lax.dot_general(a, b, (((0,), (0,)), ((), ())), preferred_element_type=F32)


def _row_tile(s):
    return min(512, s)


def _rms_bwd_tile(xv, gv, dh):
    r = lax.rsqrt(jnp.mean(xv * xv, axis=-1, keepdims=True) + RMS_EPS)
    xhat = xv * r
    dg = jnp.sum(dh * xhat, axis=0, keepdims=True)
    dxhat = dh * gv
    dx = r * (dxhat - xhat * jnp.mean(dxhat * xhat, axis=-1, keepdims=True))
    return dx, dg


def final_loss(x, g, target, name):
    S, D = x.shape
    tm = _row_tile(S)

    def body(x_ref, g_ref, t_ref, loss_ref, dx_ref, dxb_ref, dg_ref):
        i = pl.program_id(0)
        xv, gv = x_ref[...], g_ref[...]
        r = lax.rsqrt(jnp.mean(xv * xv, axis=-1, keepdims=True) + RMS_EPS)
        err = xv * r * gv - t_ref[...]
        part = 0.5 * jnp.sum(jnp.mean(err * err, axis=-1, keepdims=True), axis=0, keepdims=True)
        dx, dg = _rms_bwd_tile(xv, gv, err * (1.0 / D))

        @pl.when(i == 0)
        def _():
            loss_ref[...] = jnp.zeros_like(loss_ref)
            dg_ref[...] = jnp.zeros_like(dg_ref)

        loss_ref[...] += part
        dg_ref[...] += dg
        dx_ref[...] = dx
        dxb_ref[...] = dx.astype(BF16)

    row = pl.BlockSpec((tm, D), lambda i: (i, 0))
    vec = pl.BlockSpec((1, D), lambda i: (0, 0))
    return pl.pallas_call(
        body, grid=(S // tm,), in_specs=[row, vec, row],
        out_specs=[pl.BlockSpec((1, 1), lambda i: (0, 0)), row, row, vec],
        out_shape=[_sds((1, 1), F32), _sds((S, D), F32), _sds((S, D), BF16), _sds((1, D), F32)],
        name=name, compiler_params=_cp(1),
    )(x, g.reshape(1, D), target)


def _wspec(w, imap):
    _, a, b = w.shape
    return pl.BlockSpec((None, a, b), lambda *ids: (imap(*ids), 0, 0))


class _Comm:
    def __init__(self):
        self.ins, self.out_shapes, self.aliases, self.items = [], [], {}, []

    def _add(self, kind, operand, out_shape, alias):
        if alias:
            self.aliases[len(self.ins)] = len(self.out_shapes)
        self.items.append((kind, len(self.ins), len(self.out_shapes)))
        self.ins.append(operand)
        self.out_shapes.append(out_shape)

    def gather_first(self, src):
        self._add("first", src, _sds((NSH,) + src.shape, src.dtype), False)

    def gather_pass(self, buf):
        self._add("pass", buf, _sds(buf.shape, buf.dtype), True)

    def sibling_half(self, grad):
        self._add("half", grad, _sds((NSH, grad.shape[1] // 2, grad.shape[2]), grad.dtype), False)

    def exchange(self, partial):
        self._add("xchg", partial, _sds(partial.shape, partial.dtype), False)

    def run(self, cins, couts, send, recv, start):
        x, y, c = lax.axis_index("x"), lax.axis_index("y"), lax.axis_index("c")
        chips = [(1 - x, y), (x, 1 - y), (1 - x, 1 - y)]
        me, sib = 2 * x + y, (x, y, 1 - c)
        for it, (kind, i, o) in enumerate(self.items):
            def rc(s, src, dst, to, it=it):
                return pltpu.make_async_remote_copy(src_ref=src, dst_ref=dst, send_sem=send.at[it, s], recv_sem=recv.at[it, s],
                                                    device_id=to, device_id_type=MESH)
            src, buf = cins[i], couts[o]
            if kind == "first":
                ha = src.shape[0] // 2
                rows = pl.ds(c * ha, ha)
                cps = [rc(s, src.at[rows], buf.at[me, rows], (px, py, c)) for s, (px, py) in enumerate(chips)]
                cps.append(rc(3, src, buf.at[me], sib))
                landing = [buf.at[2 * px + py, rows] for px, py in chips] + [buf.at[me]]
            elif kind == "pass":
                ha = buf.shape[1] // 2
                rows, other = pl.ds(c * ha, ha), pl.ds((1 - c) * ha, ha)
                cps = [rc(s, buf.at[2 * px + py, rows], buf.at[2 * px + py, rows], sib) for s, (px, py) in enumerate(chips)]
                landing = [buf.at[2 * px + py, other] for px, py in chips]
            elif kind == "half":
                ha = src.shape[1] // 2
                cps = [rc(0, src.at[:, pl.ds((1 - c) * ha, ha), :], buf, sib)]
                landing = [buf]
            else:
                cps = [rc(s, src.at[2 * px + py], buf.at[me], (px, py, c)) for s, (px, py) in enumerate(chips)]
                landing = [buf.at[2 * px + py] for px, py in chips]
            if start:
                for cp in cps:
                    cp.start()
            else:
                for s, dst in enumerate(landing):
                    rc(s, dst, dst, sib).wait_recv()
                for cp in cps:
                    cp.wait_send()


def _call(body, *, grid, in_specs, out_specs, out_shape, name, args, scratch=(), comm=None):
    multi = isinstance(out_shape, (list, tuple))
    oshape = list(out_shape) if multi else [out_shape]
    ospecs = list(out_specs) if multi else [out_specs]
    if comm is None or not comm.items:
        res = pl.pallas_call(body, grid=grid, in_specs=list(in_specs), out_specs=ospecs, out_shape=oshape,
                             scratch_shapes=list(scratch), name=name, compiler_params=_cp(len(grid)))(*args)
        return (list(res) if multi else res[0]), []
    nin, nout, nci, nco, nscr = len(in_specs), len(oshape), len(comm.ins), len(comm.out_shapes), len(scratch)

    def full(*refs):
        ins, cins = refs[:nin], refs[nin:nin + nci]
        outs, couts = refs[nin + nci:nin + nci + nout], refs[nin + nci + nout:nin + nci + nout + nco]
        scr = refs[nin + nci + nout + nco:nin + nci + nout + nco + nscr]
        send, recv = refs[-2:]
        ids = [pl.program_id(a) for a in range(len(grid))]
        first = functools.reduce(jnp.logical_and, [i == 0 for i in ids])
        last = functools.reduce(jnp.logical_and, [i == g - 1 for i, g in zip(ids, grid)])

        @pl.when(first)
        def _():
            comm.run(cins, couts, send, recv, True)

        body(*ins, *outs, *scr)

        @pl.when(last)
        def _():
            comm.run(cins, couts, send, recv, False)

    sems = pltpu.SemaphoreType.DMA((len(comm.items), 4))
    res = pl.pallas_call(
        full, grid=grid, in_specs=list(in_specs) + [ANY] * nci, out_specs=ospecs + [ANY] * nco,
        out_shape=oshape + comm.out_shapes, scratch_shapes=list(scratch) + [sems, sems],
        input_output_aliases={nin + i: nout + o for i, o in comm.aliases.items()},
        name=name, compiler_params=_cp(len(grid)),
    )(*args, *comm.ins)
    main = list(res[:nout])
    return (main if multi else main[0]), list(res[nout:])


def rms_fwd(x, g, name):
    S, D = x.shape
    tm = _row_tile(S)

    def body(x_ref, g_ref, h_ref):
        xv = x_ref[...]
        r = lax.rsqrt(jnp.mean(xv * xv, axis=-1, keepdims=True) + RMS_EPS)
        h_ref[...] = (xv * r * g_ref[...]).astype(BF16)

    return pl.pallas_call(
        body, grid=(S // tm,),
        in_specs=[pl.BlockSpec((tm, D), lambda i: (i, 0)), pl.BlockSpec((1, D), lambda i: (0, 0))],
        out_specs=pl.BlockSpec((tm, D), lambda i: (i, 0)),
        out_shape=_sds((S, D), BF16), name=name, compiler_params=_cp(1),
    )(x, g.reshape(1, D))


def ffn_up(h, wg, wu, name, comm=None):
    S, D = h.shape
    FS = wg.shape[-1]
    tm = _row_tile(S)

    def body(h_ref, wg_ref, wu_ref, sg_ref, up_ref, a_ref):
        hv = h_ref[...]
        g = _dot(hv, wg_ref[...])
        u = _dot(hv, wu_ref[...])
        s = jax.nn.sigmoid(g)
        sg = g * s
        sg_ref[...] = sg.astype(BF16)
        up_ref[...] = (u * s * (1.0 + g * (1.0 - s))).astype(BF16)
        a_ref[...] = (sg * u).astype(BF16)

    out = pl.BlockSpec((tm, FS), lambda k, i: (i, k))
    shard = lambda k, i: k
    return _call(
        body, grid=(NSH, S // tm),
        in_specs=[pl.BlockSpec((tm, D), lambda k, i: (i, 0)), _wspec(wg, shard), _wspec(wu, shard)],
        out_specs=[out] * 3, out_shape=[_sds((S, NSH * FS), BF16)] * 3,
        name=name, args=(h, wg, wu), comm=comm)


def col_mm(h, w, name, comm=None):
    S, D = h.shape
    NS = w.shape[-1]
    tm = _row_tile(S)

    def body(h_ref, w_ref, z_ref):
        z_ref[...] = _dot(h_ref[...], w_ref[...])

    return _call(
        body, grid=(NSH, S // tm),
        in_specs=[pl.BlockSpec((tm, D), lambda k, i: (i, 0)), _wspec(w, lambda k, i: k)],
        out_specs=pl.BlockSpec((tm, NS), lambda k, i: (i, k)), out_shape=_sds((S, NSH * NS), F32),
        name=name, args=(h, w), comm=comm)


def resid_mm(a, w, x, scale, name, comm=None):
    S, K = a.shape
    D = w.shape[-1]
    tm = _row_tile(S)
    tn = D // 2 if D % 256 == 0 else D

    def body(a_ref, w_ref, x_ref, o_ref):
        o_ref[...] = x_ref[...] + scale * _dot(a_ref[...], w_ref[...])

    out = pl.BlockSpec((tm, tn), lambda j, i: (i, j))
    return _call(
        body, grid=(D // tn, S // tm),
        in_specs=[pl.BlockSpec((tm, K), lambda j, i: (i, 0)), pl.BlockSpec((K, tn), lambda j, i: (0, j)), out],
        out_specs=out, out_shape=_sds((S, D), F32), name=name, args=(a, w.reshape(K, D), x), comm=comm)


def ffn_bwd_act(dxb, wd, sg, up, name, comm=None):
    S, D = dxb.shape
    FS = wd.shape[-2]
    tm = _row_tile(S)

    def body(dx_ref, w_ref, sg_ref, up_ref, dg_ref, du_ref):
        da = 0.5 * _dot_nt(dx_ref[...], w_ref[...])
        du_ref[...] = (da * sg_ref[...].astype(F32)).astype(BF16)
        dg_ref[...] = (da * up_ref[...].astype(F32)).astype(BF16)

    act = pl.BlockSpec((tm, FS), lambda k, i: (i, k))
    return _call(
        body, grid=(NSH, S // tm),
        in_specs=[pl.BlockSpec((tm, D), lambda k, i: (i, 0)), _wspec(wd, lambda k, i: k), act, act],
        out_specs=[act, act], out_shape=[_sds((S, NSH * FS), BF16)] * 2,
        name=name, args=(dxb, wd, sg, up), comm=comm)


def nt_col_mm(dxb, w, name, comm=None):
    S, D = dxb.shape
    KS = w.shape[-2]
    tm = _row_tile(S)

    def body(dx_ref, w_ref, o_ref):
        o_ref[...] = _dot_nt(dx_ref[...], w_ref[...])

    return _call(
        body, grid=(NSH, S // tm),
        in_specs=[pl.BlockSpec((tm, D), lambda k, i: (i, 0)), _wspec(w, lambda k, i: k)],
        out_specs=pl.BlockSpec((tm, KS), lambda k, i: (i, k)), out_shape=_sds((S, NSH * KS), F32),
        name=name, args=(dxb, w), comm=comm)


def wgrad(lhs, rhs, lhs_sharded, rhs_sharded, scale, tr, tc, name, comm=None):
    S = lhs.shape[0]
    R = lhs.shape[1] // (NSH if lhs_sharded else 1)
    C = rhs.shape[1] // (NSH if rhs_sharded else 1)
    ts = min(2048, S)
    nr, nc = R // tr, C // tc

    def body(l_ref, r_ref, o_ref):
        @pl.when(pl.program_id(3) == 0)
        def _():
            o_ref[...] = jnp.zeros_like(o_ref)

        o_ref[...] += scale * _dot_tn(l_ref[...], r_ref[...])

    lmap = (lambda k, a, b, s: (s, k * nr + a)) if lhs_sharded else (lambda k, a, b, s: (s, a))
    rmap = (lambda k, a, b, s: (s, k * nc + b)) if rhs_sharded else (lambda k, a, b, s: (s, b))
    return _call(
        body, grid=(NSH, nr, nc, S // ts),
        in_specs=[pl.BlockSpec((ts, tr), lmap), pl.BlockSpec((ts, tc), rmap)],
        out_specs=pl.BlockSpec((None, tr, tc), lambda k, a, b, s: (k, a, b)),
        out_shape=_sds((NSH, R, C), F32), name=name, args=(lhs, rhs), comm=comm)


def bwd_dh(dys, ws, x, g, dxin, name, comm=None, after=None):
    S, D = x.shape
    NS = ws[0].shape[-1]
    tm = _row_tile(S)
    nj = len(dys)

    def mm_body(*refs):
        dy_refs, w_refs, dh_ref = refs[:nj], refs[nj:2 * nj], refs[2 * nj]

        @pl.when(pl.program_id(1) == 0)
        def _():
            dh_ref[...] = jnp.zeros_like(dh_ref)

        for dy_ref, w_ref in zip(dy_refs, w_refs):
            dh_ref[...] += _dot_nt(dy_ref[...], w_ref[...])

    dh, comm_out = _call(
        mm_body, grid=(S // tm, NSH),
        in_specs=[pl.BlockSpec((tm, NS), lambda i, k: (i, k))] * nj + [_wspec(w, lambda i, k: k) for w in ws],
        out_specs=pl.BlockSpec((tm, D), lambda i, k: (i, 0)), out_shape=_sds((S, D), F32),
        name=name + "_mm", args=(*dys, *ws), comm=comm)

    def norm_body(dh_ref, x_ref, g_ref, dxin_ref, dx_ref, dxb_ref, dg_ref):
        dx, dg = _rms_bwd_tile(x_ref[...], g_ref[...], dh_ref[...])
        tot = dxin_ref[...] + dx
        dx_ref[...] = tot
        dxb_ref[...] = tot.astype(BF16)

        @pl.when(pl.program_id(0) == 0)
        def _():
            dg_ref[...] = jnp.zeros_like(dg_ref)

        dg_ref[...] += dg

    row = pl.BlockSpec((tm, D), lambda i: (i, 0))
    vec = pl.BlockSpec((1, D), lambda i: (0, 0))
    comm2 = after(comm_out) if after is not None else None
    (dx, dxb, dgain), comm2_out = _call(
        norm_body, grid=(S // tm,), in_specs=[row, row, vec, row], out_specs=[row, row, vec],
        out_shape=[_sds((S, D), F32), _sds((S, D), BF16), _sds((1, D), F32)],
        name=name + "_norm", args=(dh, x, g.reshape(1, D), dxin), comm=comm2)
    return dx, dxb, dgain, comm_out, comm2_out


PAIRS_PER_STEP = {1: 8, 4: 1, 16: 1}
ZQ, ZK, ZV = 0, D_ATTN // 128, 2 * D_ATTN // 128


def _band_valid(n):
    qi = lax.broadcasted_iota(jnp.int32, (BLK, 2 * BLK), 0)
    kj = lax.broadcasted_iota(jnp.int32, (BLK, 2 * BLK), 1)
    dist = qi + BLK - kj
    return (dist >= 0) & (dist <= SPAN) & ((kj >= BLK) | (n > 0))


PERM_FROM = 8


def _residue_perm(d, transpose):
    q = 128 // d
    a = lax.broadcasted_iota(jnp.int32, (128, 128), 1 if transpose else 0)
    b = lax.broadcasted_iota(jnp.int32, (128, 128), 0 if transpose else 1)
    return (b == (a % q) * d + a // q).astype(BF16)


def _perm_apply(perm, x, n_terms):
    out, rest = None, x
    for t in range(n_terms):
        term = rest.astype(BF16)
        out = _dot(perm, term) if out is None else out + _dot(perm, term)
        if t + 1 < n_terms:
            rest = rest - term.astype(F32)
    return out


def _split_residues(src_ref, dst, d, pp, row0=0, n_terms=1):
    if d < PERM_FROM:
        for r in range(d):
            for p in range(pp):
                dst[r * pp + p, row0:row0 + BLK, :] = src_ref[pl.ds(r, BLK, stride=d), p * 128:(p + 1) * 128]
        return
    perm, q = _residue_perm(d, False), 128 // d
    for c in range(d):
        t = _perm_apply(perm, src_ref[c * 128:(c + 1) * 128, :], n_terms)
        for r in range(d):
            dst[r, row0 + c * q:row0 + (c + 1) * q, :] = t[r * q:(r + 1) * q]


def _merge_residues(bufs, d, pp, c, n_terms=2):
    q = 128 // d
    t = jnp.concatenate([bufs[r, c * q:(c + 1) * q, :] for r in range(d)], axis=0)
    return _perm_apply(_residue_perm(d, True), t, n_terms)


def _lane_pick(stat, lane, idx):
    return jnp.sum(jnp.where(lane == idx, stat, 0.0), axis=-1, keepdims=True)


def _residue_view(dilation):
    return 1, dilation


def _view_shape(shape, dv):
    return (shape[0] // dv, dv, shape[1]) if dv > 1 else tuple(shape)


def _viewed(a, dv):
    return a.reshape(_view_shape(a.shape, dv))


def _view_spec(dv, rows, width, row_of, col_of):
    if dv > 1:
        return pl.BlockSpec((rows, None, width), lambda r, a, b: (row_of(a, b), r, col_of(a, b)))
    return pl.BlockSpec((rows, width), lambda r, a, b: (row_of(a, b), col_of(a, b)))


def attn_fwd(z, dilation, name):
    S = z.shape[0]
    dv, d = _residue_view(dilation)
    R = BLK * d
    nb = S // (BLK * dilation)
    pp = PAIRS_PER_STEP[d]
    G = d * pp
    W = 128 * pp

    def body(q_ref, kc_ref, kp_ref, vc_ref, vp_ref, o_ref, lse_ref, qs, ks, vs, os_, ls):
        n, hb = pl.program_id(1), pl.program_id(2)
        valid = _band_valid(n)
        lane = lax.broadcasted_iota(jnp.int32, (1, 128), 1)
        _split_residues(q_ref, qs, d, pp)
        _split_residues(kp_ref, ks, d, pp)
        _split_residues(kc_ref, ks, d, pp, BLK)
        _split_residues(vp_ref, vs, d, pp)
        _split_residues(vc_ref, vs, d, pp, BLK)

        def step(j, carry):
            q2, k2, v2 = qs[j].astype(BF16), ks[j].astype(BF16), vs[j].astype(BF16)
            pair = hb * pp + j % pp
            o2 = jnp.zeros((BLK, 128), F32)
            stat = jnp.zeros((BLK, 128), F32)
            for h in range(2):
                hm = (lane < HEAD_DIM) if h == 0 else (lane >= HEAD_DIM)
                qm = jnp.where(hm, q2, jnp.zeros_like(q2))
                vm = jnp.where(hm, v2, jnp.zeros_like(v2))
                s = jnp.where(valid, _dot_nt(qm, k2) * ATTN_SCALE, NEG_INF)
                m = jnp.max(s, axis=-1, keepdims=True)
                p = jnp.exp(s - m)
                lsum = jnp.sum(p, axis=-1, keepdims=True)
                o2 = o2 + _dot(p.astype(BF16), vm) / lsum
                stat = jnp.where(lane == 2 * pair + h, m + jnp.log(lsum), stat)
            os_[j] = o2
            ls[j] = stat
            return carry

        lax.fori_loop(0, G, step, 0, unroll=min(G, 8))

        @pl.when(hb == 0)
        def _():
            lse_ref[...] = jnp.zeros_like(lse_ref)

        if d < PERM_FROM:
            for r in range(d):
                rows = pl.ds(r, BLK, stride=d)
                acc = lse_ref[rows, :]
                for p in range(pp):
                    o_ref[rows, p * 128:(p + 1) * 128] = os_[r * pp + p]
                    acc = acc + ls[r * pp + p]
                lse_ref[rows, :] = acc
        else:
            for c in range(d):
                rows = slice(c * 128, (c + 1) * 128)
                o_ref[rows, :] = _merge_residues(os_, d, pp, c)
                lse_ref[rows, :] += _merge_residues(ls, d, pp, c, n_terms=3)

    cur = lambda c: _view_spec(dv, R, W, lambda n, hb: n, lambda n, hb: c // pp + hb)
    prev = lambda c: _view_spec(dv, R, W, lambda n, hb: jnp.maximum(n - 1, 0), lambda n, hb: c // pp + hb)
    zv = _viewed(z, dv)
    o, lse = pl.pallas_call(
        body, grid=(dv, nb, N_PAIR // pp),
        in_specs=[cur(ZQ), cur(ZK), prev(ZK), cur(ZV), prev(ZV)],
        out_specs=[_view_spec(dv, R, W, lambda n, hb: n, lambda n, hb: hb), _view_spec(dv, R, 128, lambda n, hb: n, lambda n, hb: 0)],
        out_shape=[_sds(_view_shape((S, D_ATTN), dv), F32), _sds(_view_shape((S, 128), dv), F32)],
        scratch_shapes=[pltpu.VMEM((G, BLK, 128), F32), pltpu.VMEM((G, 2 * BLK, 128), F32), pltpu.VMEM((G, 2 * BLK, 128), F32),
                        pltpu.VMEM((G, BLK, 128), F32), pltpu.VMEM((G, BLK, 128), F32)],
        name=name, compiler_params=_cp(3),
    )(zv, zv, zv, zv, zv)
    return o.reshape(S, D_ATTN), lse.reshape(S, 128)


def _pair_weights(w, lane):
    return [jnp.where(lane < HEAD_DIM, w[:, 2 * hp:2 * hp + 1], w[:, 2 * hp + 1:2 * hp + 2]) for hp in range(N_PAIR)]


def attn_merge(os_, lses, name):
    S = os_[0].shape[0]
    tm = _row_tile(S)
    npat = len(os_)

    def body(*refs):
        o_refs, l_refs = refs[:npat], refs[npat:2 * npat]
        y_ref, yb_ref, lt_ref = refs[2 * npat:]
        lane = lax.broadcasted_iota(jnp.int32, (1, 128), 1)
        ls = [r[...] for r in l_refs]
        mx = functools.reduce(jnp.maximum, ls)
        es = [jnp.exp(v - mx) for v in ls]
        den = functools.reduce(jnp.add, es)
        lt_ref[...] = mx + jnp.log(den)
        ws = [_pair_weights(e / den, lane) for e in es]
        for hp in range(N_PAIR):
            sl = slice(hp * 128, (hp + 1) * 128)
            y = ws[0][hp] * o_refs[0][:, sl]
            for p in range(1, npat):
                y = y + ws[p][hp] * o_refs[p][:, sl]
            y_ref[:, sl] = y
            yb_ref[:, sl] = y.astype(BF16)

    big = pl.BlockSpec((tm, D_ATTN), lambda i: (i, 0))
    st = pl.BlockSpec((tm, 128), lambda i: (i, 0))
    return pl.pallas_call(
        body, grid=(S // tm,), in_specs=[big] * npat + [st] * npat, out_specs=[big, big, st],
        out_shape=[_sds((S, D_ATTN), F32), _sds((S, D_ATTN), BF16), _sds((S, 128), F32)],
        name=name, compiler_params=_cp(1),
    )(*os_, *lses)


def attn_bwd(z, dycat, y, lse_tot, dilation, name):
    S = z.shape[0]
    dv, d = _residue_view(dilation)
    R = BLK * d
    nb = S // (BLK * dilation)
    pp = PAIRS_PER_STEP[d]
    G = d * pp
    W = 128 * pp

    def body(q_ref, kc_ref, kp_ref, vc_ref, vp_ref, dy_ref, y_ref, l_ref, dq_ref, dk_ref, dv_ref,
             qs, ks, vs, dys, ys, ls, dqs, dks, dvs, ck, cv):
        hb, n = pl.program_id(1), pl.program_id(2)

        def store_rows(ref, buf):
            if d < PERM_FROM:
                for r in range(d):
                    for p in range(pp):
                        ref[pl.ds(r, BLK, stride=d), p * 128:(p + 1) * 128] = buf[r * pp + p]
            else:
                for c in range(d):
                    ref[c * 128:(c + 1) * 128, :] = _merge_residues(buf, d, pp, c)

        @pl.when(n < nb)
        def _():
            valid = _band_valid(n)
            first = n == 0
            lane = lax.broadcasted_iota(jnp.int32, (1, 128), 1)
            _split_residues(q_ref, qs, d, pp)
            _split_residues(kp_ref, ks, d, pp)
            _split_residues(kc_ref, ks, d, pp, BLK)
            _split_residues(vp_ref, vs, d, pp)
            _split_residues(vc_ref, vs, d, pp, BLK)
            _split_residues(dy_ref, dys, d, pp, n_terms=2)
            _split_residues(y_ref, ys, d, pp, n_terms=2)
            _split_residues(l_ref, ls, d, 1, n_terms=3)

            def step(j, carry):
                q2, k2, v2, dy2, y2 = qs[j].astype(BF16), ks[j].astype(BF16), vs[j].astype(BF16), dys[j], ys[j]
                stat = ls[j // pp]
                pair = hb * pp + j % pp
                dq2 = jnp.zeros((BLK, 128), F32)
                dk2 = jnp.zeros((2 * BLK, 128), F32)
                dv2 = jnp.zeros((2 * BLK, 128), F32)
                for h in range(2):
                    hm = (lane < HEAD_DIM) if h == 0 else (lane >= HEAD_DIM)
                    qm = jnp.where(hm, q2, jnp.zeros_like(q2))
                    km = jnp.where(hm, k2, jnp.zeros_like(k2))
                    dym = jnp.where(hm, dy2, 0.0)
                    dymb = dym.astype(BF16)
                    s = _dot_nt(qm, k2) * ATTN_SCALE
                    p = jnp.where(valid, jnp.exp(s - _lane_pick(stat, lane, 2 * pair + h)), 0.0)
                    dp = _dot_nt(dymb, v2)
                    delta = jnp.sum(dym * y2, axis=-1, keepdims=True)
                    ds = (p * (dp - delta) * ATTN_SCALE).astype(BF16)
                    dq2 = dq2 + _dot(ds, km)
                    dk2 = dk2 + _dot_tn(ds, qm)
                    dv2 = dv2 + _dot_tn(p.astype(BF16), dymb)
                dqs[j] = dq2
                dks[j] = jnp.where(first, 0.0, ck[j]) + dk2[:BLK]
                dvs[j] = jnp.where(first, 0.0, cv[j]) + dv2[:BLK]
                ck[j] = dk2[BLK:]
                cv[j] = dv2[BLK:]
                return carry

            lax.fori_loop(0, G, step, 0, unroll=min(G, 8))
            store_rows(dq_ref, dqs)
            store_rows(dk_ref, dks)
            store_rows(dv_ref, dvs)

        @pl.when(n == nb)
        def _():
            store_rows(dk_ref, ck)
            store_rows(dv_ref, cv)

    cur = lambda n: jnp.minimum(n, nb - 1)
    prev = lambda n: jnp.clip(n - 1, 0, nb - 1)
    zb = lambda c, p: _view_spec(dv, R, W, lambda hb, n: (prev(n) if p else cur(n)), lambda hb, n: c // pp + hb)
    big_cur = _view_spec(dv, R, W, lambda hb, n: cur(n), lambda hb, n: hb)
    big_lag = _view_spec(dv, R, W, lambda hb, n: jnp.maximum(n - 1, 0), lambda hb, n: hb)
    buf = lambda rows, dt: pltpu.VMEM((G, rows, 128), dt)
    zv = _viewed(z, dv)
    outs = pl.pallas_call(
        body, grid=(dv, N_PAIR // pp, nb + 1),
        in_specs=[zb(ZQ, False), zb(ZK, False), zb(ZK, True), zb(ZV, False), zb(ZV, True),
                  big_cur, big_cur, _view_spec(dv, R, 128, lambda hb, n: cur(n), lambda hb, n: 0)],
        out_specs=[big_cur, big_lag, big_lag],
        out_shape=[_sds(_view_shape((S, D_ATTN), dv), F32)] * 3,
        scratch_shapes=[buf(BLK, F32), buf(2 * BLK, F32), buf(2 * BLK, F32), buf(BLK, F32), buf(BLK, F32),
                        pltpu.VMEM((d, BLK, 128), F32), buf(BLK, F32), buf(BLK, F32), buf(BLK, F32), buf(BLK, F32), buf(BLK, F32)],
        name=name, compiler_params=_cp(3),
    )(zv, zv, zv, zv, zv, _viewed(dycat, dv), _viewed(y, dv), _viewed(lse_tot, dv))
    return tuple(o.reshape(S, D_ATTN) for o in outs)


ZC_GB, ZC_GC, ZC_CI, ZC_PI = 6, 7, 8, 9


def _pool_counts(i, tb):
    pos = lax.broadcasted_iota(jnp.int32, (tb, D_POOL), 0) + i * tb + 1
    grp = lax.broadcasted_iota(jnp.int32, (tb, D_POOL), 1) // POOL_GROUP
    win = jnp.where(grp == 0, POOL_WINDOWS[0], jnp.where(grp == 1, POOL_WINDOWS[1],
                    jnp.where(grp == 2, POOL_WINDOWS[2], POOL_WINDOWS[3])))
    return jnp.minimum(pos, win).astype(F32), grp


def _select_group(stages, grp):
    return jnp.where(grp == 0, stages[0], jnp.where(grp == 1, stages[1], jnp.where(grp == 2, stages[2], stages[3])))


def _causal_window_sums(x2):
    s1 = x2 + pltpu.roll(x2, 1, 0)
    s2 = s1 + pltpu.roll(s1, 2, 0)
    s3 = s2 + pltpu.roll(s2, 4, 0)
    s4 = s3 + pltpu.roll(s3, 8, 0)
    return [s1, s2, s3, s4]


def _anticausal_window_sums(x2):
    n = x2.shape[0]
    s1 = x2 + pltpu.roll(x2, n - 1, 0)
    s2 = s1 + pltpu.roll(s1, n - 2, 0)
    s3 = s2 + pltpu.roll(s2, n - 4, 0)
    s4 = s3 + pltpu.roll(s3, n - 8, 0)
    return [s1, s2, s3, s4]


def _pooled(p_prev, p_cur, i, tb):
    x2 = jnp.concatenate([jnp.where(i > 0, p_prev, 0.0), p_cur], axis=0)
    count, grp = _pool_counts(i, tb)
    win_sum = _select_group([s[tb:] for s in _causal_window_sums(x2)], grp)
    return win_sum / count - p_cur, count, grp


def _pool_mm(v, pw_ref, nt):
    outs = []
    for gi in range(len(POOL_WINDOWS)):
        sl = slice(gi * POOL_GROUP, (gi + 1) * POOL_GROUP)
        outs.append(_dot_nt(v[:, sl], pw_ref[gi]) if nt else _dot(v[:, sl], pw_ref[gi]))
    return jnp.concatenate(outs, axis=1)


def conv_pool_fwd(z, conv_w, pool_w, pool_scale, name):
    S = z.shape[0]
    tb = min(256, S)

    def body(gb_ref, gc_ref, gcp_ref, ci_ref, cip_ref, pi_ref, pip_ref, cw_ref, pw_ref, ps_ref, yc_ref, yp_ref):
        i = pl.program_id(0)
        u2 = jnp.concatenate([jnp.where(i > 0, gcp_ref[...] * cip_ref[...], 0.0), gc_ref[...] * ci_ref[...]], axis=0)
        conv = cw_ref[0:1, :] * pltpu.roll(u2, 2, 0) + cw_ref[1:2, :] * pltpu.roll(u2, 1, 0) + cw_ref[2:3, :] * u2
        yc_ref[...] = (gb_ref[...] * conv[tb:]).astype(BF16)
        pooled, _, _ = _pooled(pip_ref[...], pi_ref[...], i, tb)
        yp_ref[...] = (_pool_mm(pooled.astype(BF16), pw_ref, False) * ps_ref[...]).astype(BF16)

    cur = lambda c: pl.BlockSpec((tb, 512), lambda i: (i, c))
    prev = lambda c: pl.BlockSpec((tb, 512), lambda i: (jnp.maximum(i - 1, 0), c))
    full = lambda a: pl.BlockSpec(a.shape, lambda i: (0,) * a.ndim)
    out = pl.BlockSpec((tb, 512), lambda i: (i, 0))
    return pl.pallas_call(
        body, grid=(S // tb,),
        in_specs=[cur(ZC_GB), cur(ZC_GC), prev(ZC_GC), cur(ZC_CI), prev(ZC_CI), cur(ZC_PI), prev(ZC_PI),
                  full(conv_w), full(pool_w), full(pool_scale)],
        out_specs=[out, out], out_shape=[_sds((S, 512), BF16)] * 2, name=name, compiler_params=_cp(1),
    )(z, z, z, z, z, z, z, conv_w, pool_w, pool_scale)


def conv_pool_bwd(z, dycat, conv_w, pool_w, pool_scale, name):
    S = z.shape[0]
    tb = min(256, S)
    nblk = S // tb
    ng = len(POOL_WINDOWS)

    def body(gb_ref, gbn_ref, gc_ref, gcp_ref, ci_ref, cip_ref, pi_ref, pip_ref, dyc_ref, dycn_ref, dyp_ref, dypn_ref,
             cw_ref, pw_ref, ps_ref, dz_ref, dcw_ref, dpw_ref, dps_ref):
        i = pl.program_id(0)
        last = i == nblk - 1

        @pl.when(i == 0)
        def _():
            dcw_ref[...] = jnp.zeros_like(dcw_ref)
            dpw_ref[...] = jnp.zeros_like(dpw_ref)
            dps_ref[...] = jnp.zeros_like(dps_ref)

        gc, ci = gc_ref[...], ci_ref[...]
        u2 = jnp.concatenate([jnp.where(i > 0, gcp_ref[...] * cip_ref[...], 0.0), gc * ci], axis=0)
        um2, um1, u0 = pltpu.roll(u2, 2, 0)[tb:], pltpu.roll(u2, 1, 0)[tb:], u2[tb:]
        conv = cw_ref[0:1, :] * um2 + cw_ref[1:2, :] * um1 + cw_ref[2:3, :] * u0
        dyc = dyc_ref[...]
        dconv = dyc * gb_ref[...]
        dc2 = jnp.concatenate([dconv, jnp.where(last, 0.0, dycn_ref[...] * gbn_ref[...])], axis=0)
        du = (cw_ref[2:3, :] * dconv + cw_ref[1:2, :] * pltpu.roll(dc2, 2 * tb - 1, 0)[:tb]
              + cw_ref[0:1, :] * pltpu.roll(dc2, 2 * tb - 2, 0)[:tb])
        dz_ref[:, 0:512] = (dyc * conv).astype(BF16)
        dz_ref[:, 512:1024] = (du * ci).astype(BF16)
        dz_ref[:, 1024:1536] = (du * gc).astype(BF16)
        dcw_ref[0:1, :] += jnp.sum(dconv * um2, axis=0, keepdims=True)
        dcw_ref[1:2, :] += jnp.sum(dconv * um1, axis=0, keepdims=True)
        dcw_ref[2:3, :] += jnp.sum(dconv * u0, axis=0, keepdims=True)

        pooled, count, grp = _pooled(pip_ref[...], pi_ref[...], i, tb)
        pooled_b = pooled.astype(BF16)
        t = _pool_mm(pooled_b, pw_ref, False)
        dyp, ps = dyp_ref[...], ps_ref[...]
        dps_ref[...] += jnp.sum(dyp * t, axis=0, keepdims=True)
        dt_b = (dyp * ps).astype(BF16)
        for gi in range(ng):
            sl = slice(gi * POOL_GROUP, (gi + 1) * POOL_GROUP)
            dpw_ref[gi] += _dot_tn(pooled_b[:, sl], dt_b[:, sl])
        dpooled = _pool_mm(dt_b, pw_ref, True)
        dpooled_n = _pool_mm((dypn_ref[...] * ps).astype(BF16), pw_ref, True)
        count_n, _ = _pool_counts(i + 1, tb)
        dq2 = jnp.concatenate([dpooled / count, jnp.where(last, 0.0, dpooled_n / count_n)], axis=0)
        lead = _select_group([s[:tb] for s in _anticausal_window_sums(dq2)], grp)
        dz_ref[:, 1536:2048] = (lead - dpooled).astype(BF16)

    cur = lambda c: pl.BlockSpec((tb, 512), lambda i: (i, c))
    prev = lambda c: pl.BlockSpec((tb, 512), lambda i: (jnp.maximum(i - 1, 0), c))
    nxt = lambda c: pl.BlockSpec((tb, 512), lambda i: (jnp.minimum(i + 1, nblk - 1), c))
    full = lambda a: pl.BlockSpec(a.shape, lambda i: (0,) * a.ndim)
    yc_c, yp_c = D_ATTN // 512, D_ATTN // 512 + 1
    return pl.pallas_call(
        body, grid=(nblk,),
        in_specs=[cur(ZC_GB), nxt(ZC_GB), cur(ZC_GC), prev(ZC_GC), cur(ZC_CI), prev(ZC_CI), cur(ZC_PI), prev(ZC_PI),
                  cur(yc_c), nxt(yc_c), cur(yp_c), nxt(yp_c), full(conv_w), full(pool_w), full(pool_scale)],
        out_specs=[pl.BlockSpec((tb, 2048), lambda i: (i, 0)), pl.BlockSpec((3, 512), lambda i: (0, 0)),
                   pl.BlockSpec((ng, POOL_GROUP, POOL_GROUP), lambda i: (0, 0, 0)), pl.BlockSpec((1, 512), lambda i: (0, 0))],
        out_shape=[_sds((S, 2048), BF16), _sds((3, 512), F32), _sds((ng, POOL_GROUP, POOL_GROUP), F32), _sds((1, 512), F32)],
        name=name, compiler_params=_cp(1),
    )(z, z, z, z, z, z, z, z, dycat, dycat, dycat, dycat, conv_w, pool_w, pool_scale)


def assemble_dz(parts_q, parts_k, parts_v, dcp, name):
    S = dcp.shape[0]
    tm = _row_tile(S)
    npat = len(parts_q)

    def body(*refs):
        dz_ref = refs[-1]
        dcp_ref = refs[-2]
        for j in range(3):
            acc = refs[j * npat][...]
            for p in range(1, npat):
                acc = acc + refs[j * npat + p][...]
            dz_ref[:, j * D_ATTN:(j + 1) * D_ATTN] = acc.astype(BF16)
        dz_ref[:, 3 * D_ATTN:] = dcp_ref[...]

    big = pl.BlockSpec((tm, D_ATTN), lambda i: (i, 0))
    return pl.pallas_call(
        body, grid=(S // tm,), in_specs=[big] * (3 * npat) + [pl.BlockSpec((tm, D_IN - 3 * D_ATTN), lambda i: (i, 0))],
        out_specs=pl.BlockSpec((tm, D_IN), lambda i: (i, 0)), out_shape=_sds((S, D_IN), BF16),
        name=name, compiler_params=_cp(1),
    )(*parts_q, *parts_k, *parts_v, dcp)


def adamw(w, g, m, v, name):
    shape = w.shape
    cols = shape[-1]
    rows = w.size // cols
    tr = rows
    for cand in (256, 128, 64, 32, 16, 8):
        if rows % cand == 0:
            tr = cand
            break
    c1 = 1.0 - ADAM_B1 ** ADAM_STEP
    c2 = 1.0 - ADAM_B2 ** ADAM_STEP

    def body(w_ref, g_ref, m_ref, v_ref, d_ref, mo_ref, vo_ref):
        gv = g_ref[...]
        mn = ADAM_B1 * m_ref[...] + (1.0 - ADAM_B1) * gv
        vn = ADAM_B2 * v_ref[...] + (1.0 - ADAM_B2) * (gv * gv)
        d_ref[...] = -ADAM_LR * ((mn / c1) / (jnp.sqrt(vn / c2) + ADAM_EPS) + ADAM_WD * w_ref[...])
        mo_ref[...] = mn
        vo_ref[...] = vn

    blk = pl.BlockSpec((tr, cols), lambda i: (i, 0))
    outs = pl.pallas_call(
        body, grid=(rows // tr,), in_specs=[blk] * 4, out_specs=[blk] * 3,
        out_shape=[_sds((rows, cols), F32)] * 3, name=name, compiler_params=_cp(1),
    )(*(a.reshape(rows, cols) for a in (w, g, m, v)))
    return tuple(o.reshape(shape) for o in outs)


ANY = pl.BlockSpec(memory_space=pl.ANY)


def _place():
    x, y, c = lax.axis_index("x"), lax.axis_index("y"), lax.axis_index("c")
    chips = [(1 - x, y), (x, 1 - y), (1 - x, 1 - y)]
    return x, y, c, chips


def gather_weights(ws):
    nw = len(ws)
    split = [w.ndim == 2 and w.shape[0] % 32 == 0 for w in ws]

    def body(*refs):
        ins, outs = refs[:nw], refs[nw:2 * nw]
        send, recv = refs[2 * nw:]
        x, y, c, chips = _place()
        me, sib = 2 * x + y, (x, y, 1 - c)

        def half(j, k, hc):
            if not split[j]:
                return outs[j].at[k]
            ha = ws[j].shape[0] // 2
            return outs[j].at[k, pl.ds(hc * ha, ha), :]

        def rcopy(j, s, src, dst, to):
            return pltpu.make_async_remote_copy(src_ref=src, dst_ref=dst, send_sem=send.at[j, s], recv_sem=recv.at[j, s],
                                                device_id=to, device_id_type=MESH)

        first = [rcopy(j, 6, ins[j], outs[j].at[me], sib) for j in range(nw)]
        for j in range(nw):
            ha = ws[j].shape[0] // 2
            mine = ins[j].at[pl.ds(c * ha, ha), :] if split[j] else ins[j]
            for s, (px, py) in enumerate(chips):
                first.append(rcopy(j, s, mine, half(j, me, c), (px, py, c)))
        for cp in first:
            cp.start()
        passed = []
        for j in range(nw):
            for s, (px, py) in enumerate(chips):
                blk = half(j, 2 * px + py, c)
                rcopy(j, s, blk, blk, (px, py, c)).wait_recv()
                if split[j]:
                    fwd = rcopy(j, 3 + s, blk, blk, sib)
                    fwd.start()
                    passed.append(fwd)
        for j in range(nw):
            for s, (px, py) in enumerate(chips):
                if split[j]:
                    blk = half(j, 2 * px + py, 1 - c)
                    rcopy(j, 3 + s, blk, blk, sib).wait_recv()
        for j in range(nw):
            rcopy(j, 6, ins[j], outs[j].at[me], sib).wait_recv()
        for cp in first + passed:
            cp.wait_send()

    return pl.pallas_call(
        body, in_specs=[ANY] * nw, out_specs=[ANY] * nw,
        out_shape=[_sds((NSH,) + w.shape, w.dtype) for w in ws],
        scratch_shapes=[pltpu.SemaphoreType.DMA((nw, 7)), pltpu.SemaphoreType.DMA((nw, 7))],
        name="gather_weights",
    )(*ws)


def pair_sum(g, got, half_index, name):
    _, A, B = g.shape
    ha = A // 2
    tr = ha
    for cand in (512, 256, 128, 64):
        if ha % cand == 0:
            tr = cand
            break
    nt = ha // tr

    def body(c_ref, g_ref, r_ref, o_ref):
        del c_ref
        o_ref[...] = (g_ref[...] + r_ref[...]).astype(BF16)

    return pl.pallas_call(
        body,
        grid_spec=pltpu.PrefetchScalarGridSpec(
            num_scalar_prefetch=1, grid=(NSH, nt),
            in_specs=[pl.BlockSpec((None, tr, B), lambda k, t, c: (k, c[0] * nt + t, 0)),
                      pl.BlockSpec((None, tr, B), lambda k, t, c: (k, t, 0))],
            out_specs=pl.BlockSpec((None, tr, B), lambda k, t, c: (k, t, 0))),
        out_shape=_sds((NSH, ha, B), BF16), name=name, compiler_params=_cp(2),
    )(half_index, g, got)


def _half_tile(ha):
    for cand in (512, 256, 128, 64):
        if ha % cand == 0:
            return cand
    return ha


def chip_sum(partials, slots, chip_index, name):
    depth = len(partials)
    _, ha, B = partials[0].shape
    tr = _half_tile(ha)
    nt = ha // tr

    def body(me_ref, *refs):
        p_refs, s_refs, o_ref = refs[:depth], refs[depth:depth + depth * NSH], refs[depth + depth * NSH]
        l = pl.program_id(0)
        for ll in range(depth):
            @pl.when(l == ll)
            def _(ll=ll):
                own = p_refs[ll][...].astype(F32)
                acc = jnp.where(me_ref[0] == 0, own, s_refs[ll * NSH][...].astype(F32))
                for k in range(1, NSH):
                    acc = acc + jnp.where(me_ref[0] == k, own, s_refs[ll * NSH + k][...].astype(F32))
                o_ref[...] = acc

    def frozen(ll):
        return lambda l, t: jnp.where(l == ll, t, jnp.where(l < ll, 0, nt - 1))

    def slot(k):
        return lambda me: jnp.where(me[0] == k, (k + 1) % NSH, k)

    in_specs = [pl.BlockSpec((None, tr, B), lambda l, t, me, ll=ll: (me[0], frozen(ll)(l, t), 0)) for ll in range(depth)]
    in_specs += [pl.BlockSpec((None, tr, B), lambda l, t, me, ll=ll, k=k: (slot(k)(me), frozen(ll)(l, t), 0))
                 for ll in range(depth) for k in range(NSH)]
    return pl.pallas_call(
        body,
        grid_spec=pltpu.PrefetchScalarGridSpec(
            num_scalar_prefetch=1, grid=(depth, nt), in_specs=in_specs,
            out_specs=pl.BlockSpec((None, tr, B), lambda l, t, me: (l, t, 0))),
        out_shape=_sds((depth, ha, B), F32), name=name, compiler_params=_cp(2),
    )(chip_index, *partials, *[s for s in slots for _ in range(NSH)])


def sibling_swap(mine):
    n = len(mine)

    def body(*refs):
        ins, outs = refs[:n], refs[n:2 * n]
        send, recv = refs[2 * n:]
        x, y, c, _ = _place()
        cps = [pltpu.make_async_remote_copy(src_ref=ins[j], dst_ref=outs[j], send_sem=send.at[j], recv_sem=recv.at[j],
                                            device_id=(x, y, 1 - c), device_id_type=MESH) for j in range(n)]
        for cp in cps:
            cp.start()
        for cp in cps:
            cp.wait()

    return pl.pallas_call(
        body, in_specs=[ANY] * n, out_specs=[ANY] * n, out_shape=[_sds(m.shape, m.dtype) for m in mine],
        scratch_shapes=[pltpu.SemaphoreType.DMA((n,)), pltpu.SemaphoreType.DMA((n,))], name="sibling_swap",
    )(*mine)


def adamw_halves(w, mine, theirs, m, v, core_index, name):
    depth, A, B = w.shape
    ha = A // 2
    tr = _half_tile(ha)
    while tr * B * 4 > 2 ** 20 and tr % 16 == 0:
        tr //= 2
    nt = ha // tr
    c1 = 1.0 - ADAM_B1 ** ADAM_STEP
    c2 = 1.0 - ADAM_B2 ** ADAM_STEP

    def body(c_ref, w_ref, a_ref, b_ref, m_ref, v_ref, g_ref, d_ref, mo_ref, vo_ref):
        gv = jnp.where(pl.program_id(1) == c_ref[0], a_ref[...], b_ref[...])
        mn = ADAM_B1 * m_ref[...] + (1.0 - ADAM_B1) * gv
        vn = ADAM_B2 * v_ref[...] + (1.0 - ADAM_B2) * (gv * gv)
        g_ref[...] = gv
        d_ref[...] = -ADAM_LR * ((mn / c1) / (jnp.sqrt(vn / c2) + ADAM_EPS) + ADAM_WD * w_ref[...])
        mo_ref[...] = mn
        vo_ref[...] = vn

    full = pl.BlockSpec((None, tr, B), lambda l, h, t, c: (l, h * nt + t, 0))
    a_spec = pl.BlockSpec((None, tr, B), lambda l, h, t, c: (l, jnp.where(h == c[0], t, 0), 0))
    b_spec = pl.BlockSpec((None, tr, B), lambda l, h, t, c: (l, jnp.where(h == c[0], 0, t), 0))
    return pl.pallas_call(
        body,
        grid_spec=pltpu.PrefetchScalarGridSpec(
            num_scalar_prefetch=1, grid=(depth, 2, nt), in_specs=[full, a_spec, b_spec, full, full], out_specs=[full] * 4),
        out_shape=[_sds(w.shape, F32)] * 4, name=name, compiler_params=_cp(3),
    )(core_index, w, mine, theirs, m, v)


def small_all_reduce(v):
    R = v.shape[0]

    def body(v_ref, o_ref, slots, send, recv):
        x, y, c, _ = _place()
        me = 4 * x + 2 * y + c
        slots[me] = v_ref[...]
        cps = []
        for m in range(1, 8):
            mx, my, mc = (m >> 2) & 1, (m >> 1) & 1, m & 1
            cps.append(pltpu.make_async_remote_copy(
                src_ref=v_ref, dst_ref=slots.at[me], send_sem=send.at[m - 1], recv_sem=recv.at[m - 1],
                device_id=(x ^ mx, y ^ my, c ^ mc), device_id_type=MESH))
        for cp in cps:
            cp.start()
        for cp in cps:
            cp.wait()
        acc = slots[0]
        for d in range(1, 8):
            acc = acc + slots[d]
        o_ref[...] = acc

    vm = pl.BlockSpec(memory_space=pltpu.VMEM)
    return pl.pallas_call(
        body, in_specs=[vm], out_specs=vm, out_shape=_sds((R, 128), F32),
        scratch_shapes=[pltpu.VMEM((8, R, 128), F32), pltpu.SemaphoreType.DMA((7,)), pltpu.SemaphoreType.DMA((7,))],
        name="small_all_reduce",
    )(v)


BIG = ("ffn1_w_gate", "ffn1_w_up", "ffn1_w_down", "w_in", "w_out", "ffn2_w_gate", "ffn2_w_up", "ffn2_w_down")
SMALL = ("ffn1_norm", "mix_norm", "conv_w", "pool_w", "pool_scale", "ffn2_norm", "final_norm")
WEIGHTS = ("ffn1_norm", "ffn1_w_gate", "ffn1_w_up", "ffn1_w_down", "mix_norm", "w_in", "conv_w", "pool_w", "pool_scale",
           "w_out", "ffn2_norm", "ffn2_w_gate", "ffn2_w_up", "ffn2_w_down", "final_norm")


def _tile_for(n, cap=1024):
    best = 128
    for t in range(128, min(n, cap) + 1, 128):
        if n % t == 0:
            best = t
    return n if n <= cap else best


GATHER_BEHIND = {
    (0, "ffn1_up"): [("w_in", 0), ("w_out", 0), ("ffn2_w_gate", 0), ("ffn2_w_up", 0)],
    (0, "ffn1_down"): [("ffn2_w_down", 0)],
    (0, "mix_in"): [("ffn1_w_gate", 1)],
    (0, "mix_out"): [("ffn1_w_up", 1)],
    (0, "ffn2_up"): [("ffn1_w_down", 1), ("w_in", 1), ("w_out", 1)],
    (0, "ffn2_down"): [("ffn2_w_gate", 1)],
    (1, "ffn1_up"): [("ffn2_w_up", 1), ("ffn2_w_down", 1)],
}


class _GatherPlan:
    def __init__(self, local):
        self.local, self.pending, self.ready = local, [], {}

    def comm(self, firsts):
        cm = _Comm()
        self._passing, self._firsts = list(self.pending), list(firsts)
        for _, buf in self._passing:
            cm.gather_pass(buf)
        for key in self._firsts:
            cm.gather_first(self.local[key])
        return cm

    def done(self, couts):
        npass = len(self._passing)
        for (key, _), buf in zip(self._passing, couts[:npass]):
            self.ready[key] = buf
        self.pending = list(zip(self._firsts, couts[npass:]))


def kernel(x, ffn1_norm, ffn1_w_gate, ffn1_w_up, ffn1_w_down, mix_norm, w_in, conv_w, pool_w, pool_scale, w_out, ffn2_norm, ffn2_w_gate, ffn2_w_up, ffn2_w_down, final_norm, loss_target, m_ffn1_norm, m_ffn1_w_gate, m_ffn1_w_up, m_ffn1_w_down, m_mix_norm, m_w_in, m_conv_w, m_pool_w, m_pool_scale, m_w_out, m_ffn2_norm, m_ffn2_w_gate, m_ffn2_w_up, m_ffn2_w_down, m_final_norm, v_ffn1_norm, v_ffn1_w_gate, v_ffn1_w_up, v_ffn1_w_down, v_mix_norm, v_w_in, v_conv_w, v_pool_w, v_pool_scale, v_w_out, v_ffn2_norm, v_ffn2_w_gate, v_ffn2_w_up, v_ffn2_w_down, v_final_norm):
    given = dict(locals())
    W = {n: given[n] for n in WEIGHTS}
    M = {n: given["m_" + n] for n in WEIGHTS}
    V = {n: given["v_" + n] for n in WEIGHTS}
    depth = ffn1_norm.shape[0]
    D = x.shape[-1]
    xs = x[0]
    my_chip = 2 * lax.axis_index("x") + lax.axis_index("y")
    my_core = lax.axis_index("c")

    half_index = my_core.astype(jnp.int32).reshape(1)
    chip_index = my_chip.astype(jnp.int32).reshape(1)
    plan = _GatherPlan({(n, l): W[n][l].astype(BF16) for n in BIG for l in range(depth)})
    head = [("ffn1_w_gate", 0), ("ffn1_w_up", 0), ("ffn1_w_down", 0)]
    gathered = gather_weights([plan.local[key] for key in head] + [conv_w])
    plan.ready.update(zip(head, gathered[:-1]))
    conv_full = jnp.moveaxis(gathered[-1], 0, 2).reshape(depth, 3, D_CONV)
    pool_b = pool_w.astype(BF16)
    G = plan.ready

    def behind(l, stage):
        return plan.comm([(n, ll) for n, ll in GATHER_BEHIND.get((l, stage), []) if ll < depth])

    saved = []
    cur = xs
    for l in range(depth):
        x0 = cur
        h = rms_fwd(x0, ffn1_norm[l], "ffn1_norm")
        (g, u, a), got = ffn_up(h, G["ffn1_w_gate", l], G["ffn1_w_up", l], "ffn1_up", behind(l, "ffn1_up"))
        plan.done(got)
        cur, got = resid_mm(a, G["ffn1_w_down", l], x0, 0.5, "ffn1_down", behind(l, "ffn1_down"))
        plan.done(got)
        s1 = (x0, h, g, u, a)

        x1 = cur
        hm = rms_fwd(x1, mix_norm[l], "mix_norm")
        z, got = col_mm(hm, G["w_in", l], "mix_in", behind(l, "mix_in"))
        plan.done(got)
        pats = [attn_fwd(z, d, f"attn_fwd_d{d}") for d in DILATIONS]
        y, yb, lse_tot = attn_merge([p[0] for p in pats], [p[1] for p in pats], "attn_merge")
        y_conv, y_pool = conv_pool_fwd(z, conv_full[l], pool_b[l], pool_scale[l].reshape(1, D_POOL), "conv_pool_fwd")
        ycat = jnp.concatenate([yb, y_conv, y_pool], axis=1)
        cur, got = resid_mm(ycat, G["w_out", l], x1, 1.0, "mix_out", behind(l, "mix_out"))
        plan.done(got)
        sm = (x1, hm, z, y, lse_tot, ycat)

        x2 = cur
        h = rms_fwd(x2, ffn2_norm[l], "ffn2_norm")
        (g, u, a), got = ffn_up(h, G["ffn2_w_gate", l], G["ffn2_w_up", l], "ffn2_up", behind(l, "ffn2_up"))
        plan.done(got)
        cur, got = resid_mm(a, G["ffn2_w_down", l], x2, 0.5, "ffn2_down", behind(l, "ffn2_down"))
        plan.done(got)
        saved.append((s1, sm, (x2, h, g, u, a)))
    assert not plan.pending and len(G) == len(BIG) * depth

    loss11, dx, dxb, d_final = final_loss(cur, final_norm, loss_target[0], "final_loss")

    small_grads = {n: [None] * depth for n in SMALL if n != "final_norm"}
    partials = {n: [None] * depth for n in BIG}
    slots = {n: [None] * depth for n in BIG}

    def halves(grad):
        cm = _Comm()
        cm.sibling_half(grad)
        return cm

    carry = []

    def carried_comm():
        cm = _Comm()
        taken = list(carry)
        carry.clear()
        for _, _, p in taken:
            cm.exchange(p)
        return cm, taken

    def settle(taken, got):
        for (n, ll, p), s in zip(taken, got):
            partials[n][ll], slots[n][ll] = p, s

    def finish_sublayer(name, l, grad, dys, ws, x_in, gain, dx, tag, cm, final):
        if not final:
            dx, dxb, dnorm, got, _ = bwd_dh(dys, ws, x_in, gain, dx, tag, cm)
            carry.append((name, l, pair_sum(grad, got[0], half_index, "pair_sum")))
            return dx, dxb, dnorm, got
        last = {}

        def after(got):
            last["p"] = pair_sum(grad, got[0], half_index, "pair_sum")
            cm2 = _Comm()
            cm2.exchange(last["p"])
            return cm2

        dx, dxb, dnorm, got, got2 = bwd_dh(dys, ws, x_in, gain, dx, tag, cm, after)
        partials[name][l], slots[name][l] = last["p"], got2[0]
        return dx, dxb, dnorm, got

    def ffn_backward(sv, dx, dxb, norm, tag, l, final=False):
        x_in, h, g, u, a = sv
        names = [f"{tag}_w_down", f"{tag}_w_gate", f"{tag}_w_up"]
        wd, wg, wu = (G[n, l] for n in names)
        FS = wg.shape[-1]
        cm0, taken = carried_comm()
        (dg, du), got0 = ffn_bwd_act(dxb, wd, g, u, f"{tag}_bwd_act", cm0)
        settle(taken, got0)
        g_d, _ = wgrad(a, dxb, True, False, 0.5, FS, _tile_for(D), f"{tag}_dwd")
        g_g, (got_d,) = wgrad(h, dg, False, True, 1.0, _tile_for(D), FS, f"{tag}_dwg", halves(g_d))
        g_u, (got_g,) = wgrad(h, du, False, True, 1.0, _tile_for(D), FS, f"{tag}_dwu", halves(g_g))
        p_d = pair_sum(g_d, got_d, half_index, "pair_sum")
        p_g = pair_sum(g_g, got_g, half_index, "pair_sum")
        cm = halves(g_u)
        cm.exchange(p_d)
        cm.exchange(p_g)
        dx, dxb, dnorm, got = finish_sublayer(names[2], l, g_u, [dg, du], [wg, wu], x_in, norm[l], dx, f"{tag}_bwd_dh", cm, final)
        settle([(names[0], l, p_d), (names[1], l, p_g)], got[1:])
        return dx, dxb, dnorm

    for l in reversed(range(depth)):
        s1, sm, s2 = saved[l]
        dx, dxb, small_grads["ffn2_norm"][l] = ffn_backward(s2, dx, dxb, ffn2_norm, "ffn2", l)

        x1, hm, z, y, lse_tot, ycat = sm
        cm0, taken = carried_comm()
        dycat, got0 = nt_col_mm(dxb, G["w_out", l], "mix_out_bwd", cm0)
        settle(taken, got0)
        g_wout, _ = wgrad(ycat, dxb, True, False, 1.0, D_MIX // NSH, _tile_for(D, 2048), "mix_dwout")
        parts = [attn_bwd(z, dycat, y, lse_tot, d, f"attn_bwd_d{d}") for d in DILATIONS]
        dcp, dcw, dpw, dps = conv_pool_bwd(z, dycat, conv_full[l], pool_b[l], pool_scale[l].reshape(1, D_POOL), "conv_pool_bwd")
        dz = assemble_dz([p[0] for p in parts], [p[1] for p in parts], [p[2] for p in parts], dcp, "assemble_dz")
        g_win, (got_out,) = wgrad(hm, dz, False, True, 1.0, _tile_for(D), D_IN // NSH, "mix_dwin", halves(g_wout))
        p_out = pair_sum(g_wout, got_out, half_index, "pair_sum")
        cm = halves(g_win)
        cm.exchange(p_out)
        dx, dxb, dnm, got = finish_sublayer("w_in", l, g_win, [dz], [G["w_in", l]], x1, mix_norm[l], dx, "mix_bwd_dh", cm, False)
        settle([("w_out", l, p_out)], got[1:])
        small_grads["mix_norm"][l], small_grads["conv_w"][l] = dnm, dcw
        small_grads["pool_w"][l], small_grads["pool_scale"][l] = dpw, dps

        dx, dxb, small_grads["ffn1_norm"][l] = ffn_backward(s1, dx, dxb, ffn1_norm, "ffn1", l, final=(l == 0))
    assert not carry

    mine = [chip_sum(partials[n], slots[n], chip_index, "chip_sum") for n in BIG]
    theirs = sibling_swap(mine)
    grads, delta, new_m, new_v = {}, {}, {}, {}
    for n, a, b in zip(BIG, mine, theirs):
        grads[n], delta[n], new_m[n], new_v[n] = adamw_halves(W[n], a, b, M[n], V[n], half_index, "adamw")

    small_full = {n: jnp.stack([a.reshape(W[n].shape[1:] if n != "conv_w" else (3, D_CONV)) for a in small_grads[n]])
                  for n in small_grads}
    small_full["final_norm"] = d_final.reshape(D)
    order = list(SMALL)
    packed = jnp.concatenate([small_full[n].reshape(-1) for n in order])
    pad = (-packed.shape[0]) % (8 * 128)
    packed = jnp.pad(packed, (0, pad)).reshape(-1, 128)
    summed = small_all_reduce(packed).reshape(-1)
    off = 0
    for n in order:
        size = small_full[n].size
        grads[n] = summed[off:off + size].reshape(small_full[n].shape)
        off += size
    grads["conv_w"] = lax.dynamic_slice_in_dim(grads["conv_w"], my_chip * (D_CONV // NSH), D_CONV // NSH, axis=2)

    def pack(src):
        flat_ = jnp.concatenate([src[n].reshape(-1) for n in order])
        return jnp.pad(flat_, (0, (-flat_.shape[0]) % (8 * 128))).reshape(-1, 128)

    ds, ms, vs = adamw(pack(W), pack(grads), pack(M), pack(V), "adamw_small")
    off = 0
    for n in order:
        size = W[n].size
        for dst, src in ((delta, ds), (new_m, ms), (new_v, vs)):
            dst[n] = src.reshape(-1)[off:off + size].reshape(W[n].shape)
        off += size

    loss = lax.psum(loss11[0, 0], ("x", "y", "c"))
    return (loss, dx.reshape(x.shape), *[grads[n] for n in WEIGHTS], *[delta[n] for n in WEIGHTS],
            *[new_m[n] for n in WEIGHTS], *[new_v[n] for n in WEIGHTS])
```

```python
import functools

import jax
import jax.numpy as jnp
from jax import lax
from jax.experimental import pallas as pl
from jax.experimental.pallas import tpu as pltpu

F32 = jnp.float32
BF16 = jnp.bfloat16
MESH = pl.DeviceIdType.MESH

RMS_EPS = 1e-6
NEG_INF = -1e30
HEAD_DIM = 64
BLK = 128
SPAN = 128
DILATIONS = (1, 4, 16)
D_ATTN = 1024
D_CONV = 512
D_POOL = 512
POOL_WINDOWS = (2, 4, 8, 16)
POOL_GROUP = 128
D_IN = 3 * D_ATTN + 3 * D_CONV + D_POOL
D_MIX = D_ATTN + D_CONV + D_POOL
N_PAIR = D_ATTN // 128
ATTN_SCALE = HEAD_DIM ** -0.5
NSH = 4
ADAM_LR, ADAM_B1, ADAM_B2, ADAM_EPS, ADAM_WD, ADAM_STEP = 0.001, 0.9, 0.999, 1e-08, 0.01, 10

VMEM_LIMIT = 56 * 2 ** 20


def _cp(n_axes):
    return pltpu.CompilerParams(dimension_semantics=("arbitrary",) * n_axes, vmem_limit_bytes=VMEM_LIMIT)


def _sds(shape, dtype):
    return jax.ShapeDtypeStruct(shape, dtype)


def _dot(a, b):
    return jnp.dot(a, b, preferred_element_type=F32)


def _dot_nt(a, b):
    return lax.dot_general(a, b, (((1,), (1,)), ((), ())), preferred_element_type=F32)


def _dot_tn(a, b):
    return lax.dot_general(a, b, (((0,), (0,)), ((), ())), preferred_element_type=F32)


def _row_tile(s):
    return min(512, s)


def _rms_bwd_tile(xv, gv, dh):
    r = lax.rsqrt(jnp.mean(xv * xv, axis=-1, keepdims=True) + RMS_EPS)
    xhat = xv * r
    dg = jnp.sum(dh * xhat, axis=0, keepdims=True)
    dxhat = dh * gv
    dx = r * (dxhat - xhat * jnp.mean(dxhat * xhat, axis=-1, keepdims=True))
    return dx, dg


def final_loss(x, g, target, name):
    S, D = x.shape
    tm = _row_tile(S)

    def body(x_ref, g_ref, t_ref, loss_ref, dx_ref, dxb_ref, dg_ref):
        i = pl.program_id(0)
        xv, gv = x_ref[...], g_ref[...]
        r = lax.rsqrt(jnp.mean(xv * xv, axis=-1, keepdims=True) + RMS_EPS)
        err = xv * r * gv - t_ref[...]
        part = 0.5 * jnp.sum(jnp.mean(err * err, axis=-1, keepdims=True), axis=0, keepdims=True)
        dx, dg = _rms_bwd_tile(xv, gv, err * (1.0 / D))

        @pl.when(i == 0)
        def _():
            loss_ref[...] = jnp.zeros_like(loss_ref)
            dg_ref[...] = jnp.zeros_like(dg_ref)

        loss_ref[...] += part
        dg_ref[...] += dg
        dx_ref[...] = dx
        dxb_ref[...] = dx.astype(BF16)

    row = pl.BlockSpec((tm, D), lambda i: (i, 0))
    vec = pl.BlockSpec((1, D), lambda i: (0, 0))
    return pl.pallas_call(
        body, grid=(S // tm,), in_specs=[row, vec, row],
        out_specs=[pl.BlockSpec((1, 1), lambda i: (0, 0)), row, row, vec],
        out_shape=[_sds((1, 1), F32), _sds((S, D), F32), _sds((S, D), BF16), _sds((1, D), F32)],
        name=name, compiler_params=_cp(1),
    )(x, g.reshape(1, D), target)


def _wspec(w, imap):
    _, a, b = w.shape
    return pl.BlockSpec((None, a, b), lambda *ids: (imap(*ids), 0, 0))


class _Comm:
    def __init__(self):
        self.ins, self.out_shapes, self.aliases, self.items = [], [], {}, []

    def _add(self, kind, operand, out_shape, alias):
        if alias:
            self.aliases[len(self.ins)] = len(self.out_shapes)
        self.items.append((kind, len(self.ins), len(self.out_shapes)))
        self.ins.append(operand)
        self.out_shapes.append(out_shape)

    def gather_first(self, src):
        self._add("first", src, _sds((NSH,) + src.shape, src.dtype), False)

    def gather_pass(self, buf):
        self._add("pass", buf, _sds(buf.shape, buf.dtype), True)

    def sibling_half(self, grad):
        self._add("half", grad, _sds((NSH, grad.shape[1] // 2, grad.shape[2]), grad.dtype), False)

    def exchange(self, partial):
        self._add("xchg", partial, _sds(partial.shape, partial.dtype), False)

    def run(self, cins, couts, send, recv, start):
        x, y, c = lax.axis_index("x"), lax.axis_index("y"), lax.axis_index("c")
        chips = [(1 - x, y), (x, 1 - y), (1 - x, 1 - y)]
        me, sib = 2 * x + y, (x, y, 1 - c)
        for it, (kind, i, o) in enumerate(self.items):
            def rc(s, src, dst, to, it=it):
                return pltpu.make_async_remote_copy(src_ref=src, dst_ref=dst, send_sem=send.at[it, s], recv_sem=recv.at[it, s],
                                                    device_id=to, device_id_type=MESH)
            src, buf = cins[i], couts[o]
            if kind == "first":
                ha = src.shape[0] // 2
                rows = pl.ds(c * ha, ha)
                cps = [rc(s, src.at[rows], buf.at[me, rows], (px, py, c)) for s, (px, py) in enumerate(chips)]
                cps.append(rc(3, src, buf.at[me], sib))
                landing = [buf.at[2 * px + py, rows] for px, py in chips] + [buf.at[me]]
            elif kind == "pass":
                ha = buf.shape[1] // 2
                rows, other = pl.ds(c * ha, ha), pl.ds((1 - c) * ha, ha)
                cps = [rc(s, buf.at[2 * px + py, rows], buf.at[2 * px + py, rows], sib) for s, (px, py) in enumerate(chips)]
                landing = [buf.at[2 * px + py, other] for px, py in chips]
            elif kind == "half":
                ha = src.shape[1] // 2
                cps = [rc(0, src.at[:, pl.ds((1 - c) * ha, ha), :], buf, sib)]
                landing = [buf]
            else:
                cps = [rc(s, src.at[2 * px + py], buf.at[me], (px, py, c)) for s, (px, py) in enumerate(chips)]
                landing = [buf.at[2 * px + py] for px, py in chips]
            if start:
                for cp in cps:
                    cp.start()
            else:
                for s, dst in enumerate(landing):
                    rc(s, dst, dst, sib).wait_recv()
                for cp in cps:
                    cp.wait_send()


def _call(body, *, grid, in_specs, out_specs, out_shape, name, args, scratch=(), comm=None):
    multi = isinstance(out_shape, (list, tuple))
    oshape = list(out_shape) if multi else [out_shape]
    ospecs = list(out_specs) if multi else [out_specs]
    if comm is None or not comm.items:
        res = pl.pallas_call(body, grid=grid, in_specs=list(in_specs), out_specs=ospecs, out_shape=oshape,
                             scratch_shapes=list(scratch), name=name, compiler_params=_cp(len(grid)))(*args)
        return (list(res) if multi else res[0]), []
    nin, nout, nci, nco, nscr = len(in_specs), len(oshape), len(comm.ins), len(comm.out_shapes), len(scratch)

    def full(*refs):
        ins, cins = refs[:nin], refs[nin:nin + nci]
        outs, couts = refs[nin + nci:nin + nci + nout], refs[nin + nci + nout:nin + nci + nout + nco]
        scr = refs[nin + nci + nout + nco:nin + nci + nout + nco + nscr]
        send, recv = refs[-2:]
        ids = [pl.program_id(a) for a in range(len(grid))]
        first = functools.reduce(jnp.logical_and, [i == 0 for i in ids])
        last = functools.reduce(jnp.logical_and, [i == g - 1 for i, g in zip(ids, grid)])

        @pl.when(first)
        def _():
            comm.run(cins, couts, send, recv, True)

        body(*ins, *outs, *scr)

        @pl.when(last)
        def _():
            comm.run(cins, couts, send, recv, False)

    sems = pltpu.SemaphoreType.DMA((len(comm.items), 4))
    res = pl.pallas_call(
        full, grid=grid, in_specs=list(in_specs) + [ANY] * nci, out_specs=ospecs + [ANY] * nco,
        out_shape=oshape + comm.out_shapes, scratch_shapes=list(scratch) + [sems, sems],
        input_output_aliases={nin + i: nout + o for i, o in comm.aliases.items()},
        name=name, compiler_params=_cp(len(grid)),
    )(*args, *comm.ins)
    main = list(res[:nout])
    return (main if multi else main[0]), list(res[nout:])


def rms_fwd(x, g, name):
    S, D = x.shape
    tm = _row_tile(S)

    def body(x_ref, g_ref, h_ref):
        xv = x_ref[...]
        r = lax.rsqrt(jnp.mean(xv * xv, axis=-1, keepdims=True) + RMS_EPS)
        h_ref[...] = (xv * r * g_ref[...]).astype(BF16)

    return pl.pallas_call(
        body, grid=(S // tm,),
        in_specs=[pl.BlockSpec((tm, D), lambda i: (i, 0)), pl.BlockSpec((1, D), lambda i: (0, 0))],
        out_specs=pl.BlockSpec((tm, D), lambda i: (i, 0)),
        out_shape=_sds((S, D), BF16), name=name, compiler_params=_cp(1),
    )(x, g.reshape(1, D))


def ffn_up(h, wg, wu, name, comm=None):
    S, D = h.shape
    FS = wg.shape[-1]
    tm = _row_tile(S)

    def body(h_ref, wg_ref, wu_ref, sg_ref, up_ref, a_ref):
        hv = h_ref[...]
        g = _dot(hv, wg_ref[...])
        u = _dot(hv, wu_ref[...])
        s = jax.nn.sigmoid(g)
        sg = g * s
        sg_ref[...] = sg.astype(BF16)
        up_ref[...] = (u * s * (1.0 + g * (1.0 - s))).astype(BF16)
        a_ref[...] = (sg * u).astype(BF16)

    out = pl.BlockSpec((tm, FS), lambda k, i: (i, k))
    shard = lambda k, i: k
    return _call(
        body, grid=(NSH, S // tm),
        in_specs=[pl.BlockSpec((tm, D), lambda k, i: (i, 0)), _wspec(wg, shard), _wspec(wu, shard)],
        out_specs=[out] * 3, out_shape=[_sds((S, NSH * FS), BF16)] * 3,
        name=name, args=(h, wg, wu), comm=comm)


def col_mm(h, w, name, comm=None):
    S, D = h.shape
    NS = w.shape[-1]
    tm = _row_tile(S)

    def body(h_ref, w_ref, z_ref):
        z_ref[...] = _dot(h_ref[...], w_ref[...])

    return _call(
        body, grid=(NSH, S // tm),
        in_specs=[pl.BlockSpec((tm, D), lambda k, i: (i, 0)), _wspec(w, lambda k, i: k)],
        out_specs=pl.BlockSpec((tm, NS), lambda k, i: (i, k)), out_shape=_sds((S, NSH * NS), F32),
        name=name, args=(h, w), comm=comm)


def resid_mm(a, w, x, scale, name, comm=None):
    S, K = a.shape
    D = w.shape[-1]
    tm = _row_tile(S)
    tn = D // 2 if D % 256 == 0 else D

    def body(a_ref, w_ref, x_ref, o_ref):
        o_ref[...] = x_ref[...] + scale * _dot(a_ref[...], w_ref[...])

    out = pl.BlockSpec((tm, tn), lambda j, i: (i, j))
    return _call(
        body, grid=(D // tn, S // tm),
        in_specs=[pl.BlockSpec((tm, K), lambda j, i: (i, 0)), pl.BlockSpec((K, tn), lambda j, i: (0, j)), out],
        out_specs=out, out_shape=_sds((S, D), F32), name=name, args=(a, w.reshape(K, D), x), comm=comm)


def ffn_bwd_act(dxb, wd, sg, up, name, comm=None):
    S, D = dxb.shape
    FS = wd.shape[-2]
    tm = _row_tile(S)

    def body(dx_ref, w_ref, sg_ref, up_ref, dg_ref, du_ref):
        da = 0.5 * _dot_nt(dx_ref[...], w_ref[...])
        du_ref[...] = (da * sg_ref[...].astype(F32)).astype(BF16)
        dg_ref[...] = (da * up_ref[...].astype(F32)).astype(BF16)

    act = pl.BlockSpec((tm, FS), lambda k, i: (i, k))
    return _call(
        body, grid=(NSH, S // tm),
        in_specs=[pl.BlockSpec((tm, D), lambda k, i: (i, 0)), _wspec(wd, lambda k, i: k), act, act],
        out_specs=[act, act], out_shape=[_sds((S, NSH * FS), BF16)] * 2,
        name=name, args=(dxb, wd, sg, up), comm=comm)


def nt_col_mm(dxb, w, name, comm=None):
    S, D = dxb.shape
    KS = w.shape[-2]
    tm = _row_tile(S)

    def body(dx_ref, w_ref, o_ref):
        o_ref[...] = _dot_nt(dx_ref[...], w_ref[...])

    return _call(
        body, grid=(NSH, S // tm),
        in_specs=[pl.BlockSpec((tm, D), lambda k, i: (i, 0)), _wspec(w, lambda k, i: k)],
        out_specs=pl.BlockSpec((tm, KS), lambda k, i: (i, k)), out_shape=_sds((S, NSH * KS), F32),
        name=name, args=(dxb, w), comm=comm)


def wgrad(lhs, rhs, lhs_sharded, rhs_sharded, scale, tr, tc, name, comm=None):
    S = lhs.shape[0]
    R = lhs.shape[1] // (NSH if lhs_sharded else 1)
    C = rhs.shape[1] // (NSH if rhs_sharded else 1)
    ts = min(2048, S)
    nr, nc = R // tr, C // tc

    def body(l_ref, r_ref, o_ref):
        @pl.when(pl.program_id(3) == 0)
        def _():
            o_ref[...] = jnp.zeros_like(o_ref)

        o_ref[...] += scale * _dot_tn(l_ref[...], r_ref[...])

    lmap = (lambda k, a, b, s: (s, k * nr + a)) if lhs_sharded else (lambda k, a, b, s: (s, a))
    rmap = (lambda k, a, b, s: (s, k * nc + b)) if rhs_sharded else (lambda k, a, b, s: (s, b))
    return _call(
        body, grid=(NSH, nr, nc, S // ts),
        in_specs=[pl.BlockSpec((ts, tr), lmap), pl.BlockSpec((ts, tc), rmap)],
        out_specs=pl.BlockSpec((None, tr, tc), lambda k, a, b, s: (k, a, b)),
        out_shape=_sds((NSH, R, C), F32), name=name, args=(lhs, rhs), comm=comm)


def bwd_dh(dys, ws, x, g, dxin, name, comm=None, after=None):
    S, D = x.shape
    NS = ws[0].shape[-1]
    tm = _row_tile(S)
    nj = len(dys)

    def mm_body(*refs):
        dy_refs, w_refs, dh_ref = refs[:nj], refs[nj:2 * nj], refs[2 * nj]

        @pl.when(pl.program_id(1) == 0)
        def _():
            dh_ref[...] = jnp.zeros_like(dh_ref)

        for dy_ref, w_ref in zip(dy_refs, w_refs):
            dh_ref[...] += _dot_nt(dy_ref[...], w_ref[...])

    dh, comm_out = _call(
        mm_body, grid=(S // tm, NSH),
        in_specs=[pl.BlockSpec((tm, NS), lambda i, k: (i, k))] * nj + [_wspec(w, lambda i, k: k) for w in ws],
        out_specs=pl.BlockSpec((tm, D), lambda i, k: (i, 0)), out_shape=_sds((S, D), F32),
        name=name + "_mm", args=(*dys, *ws), comm=comm)

    def norm_body(dh_ref, x_ref, g_ref, dxin_ref, dx_ref, dxb_ref, dg_ref):
        dx, dg = _rms_bwd_tile(x_ref[...], g_ref[...], dh_ref[...])
        tot = dxin_ref[...] + dx
        dx_ref[...] = tot
        dxb_ref[...] = tot.astype(BF16)

        @pl.when(pl.program_id(0) == 0)
        def _():
            dg_ref[...] = jnp.zeros_like(dg_ref)

        dg_ref[...] += dg

    row = pl.BlockSpec((tm, D), lambda i: (i, 0))
    vec = pl.BlockSpec((1, D), lambda i: (0, 0))
    comm2 = after(comm_out) if after is not None else None
    (dx, dxb, dgain), comm2_out = _call(
        norm_body, grid=(S // tm,), in_specs=[row, row, vec, row], out_specs=[row, row, vec],
        out_shape=[_sds((S, D), F32), _sds((S, D), BF16), _sds((1, D), F32)],
        name=name + "_norm", args=(dh, x, g.reshape(1, D), dxin), comm=comm2)
    return dx, dxb, dgain, comm_out, comm2_out


PAIRS_PER_STEP = {1: 8, 4: 1, 16: 1}
ZQ, ZK, ZV = 0, D_ATTN // 128, 2 * D_ATTN // 128


def _band_valid(n):
    qi = lax.broadcasted_iota(jnp.int32, (BLK, 2 * BLK), 0)
    kj = lax.broadcasted_iota(jnp.int32, (BLK, 2 * BLK), 1)
    dist = qi + BLK - kj
    return (dist >= 0) & (dist <= SPAN) & ((kj >= BLK) | (n > 0))


PERM_FROM = 8


def _residue_perm(d, transpose):
    q = 128 // d
    a = lax.broadcasted_iota(jnp.int32, (128, 128), 1 if transpose else 0)
    b = lax.broadcasted_iota(jnp.int32, (128, 128), 0 if transpose else 1)
    return (b == (a % q) * d + a // q).astype(BF16)


def _perm_apply(perm, x, n_terms):
    out, rest = None, x
    for t in range(n_terms):
        term = rest.astype(BF16)
        out = _dot(perm, term) if out is None else out + _dot(perm, term)
        if t + 1 < n_terms:
            rest = rest - term.astype(F32)
    return out


def _split_residues(src_ref, dst, d, pp, row0=0, n_terms=1):
    if d < PERM_FROM:
        for r in range(d):
            for p in range(pp):
                dst[r * pp + p, row0:row0 + BLK, :] = src_ref[pl.ds(r, BLK, stride=d), p * 128:(p + 1) * 128]
        return
    perm, q = _residue_perm(d, False), 128 // d
    for c in range(d):
        t = _perm_apply(perm, src_ref[c * 128:(c + 1) * 128, :], n_terms)
        for r in range(d):
            dst[r, row0 + c * q:row0 + (c + 1) * q, :] = t[r * q:(r + 1) * q]


def _shift_keys(ks, vs, n):
    @pl.when(n == 0)
    def _():
        ks[:, :BLK, :] = jnp.zeros((ks.shape[0], BLK, 128), ks.dtype)
        vs[:, :BLK, :] = jnp.zeros((vs.shape[0], BLK, 128), vs.dtype)

    @pl.when(n > 0)
    def _():
        ks[:, :BLK, :] = ks[:, BLK:, :]
        vs[:, :BLK, :] = vs[:, BLK:, :]


def _merge_residues(bufs, d, pp, c, n_terms=2):
    q = 128 // d
    t = jnp.concatenate([bufs[r, c * q:(c + 1) * q, :] for r in range(d)], axis=0)
    return _perm_apply(_residue_perm(d, True), t, n_terms)


def _lane_pick(stat, lane, idx):
    return jnp.sum(jnp.where(lane == idx, stat, 0.0), axis=-1, keepdims=True)


def _residue_view(dilation):
    return 1, dilation


def _view_shape(shape, dv):
    return (shape[0] // dv, dv, shape[1]) if dv > 1 else tuple(shape)


def _viewed(a, dv):
    return a.reshape(_view_shape(a.shape, dv))


def _view_spec(dv, rows, width, row_of, col_of):
    if dv > 1:
        return pl.BlockSpec((rows, None, width), lambda r, a, b: (row_of(a, b), r, col_of(a, b)))
    return pl.BlockSpec((rows, width), lambda r, a, b: (row_of(a, b), col_of(a, b)))


def attn_fwd(z, dilation, name):
    S = z.shape[0]
    dv, d = _residue_view(dilation)
    R = BLK * d
    nb = S // (BLK * dilation)
    pp = PAIRS_PER_STEP[d]
    G = d * pp
    W = 128 * pp

    def body(q_ref, kc_ref, vc_ref, o_ref, lse_ref, qs, ks, vs, os_, ls):
        hb, n = pl.program_id(1), pl.program_id(2)
        valid = _band_valid(n)
        lane = lax.broadcasted_iota(jnp.int32, (1, 128), 1)
        _shift_keys(ks, vs, n)
        _split_residues(q_ref, qs, d, pp)
        _split_residues(kc_ref, ks, d, pp, BLK)
        _split_residues(vc_ref, vs, d, pp, BLK)

        def step(j, carry):
            q2, k2, v2 = qs[j].astype(BF16), ks[j].astype(BF16), vs[j].astype(BF16)
            pair = hb * pp + j % pp
            o2 = jnp.zeros((BLK, 128), F32)
            stat = jnp.zeros((BLK, 128), F32)
            for h in range(2):
                hm = (lane < HEAD_DIM) if h == 0 else (lane >= HEAD_DIM)
                qm = jnp.where(hm, q2, jnp.zeros_like(q2))
                vm = jnp.where(hm, v2, jnp.zeros_like(v2))
                s = jnp.where(valid, _dot_nt(qm, k2) * ATTN_SCALE, NEG_INF)
                m = jnp.max(s, axis=-1, keepdims=True)
                p = jnp.exp(s - m)
                lsum = jnp.sum(p, axis=-1, keepdims=True)
                o2 = o2 + _dot(p.astype(BF16), vm) / lsum
                stat = jnp.where(lane == 2 * pair + h, m + jnp.log(lsum), stat)
            os_[j] = o2
            ls[j] = stat
            return carry

        lax.fori_loop(0, G, step, 0, unroll=min(G, 8))

        if d < PERM_FROM:
            for r in range(d):
                rows = pl.ds(r, BLK, stride=d)
                acc = ls[r * pp]
                for p in range(pp):
                    o_ref[rows, p * 128:(p + 1) * 128] = os_[r * pp + p]
                    if p:
                        acc = acc + ls[r * pp + p]
                lse_ref[rows, :] = acc
        else:
            for c in range(d):
                rows = slice(c * 128, (c + 1) * 128)
                o_ref[rows, :] = _merge_residues(os_, d, pp, c)
                lse_ref[rows, :] = _merge_residues(ls, d, pp, c, n_terms=3)

    nhb = N_PAIR // pp
    cur = lambda c: _view_spec(dv, R, W, lambda hb, n: n, lambda hb, n: c // pp + hb)
    zv = _viewed(z, dv)
    o, lse = pl.pallas_call(
        body, grid=(dv, nhb, nb),
        in_specs=[cur(ZQ), cur(ZK), cur(ZV)],
        out_specs=[_view_spec(dv, R, W, lambda hb, n: n, lambda hb, n: hb), _view_spec(dv, R, 128, lambda hb, n: n, lambda hb, n: hb)],
        out_shape=[_sds(_view_shape((S, D_ATTN), dv), F32), _sds(_view_shape((S, nhb * 128), dv), F32)],
        scratch_shapes=[pltpu.VMEM((G, BLK, 128), F32), pltpu.VMEM((G, 2 * BLK, 128), F32), pltpu.VMEM((G, 2 * BLK, 128), F32),
                        pltpu.VMEM((G, BLK, 128), F32), pltpu.VMEM((G, BLK, 128), F32)],
        name=name, compiler_params=_cp(3),
    )(zv, zv, zv)
    return o.reshape(S, D_ATTN), lse.reshape(S, nhb * 128)


def _pair_weights(w, lane):
    return [jnp.where(lane < HEAD_DIM, w[:, 2 * hp:2 * hp + 1], w[:, 2 * hp + 1:2 * hp + 2]) for hp in range(N_PAIR)]


def attn_merge(os_, lses, name):
    S = os_[0].shape[0]
    tm = _row_tile(S)
    npat = len(os_)

    def body(*refs):
        o_refs, l_refs = refs[:npat], refs[npat:2 * npat]
        y_ref, yb_ref, lt_ref = refs[2 * npat:]
        lane = lax.broadcasted_iota(jnp.int32, (1, 128), 1)
        ls = [functools.reduce(jnp.add, [r[:, b * 128:(b + 1) * 128] for b in range(r.shape[1] // 128)]) for r in l_refs]
        mx = functools.reduce(jnp.maximum, ls)
        es = [jnp.exp(v - mx) for v in ls]
        den = functools.reduce(jnp.add, es)
        lt_ref[...] = mx + jnp.log(den)
        ws = [_pair_weights(e / den, lane) for e in es]
        for hp in range(N_PAIR):
            sl = slice(hp * 128, (hp + 1) * 128)
            y = ws[0][hp] * o_refs[0][:, sl]
            for p in range(1, npat):
                y = y + ws[p][hp] * o_refs[p][:, sl]
            y_ref[:, sl] = y
            yb_ref[:, sl] = y.astype(BF16)

    big = pl.BlockSpec((tm, D_ATTN), lambda i: (i, 0))
    st = pl.BlockSpec((tm, 128), lambda i: (i, 0))
    return pl.pallas_call(
        body, grid=(S // tm,), in_specs=[big] * npat + [pl.BlockSpec((tm, l.shape[1]), lambda i: (i, 0)) for l in lses],
        out_specs=[big, big, st],
        out_shape=[_sds((S, D_ATTN), F32), _sds((S, D_ATTN), BF16), _sds((S, 128), F32)],
        name=name, compiler_params=_cp(1),
    )(*os_, *lses)


def attn_bwd(z, dycat, y, lse_tot, dilation, name):
    S = z.shape[0]
    dv, d = _residue_view(dilation)
    R = BLK * d
    nb = S // (BLK * dilation)
    pp = PAIRS_PER_STEP[d]
    G = d * pp
    W = 128 * pp

    def body(q_ref, kc_ref, vc_ref, dy_ref, y_ref, l_ref, dq_ref, dk_ref, dv_ref,
             qs, ks, vs, dys, ys, ls, dqs, dks, dvs, ck, cv):
        hb, n = pl.program_id(1), pl.program_id(2)

        def store_rows(ref, buf):
            if d < PERM_FROM:
                for r in range(d):
                    for p in range(pp):
                        ref[pl.ds(r, BLK, stride=d), p * 128:(p + 1) * 128] = buf[r * pp + p]
            else:
                for c in range(d):
                    ref[c * 128:(c + 1) * 128, :] = _merge_residues(buf, d, pp, c)

        @pl.when(n < nb)
        def _():
            valid = _band_valid(n)
            first = n == 0
            lane = lax.broadcasted_iota(jnp.int32, (1, 128), 1)
            _shift_keys(ks, vs, n)
            _split_residues(q_ref, qs, d, pp)
            _split_residues(kc_ref, ks, d, pp, BLK)
            _split_residues(vc_ref, vs, d, pp, BLK)
            _split_residues(dy_ref, dys, d, pp, n_terms=2)
            _split_residues(y_ref, ys, d, pp, n_terms=2)
            _split_residues(l_ref, ls, d, 1, n_terms=3)

            def step(j, carry):
                q2, k2, v2, dy2, y2 = qs[j].astype(BF16), ks[j].astype(BF16), vs[j].astype(BF16), dys[j], ys[j]
                stat = ls[j // pp]
                pair = hb * pp + j % pp
                dq2 = jnp.zeros((BLK, 128), F32)
                dk2 = jnp.zeros((2 * BLK, 128), F32)
                dv2 = jnp.zeros((2 * BLK, 128), F32)
                for h in range(2):
                    hm = (lane < HEAD_DIM) if h == 0 else (lane >= HEAD_DIM)
                    qm = jnp.where(hm, q2, jnp.zeros_like(q2))
                    km = jnp.where(hm, k2, jnp.zeros_like(k2))
                    dym = jnp.where(hm, dy2, 0.0)
                    dymb = dym.astype(BF16)
                    s = _dot_nt(qm, k2) * ATTN_SCALE
                    p = jnp.where(valid, jnp.exp(s - _lane_pick(stat, lane, 2 * pair + h)), 0.0)
                    dp = _dot_nt(dymb, v2)
                    delta = jnp.sum(dym * y2, axis=-1, keepdims=True)
                    ds = (p * (dp - delta) * ATTN_SCALE).astype(BF16)
                    dq2 = dq2 + _dot(ds, km)
                    dk2 = dk2 + _dot_tn(ds, qm)
                    dv2 = dv2 + _dot_tn(p.astype(BF16), dymb)
                dqs[j] = dq2
                dks[j] = jnp.where(first, 0.0, ck[j]) + dk2[:BLK]
                dvs[j] = jnp.where(first, 0.0, cv[j]) + dv2[:BLK]
                ck[j] = dk2[BLK:]
                cv[j] = dv2[BLK:]
                return carry

            lax.fori_loop(0, G, step, 0, unroll=min(G, 8))
            store_rows(dq_ref, dqs)
            store_rows(dk_ref, dks)
            store_rows(dv_ref, dvs)

        @pl.when(n == nb)
        def _():
            store_rows(dk_ref, ck)
            store_rows(dv_ref, cv)

    cur = lambda n: jnp.minimum(n, nb - 1)
    zb = lambda c: _view_spec(dv, R, W, lambda hb, n: cur(n), lambda hb, n: c // pp + hb)
    big_cur = _view_spec(dv, R, W, lambda hb, n: cur(n), lambda hb, n: hb)
    big_lag = _view_spec(dv, R, W, lambda hb, n: jnp.maximum(n - 1, 0), lambda hb, n: hb)
    buf = lambda rows, dt: pltpu.VMEM((G, rows, 128), dt)
    zv = _viewed(z, dv)
    outs = pl.pallas_call(
        body, grid=(dv, N_PAIR // pp, nb + 1),
        in_specs=[zb(ZQ), zb(ZK), zb(ZV), big_cur, big_cur, _view_spec(dv, R, 128, lambda hb, n: cur(n), lambda hb, n: 0)],
        out_specs=[big_cur, big_lag, big_lag],
        out_shape=[_sds(_view_shape((S, D_ATTN), dv), F32)] * 3,
        scratch_shapes=[buf(BLK, F32), buf(2 * BLK, F32), buf(2 * BLK, F32), buf(BLK, F32), buf(BLK, F32),
                        pltpu.VMEM((d, BLK, 128), F32), buf(BLK, F32), buf(BLK, F32), buf(BLK, F32), buf(BLK, F32), buf(BLK, F32)],
        name=name, compiler_params=_cp(3),
    )(zv, zv, zv, _viewed(dycat, dv), _viewed(y, dv), _viewed(lse_tot, dv))
    return tuple(o.reshape(S, D_ATTN) for o in outs)


ZC_GB, ZC_GC, ZC_CI, ZC_PI = 6, 7, 8, 9


def _pool_counts(i, tb):
    pos = lax.broadcasted_iota(jnp.int32, (tb, D_POOL), 0) + i * tb + 1
    grp = lax.broadcasted_iota(jnp.int32, (tb, D_POOL), 1) // POOL_GROUP
    win = jnp.where(grp == 0, POOL_WINDOWS[0], jnp.where(grp == 1, POOL_WINDOWS[1],
                    jnp.where(grp == 2, POOL_WINDOWS[2], POOL_WINDOWS[3])))
    return jnp.minimum(pos, win).astype(F32), grp


def _select_group(stages, grp):
    return jnp.where(grp == 0, stages[0], jnp.where(grp == 1, stages[1], jnp.where(grp == 2, stages[2], stages[3])))


def _causal_window_sums(x2):
    s1 = x2 + pltpu.roll(x2, 1, 0)
    s2 = s1 + pltpu.roll(s1, 2, 0)
    s3 = s2 + pltpu.roll(s2, 4, 0)
    s4 = s3 + pltpu.roll(s3, 8, 0)
    return [s1, s2, s3, s4]


def _anticausal_window_sums(x2):
    n = x2.shape[0]
    s1 = x2 + pltpu.roll(x2, n - 1, 0)
    s2 = s1 + pltpu.roll(s1, n - 2, 0)
    s3 = s2 + pltpu.roll(s2, n - 4, 0)
    s4 = s3 + pltpu.roll(s3, n - 8, 0)
    return [s1, s2, s3, s4]


def _pooled(p_prev, p_cur, i, tb):
    x2 = jnp.concatenate([jnp.where(i > 0, p_prev, 0.0), p_cur], axis=0)
    count, grp = _pool_counts(i, tb)
    win_sum = _select_group([s[tb:] for s in _causal_window_sums(x2)], grp)
    return win_sum / count - p_cur, count, grp


def _pool_mm(v, pw_ref, nt):
    outs = []
    for gi in range(len(POOL_WINDOWS)):
        sl = slice(gi * POOL_GROUP, (gi + 1) * POOL_GROUP)
        outs.append(_dot_nt(v[:, sl], pw_ref[gi]) if nt else _dot(v[:, sl], pw_ref[gi]))
    return jnp.concatenate(outs, axis=1)


def conv_pool_fwd(z, conv_w, pool_w, pool_scale, name):
    S = z.shape[0]
    tb = min(256, S)

    def body(gb_ref, gc_ref, gcp_ref, ci_ref, cip_ref, pi_ref, pip_ref, cw_ref, pw_ref, ps_ref, yc_ref, yp_ref):
        i = pl.program_id(0)
        u2 = jnp.concatenate([jnp.where(i > 0, gcp_ref[...] * cip_ref[...], 0.0), gc_ref[...] * ci_ref[...]], axis=0)
        conv = cw_ref[0:1, :] * pltpu.roll(u2, 2, 0) + cw_ref[1:2, :] * pltpu.roll(u2, 1, 0) + cw_ref[2:3, :] * u2
        yc_ref[...] = (gb_ref[...] * conv[tb:]).astype(BF16)
        pooled, _, _ = _pooled(pip_ref[...], pi_ref[...], i, tb)
        yp_ref[...] = (_pool_mm(pooled.astype(BF16), pw_ref, False) * ps_ref[...]).astype(BF16)

    cur = lambda c: pl.BlockSpec((tb, 512), lambda i: (i, c))
    prev = lambda c: pl.BlockSpec((tb, 512), lambda i: (jnp.maximum(i - 1, 0), c))
    full = lambda a: pl.BlockSpec(a.shape, lambda i: (0,) * a.ndim)
    out = pl.BlockSpec((tb, 512), lambda i: (i, 0))
    return pl.pallas_call(
        body, grid=(S // tb,),
        in_specs=[cur(ZC_GB), cur(ZC_GC), prev(ZC_GC), cur(ZC_CI), prev(ZC_CI), cur(ZC_PI), prev(ZC_PI),
                  full(conv_w), full(pool_w), full(pool_scale)],
        out_specs=[out, out], out_shape=[_sds((S, 512), BF16)] * 2, name=name, compiler_params=_cp(1),
    )(z, z, z, z, z, z, z, conv_w, pool_w, pool_scale)


def conv_pool_bwd(z, dycat, conv_w, pool_w, pool_scale, name):
    S = z.shape[0]
    tb = min(256, S)
    nblk = S // tb
    ng = len(POOL_WINDOWS)

    def body(gb_ref, gbn_ref, gc_ref, gcp_ref, ci_ref, cip_ref, pi_ref, pip_ref, dyc_ref, dycn_ref, dyp_ref, dypn_ref,
             cw_ref, pw_ref, ps_ref, dz_ref, dcw_ref, dpw_ref, dps_ref):
        i = pl.program_id(0)
        last = i == nblk - 1

        @pl.when(i == 0)
        def _():
            dcw_ref[...] = jnp.zeros_like(dcw_ref)
            dpw_ref[...] = jnp.zeros_like(dpw_ref)
            dps_ref[...] = jnp.zeros_like(dps_ref)

        gc, ci = gc_ref[...], ci_ref[...]
        u2 = jnp.concatenate([jnp.where(i > 0, gcp_ref[...] * cip_ref[...], 0.0), gc * ci], axis=0)
        um2, um1, u0 = pltpu.roll(u2, 2, 0)[tb:], pltpu.roll(u2, 1, 0)[tb:], u2[tb:]
        conv = cw_ref[0:1, :] * um2 + cw_ref[1:2, :] * um1 + cw_ref[2:3, :] * u0
        dyc = dyc_ref[...]
        dconv = dyc * gb_ref[...]
        dc2 = jnp.concatenate([dconv, jnp.where(last, 0.0, dycn_ref[...] * gbn_ref[...])], axis=0)
        du = (cw_ref[2:3, :] * dconv + cw_ref[1:2, :] * pltpu.roll(dc2, 2 * tb - 1, 0)[:tb]
              + cw_ref[0:1, :] * pltpu.roll(dc2, 2 * tb - 2, 0)[:tb])
        dz_ref[:, 0:512] = (dyc * conv).astype(BF16)
        dz_ref[:, 512:1024] = (du * ci).astype(BF16)
        dz_ref[:, 1024:1536] = (du * gc).astype(BF16)
        dcw_ref[0:1, :] += jnp.sum(dconv * um2, axis=0, keepdims=True)
        dcw_ref[1:2, :] += jnp.sum(dconv * um1, axis=0, keepdims=True)
        dcw_ref[2:3, :] += jnp.sum(dconv * u0, axis=0, keepdims=True)

        pooled, count, grp = _pooled(pip_ref[...], pi_ref[...], i, tb)
        pooled_b = pooled.astype(BF16)
        t = _pool_mm(pooled_b, pw_ref, False)
        dyp, ps = dyp_ref[...], ps_ref[...]
        dps_ref[...] += jnp.sum(dyp * t, axis=0, keepdims=True)
        dt_b = (dyp * ps).astype(BF16)
        for gi in range(ng):
            sl = slice(gi * POOL_GROUP, (gi + 1) * POOL_GROUP)
            dpw_ref[gi] += _dot_tn(pooled_b[:, sl], dt_b[:, sl])
        dpooled = _pool_mm(dt_b, pw_ref, True)
        dpooled_n = _pool_mm((dypn_ref[...] * ps).astype(BF16), pw_ref, True)
        count_n, _ = _pool_counts(i + 1, tb)
        dq2 = jnp.concatenate([dpooled / count, jnp.where(last, 0.0, dpooled_n / count_n)], axis=0)
        lead = _select_group([s[:tb] for s in _anticausal_window_sums(dq2)], grp)
        dz_ref[:, 1536:2048] = (lead - dpooled).astype(BF16)

    cur = lambda c: pl.BlockSpec((tb, 512), lambda i: (i, c))
    prev = lambda c: pl.BlockSpec((tb, 512), lambda i: (jnp.maximum(i - 1, 0), c))
    nxt = lambda c: pl.BlockSpec((tb, 512), lambda i: (jnp.minimum(i + 1, nblk - 1), c))
    full = lambda a: pl.BlockSpec(a.shape, lambda i: (0,) * a.ndim)
    yc_c, yp_c = D_ATTN // 512, D_ATTN // 512 + 1
    return pl.pallas_call(
        body, grid=(nblk,),
        in_specs=[cur(ZC_GB), nxt(ZC_GB), cur(ZC_GC), prev(ZC_GC), cur(ZC_CI), prev(ZC_CI), cur(ZC_PI), prev(ZC_PI),
                  cur(yc_c), nxt(yc_c), cur(yp_c), nxt(yp_c), full(conv_w), full(pool_w), full(pool_scale)],
        out_specs=[pl.BlockSpec((tb, 2048), lambda i: (i, 0)), pl.BlockSpec((3, 512), lambda i: (0, 0)),
                   pl.BlockSpec((ng, POOL_GROUP, POOL_GROUP), lambda i: (0, 0, 0)), pl.BlockSpec((1, 512), lambda i: (0, 0))],
        out_shape=[_sds((S, 2048), BF16), _sds((3, 512), F32), _sds((ng, POOL_GROUP, POOL_GROUP), F32), _sds((1, 512), F32)],
        name=name, compiler_params=_cp(1),
    )(z, z, z, z, z, z, z, z, dycat, dycat, dycat, dycat, conv_w, pool_w, pool_scale)


def assemble_dz(parts_q, parts_k, parts_v, dcp, name):
    S = dcp.shape[0]
    tm = _row_tile(S)
    npat = len(parts_q)

    def body(*refs):
        dz_ref = refs[-1]
        dcp_ref = refs[-2]
        for j in range(3):
            acc = refs[j * npat][...]
            for p in range(1, npat):
                acc = acc + refs[j * npat + p][...]
            dz_ref[:, j * D_ATTN:(j + 1) * D_ATTN] = acc.astype(BF16)
        dz_ref[:, 3 * D_ATTN:] = dcp_ref[...]

    big = pl.BlockSpec((tm, D_ATTN), lambda i: (i, 0))
    return pl.pallas_call(
        body, grid=(S // tm,), in_specs=[big] * (3 * npat) + [pl.BlockSpec((tm, D_IN - 3 * D_ATTN), lambda i: (i, 0))],
        out_specs=pl.BlockSpec((tm, D_IN), lambda i: (i, 0)), out_shape=_sds((S, D_IN), BF16),
        name=name, compiler_params=_cp(1),
    )(*parts_q, *parts_k, *parts_v, dcp)


def adamw(w, g, m, v, name):
    shape = w.shape
    cols = shape[-1]
    rows = w.size // cols
    tr = rows
    for cand in (256, 128, 64, 32, 16, 8):
        if rows % cand == 0:
            tr = cand
            break
    c1 = 1.0 - ADAM_B1 ** ADAM_STEP
    c2 = 1.0 - ADAM_B2 ** ADAM_STEP

    def body(w_ref, g_ref, m_ref, v_ref, d_ref, mo_ref, vo_ref):
        gv = g_ref[...]
        mn = ADAM_B1 * m_ref[...] + (1.0 - ADAM_B1) * gv
        vn = ADAM_B2 * v_ref[...] + (1.0 - ADAM_B2) * (gv * gv)
        d_ref[...] = -ADAM_LR * ((mn / c1) / (jnp.sqrt(vn / c2) + ADAM_EPS) + ADAM_WD * w_ref[...])
        mo_ref[...] = mn
        vo_ref[...] = vn

    blk = pl.BlockSpec((tr, cols), lambda i: (i, 0))
    outs = pl.pallas_call(
        body, grid=(rows // tr,), in_specs=[blk] * 4, out_specs=[blk] * 3,
        out_shape=[_sds((rows, cols), F32)] * 3, name=name, compiler_params=_cp(1),
    )(*(a.reshape(rows, cols) for a in (w, g, m, v)))
    return tuple(o.reshape(shape) for o in outs)


ANY = pl.BlockSpec(memory_space=pl.ANY)


def _place():
    x, y, c = lax.axis_index("x"), lax.axis_index("y"), lax.axis_index("c")
    chips = [(1 - x, y), (x, 1 - y), (1 - x, 1 - y)]
    return x, y, c, chips


def gather_weights(ws):
    nw = len(ws)
    split = [w.ndim == 2 and w.shape[0] % 32 == 0 for w in ws]

    def body(*refs):
        ins, outs = refs[:nw], refs[nw:2 * nw]
        send, recv = refs[2 * nw:]
        x, y, c, chips = _place()
        me, sib = 2 * x + y, (x, y, 1 - c)

        def half(j, k, hc):
            if not split[j]:
                return outs[j].at[k]
            ha = ws[j].shape[0] // 2
            return outs[j].at[k, pl.ds(hc * ha, ha), :]

        def rcopy(j, s, src, dst, to):
            return pltpu.make_async_remote_copy(src_ref=src, dst_ref=dst, send_sem=send.at[j, s], recv_sem=recv.at[j, s],
                                                device_id=to, device_id_type=MESH)

        first = [rcopy(j, 6, ins[j], outs[j].at[me], sib) for j in range(nw)]
        for j in range(nw):
            ha = ws[j].shape[0] // 2
            mine = ins[j].at[pl.ds(c * ha, ha), :] if split[j] else ins[j]
            for s, (px, py) in enumerate(chips):
                first.append(rcopy(j, s, mine, half(j, me, c), (px, py, c)))
        for cp in first:
            cp.start()
        passed = []
        for j in range(nw):
            for s, (px, py) in enumerate(chips):
                blk = half(j, 2 * px + py, c)
                rcopy(j, s, blk, blk, (px, py, c)).wait_recv()
                if split[j]:
                    fwd = rcopy(j, 3 + s, blk, blk, sib)
                    fwd.start()
                    passed.append(fwd)
        for j in range(nw):
            for s, (px, py) in enumerate(chips):
                if split[j]:
                    blk = half(j, 2 * px + py, 1 - c)
                    rcopy(j, 3 + s, blk, blk, sib).wait_recv()
        for j in range(nw):
            rcopy(j, 6, ins[j], outs[j].at[me], sib).wait_recv()
        for cp in first + passed:
            cp.wait_send()

    return pl.pallas_call(
        body, in_specs=[ANY] * nw, out_specs=[ANY] * nw,
        out_shape=[_sds((NSH,) + w.shape, w.dtype) for w in ws],
        scratch_shapes=[pltpu.SemaphoreType.DMA((nw, 7)), pltpu.SemaphoreType.DMA((nw, 7))],
        name="gather_weights",
    )(*ws)


def pair_sum(g, got, half_index, name):
    _, A, B = g.shape
    ha = A // 2
    tr = ha
    for cand in (512, 256, 128, 64):
        if ha % cand == 0:
            tr = cand
            break
    nt = ha // tr

    def body(c_ref, g_ref, r_ref, o_ref):
        del c_ref
        o_ref[...] = (g_ref[...] + r_ref[...]).astype(BF16)

    return pl.pallas_call(
        body,
        grid_spec=pltpu.PrefetchScalarGridSpec(
            num_scalar_prefetch=1, grid=(NSH, nt),
            in_specs=[pl.BlockSpec((None, tr, B), lambda k, t, c: (k, c[0] * nt + t, 0)),
                      pl.BlockSpec((None, tr, B), lambda k, t, c: (k, t, 0))],
            out_specs=pl.BlockSpec((None, tr, B), lambda k, t, c: (k, t, 0))),
        out_shape=_sds((NSH, ha, B), BF16), name=name, compiler_params=_cp(2),
    )(half_index, g, got)


def _half_tile(ha):
    for cand in (512, 256, 128, 64):
        if ha % cand == 0:
            return cand
    return ha


def chip_sum(partials, slots, chip_index, name):
    depth = len(partials)
    _, ha, B = partials[0].shape
    tr = _half_tile(ha)
    nt = ha // tr

    def body(me_ref, *refs):
        p_refs, s_refs, o_ref = refs[:depth], refs[depth:depth + depth * NSH], refs[depth + depth * NSH]
        l = pl.program_id(0)
        for ll in range(depth):
            @pl.when(l == ll)
            def _(ll=ll):
                own = p_refs[ll][...].astype(F32)
                acc = jnp.where(me_ref[0] == 0, own, s_refs[ll * NSH][...].astype(F32))
                for k in range(1, NSH):
                    acc = acc + jnp.where(me_ref[0] == k, own, s_refs[ll * NSH + k][...].astype(F32))
                o_ref[...] = acc

    def frozen(ll):
        return lambda l, t: jnp.where(l == ll, t, jnp.where(l < ll, 0, nt - 1))

    def slot(k):
        return lambda me: jnp.where(me[0] == k, (k + 1) % NSH, k)

    in_specs = [pl.BlockSpec((None, tr, B), lambda l, t, me, ll=ll: (me[0], frozen(ll)(l, t), 0)) for ll in range(depth)]
    in_specs += [pl.BlockSpec((None, tr, B), lambda l, t, me, ll=ll, k=k: (slot(k)(me), frozen(ll)(l, t), 0))
                 for ll in range(depth) for k in range(NSH)]
    return pl.pallas_call(
        body,
        grid_spec=pltpu.PrefetchScalarGridSpec(
            num_scalar_prefetch=1, grid=(depth, nt), in_specs=in_specs,
            out_specs=pl.BlockSpec((None, tr, B), lambda l, t, me: (l, t, 0))),
        out_shape=_sds((depth, ha, B), F32), name=name, compiler_params=_cp(2),
    )(chip_index, *partials, *[s for s in slots for _ in range(NSH)])


def sibling_swap(mine):
    n = len(mine)

    def body(*refs):
        ins, outs = refs[:n], refs[n:2 * n]
        send, recv = refs[2 * n:]
        x, y, c, _ = _place()
        cps = [pltpu.make_async_remote_copy(src_ref=ins[j], dst_ref=outs[j], send_sem=send.at[j], recv_sem=recv.at[j],
                                            device_id=(x, y, 1 - c), device_id_type=MESH) for j in range(n)]
        for cp in cps:
            cp.start()
        for cp in cps:
            cp.wait()

    return pl.pallas_call(
        body, in_specs=[ANY] * n, out_specs=[ANY] * n, out_shape=[_sds(m.shape, m.dtype) for m in mine],
        scratch_shapes=[pltpu.SemaphoreType.DMA((n,)), pltpu.SemaphoreType.DMA((n,))], name="sibling_swap",
    )(*mine)


def adamw_halves(w, mine, theirs, m, v, core_index, name):
    depth, A, B = w.shape
    ha = A // 2
    tr = _half_tile(ha)
    while tr * B * 4 > 2 ** 20 and tr % 16 == 0:
        tr //= 2
    nt = ha // tr
    c1 = 1.0 - ADAM_B1 ** ADAM_STEP
    c2 = 1.0 - ADAM_B2 ** ADAM_STEP

    def body(c_ref, w_ref, a_ref, b_ref, m_ref, v_ref, g_ref, d_ref, mo_ref, vo_ref):
        gv = jnp.where(pl.program_id(1) == c_ref[0], a_ref[...], b_ref[...])
        mn = ADAM_B1 * m_ref[...] + (1.0 - ADAM_B1) * gv
        vn = ADAM_B2 * v_ref[...] + (1.0 - ADAM_B2) * (gv * gv)
        g_ref[...] = gv
        d_ref[...] = -ADAM_LR * ((mn / c1) / (jnp.sqrt(vn / c2) + ADAM_EPS) + ADAM_WD * w_ref[...])
        mo_ref[...] = mn
        vo_ref[...] = vn

    full = pl.BlockSpec((None, tr, B), lambda l, h, t, c: (l, h * nt + t, 0))
    a_spec = pl.BlockSpec((None, tr, B), lambda l, h, t, c: (l, jnp.where(h == c[0], t, 0), 0))
    b_spec = pl.BlockSpec((None, tr, B), lambda l, h, t, c: (l, jnp.where(h == c[0], 0, t), 0))
    return pl.pallas_call(
        body,
        grid_spec=pltpu.PrefetchScalarGridSpec(
            num_scalar_prefetch=1, grid=(depth, 2, nt), in_specs=[full, a_spec, b_spec, full, full], out_specs=[full] * 4),
        out_shape=[_sds(w.shape, F32)] * 4, name=name, compiler_params=_cp(3),
    )(core_index, w, mine, theirs, m, v)


def small_all_reduce(v):
    R = v.shape[0]

    def body(v_ref, o_ref, slots, send, recv):
        x, y, c, _ = _place()
        me = 4 * x + 2 * y + c
        slots[me] = v_ref[...]
        cps = []
        for m in range(1, 8):
            mx, my, mc = (m >> 2) & 1, (m >> 1) & 1, m & 1
            cps.append(pltpu.make_async_remote_copy(
                src_ref=v_ref, dst_ref=slots.at[me], send_sem=send.at[m - 1], recv_sem=recv.at[m - 1],
                device_id=(x ^ mx, y ^ my, c ^ mc), device_id_type=MESH))
        for cp in cps:
            cp.start()
        for cp in cps:
            cp.wait()
        acc = slots[0]
        for d in range(1, 8):
            acc = acc + slots[d]
        o_ref[...] = acc

    vm = pl.BlockSpec(memory_space=pltpu.VMEM)
    return pl.pallas_call(
        body, in_specs=[vm], out_specs=vm, out_shape=_sds((R, 128), F32),
        scratch_shapes=[pltpu.VMEM((8, R, 128), F32), pltpu.SemaphoreType.DMA((7,)), pltpu.SemaphoreType.DMA((7,))],
        name="small_all_reduce",
    )(v)


BIG = ("ffn1_w_gate", "ffn1_w_up", "ffn1_w_down", "w_in", "w_out", "ffn2_w_gate", "ffn2_w_up", "ffn2_w_down")
SMALL = ("ffn1_norm", "mix_norm", "conv_w", "pool_w", "pool_scale", "ffn2_norm", "final_norm")
WEIGHTS = ("ffn1_norm", "ffn1_w_gate", "ffn1_w_up", "ffn1_w_down", "mix_norm", "w_in", "conv_w", "pool_w", "pool_scale",
           "w_out", "ffn2_norm", "ffn2_w_gate", "ffn2_w_up", "ffn2_w_down", "final_norm")


def _tile_for(n, cap=1024):
    best = 128
    for t in range(128, min(n, cap) + 1, 128):
        if n % t == 0:
            best = t
    return n if n <= cap else best


GATHER_BEHIND = {
    (0, "ffn1_up"): [("w_in", 0), ("w_out", 0), ("ffn2_w_gate", 0), ("ffn2_w_up", 0)],
    (0, "ffn1_down"): [("ffn2_w_down", 0)],
    (0, "mix_in"): [("ffn1_w_gate", 1)],
    (0, "mix_out"): [("ffn1_w_up", 1)],
    (0, "ffn2_up"): [("ffn1_w_down", 1), ("w_in", 1), ("w_out", 1)],
    (0, "ffn2_down"): [("ffn2_w_gate", 1)],
    (1, "ffn1_up"): [("ffn2_w_up", 1), ("ffn2_w_down", 1)],
}


class _GatherPlan:
    def __init__(self, local):
        self.local, self.pending, self.ready = local, [], {}

    def comm(self, firsts):
        cm = _Comm()
        self._passing, self._firsts = list(self.pending), list(firsts)
        for _, buf in self._passing:
            cm.gather_pass(buf)
        for key in self._firsts:
            cm.gather_first(self.local[key])
        return cm

    def done(self, couts):
        npass = len(self._passing)
        for (key, _), buf in zip(self._passing, couts[:npass]):
            self.ready[key] = buf
        self.pending = list(zip(self._firsts, couts[npass:]))


def kernel(x, ffn1_norm, ffn1_w_gate, ffn1_w_up, ffn1_w_down, mix_norm, w_in, conv_w, pool_w, pool_scale, w_out, ffn2_norm, ffn2_w_gate, ffn2_w_up, ffn2_w_down, final_norm, loss_target, m_ffn1_norm, m_ffn1_w_gate, m_ffn1_w_up, m_ffn1_w_down, m_mix_norm, m_w_in, m_conv_w, m_pool_w, m_pool_scale, m_w_out, m_ffn2_norm, m_ffn2_w_gate, m_ffn2_w_up, m_ffn2_w_down, m_final_norm, v_ffn1_norm, v_ffn1_w_gate, v_ffn1_w_up, v_ffn1_w_down, v_mix_norm, v_w_in, v_conv_w, v_pool_w, v_pool_scale, v_w_out, v_ffn2_norm, v_ffn2_w_gate, v_ffn2_w_up, v_ffn2_w_down, v_final_norm):
    given = dict(locals())
    W = {n: given[n] for n in WEIGHTS}
    M = {n: given["m_" + n] for n in WEIGHTS}
    V = {n: given["v_" + n] for n in WEIGHTS}
    depth = ffn1_norm.shape[0]
    D = x.shape[-1]
    xs = x[0]
    my_chip = 2 * lax.axis_index("x") + lax.axis_index("y")
    my_core = lax.axis_index("c")

    half_index = my_core.astype(jnp.int32).reshape(1)
    chip_index = my_chip.astype(jnp.int32).reshape(1)
    plan = _GatherPlan({(n, l): W[n][l].astype(BF16) for n in BIG for l in range(depth)})
    head = [("ffn1_w_gate", 0), ("ffn1_w_up", 0), ("ffn1_w_down", 0)]
    gathered = gather_weights([plan.local[key] for key in head] + [conv_w])
    plan.ready.update(zip(head, gathered[:-1]))
    conv_full = jnp.moveaxis(gathered[-1], 0, 2).reshape(depth, 3, D_CONV)
    pool_b = pool_w.astype(BF16)
    G = plan.ready

    def behind(l, stage):
        return plan.comm([(n, ll) for n, ll in GATHER_BEHIND.get((l, stage), []) if ll < depth])

    saved = []
    cur = xs
    for l in range(depth):
        x0 = cur
        h = rms_fwd(x0, ffn1_norm[l], "ffn1_norm")
        (g, u, a), got = ffn_up(h, G["ffn1_w_gate", l], G["ffn1_w_up", l], "ffn1_up", behind(l, "ffn1_up"))
        plan.done(got)
        cur, got = resid_mm(a, G["ffn1_w_down", l], x0, 0.5, "ffn1_down", behind(l, "ffn1_down"))
        plan.done(got)
        s1 = (x0, h, g, u, a)

        x1 = cur
        hm = rms_fwd(x1, mix_norm[l], "mix_norm")
        z, got = col_mm(hm, G["w_in", l], "mix_in", behind(l, "mix_in"))
        plan.done(got)
        pats = [attn_fwd(z, d, f"attn_fwd_d{d}") for d in DILATIONS]
        y, yb, lse_tot = attn_merge([p[0] for p in pats], [p[1] for p in pats], "attn_merge")
        y_conv, y_pool = conv_pool_fwd(z, conv_full[l], pool_b[l], pool_scale[l].reshape(1, D_POOL), "conv_pool_fwd")
        ycat = jnp.concatenate([yb, y_conv, y_pool], axis=1)
        cur, got = resid_mm(ycat, G["w_out", l], x1, 1.0, "mix_out", behind(l, "mix_out"))
        plan.done(got)
        sm = (x1, hm, z, y, lse_tot, ycat)

        x2 = cur
        h = rms_fwd(x2, ffn2_norm[l], "ffn2_norm")
        (g, u, a), got = ffn_up(h, G["ffn2_w_gate", l], G["ffn2_w_up", l], "ffn2_up", behind(l, "ffn2_up"))
        plan.done(got)
        cur, got = resid_mm(a, G["ffn2_w_down", l], x2, 0.5, "ffn2_down", behind(l, "ffn2_down"))
        plan.done(got)
        saved.append((s1, sm, (x2, h, g, u, a)))
    assert not plan.pending and len(G) == len(BIG) * depth

    loss11, dx, dxb, d_final = final_loss(cur, final_norm, loss_target[0], "final_loss")

    small_grads = {n: [None] * depth for n in SMALL if n != "final_norm"}
    partials = {n: [None] * depth for n in BIG}
    slots = {n: [None] * depth for n in BIG}

    def halves(grad):
        cm = _Comm()
        cm.sibling_half(grad)
        return cm

    carry = []

    def carried_comm():
        cm = _Comm()
        taken = list(carry)
        carry.clear()
        for _, _, p in taken:
            cm.exchange(p)
        return cm, taken

    def settle(taken, got):
        for (n, ll, p), s in zip(taken, got):
            partials[n][ll], slots[n][ll] = p, s

    def finish_sublayer(name, l, grad, dys, ws, x_in, gain, dx, tag, cm, final):
        if not final:
            dx, dxb, dnorm, got, _ = bwd_dh(dys, ws, x_in, gain, dx, tag, cm)
            carry.append((name, l, pair_sum(grad, got[0], half_index, "pair_sum")))
            return dx, dxb, dnorm, got
        last = {}

        def after(got):
            last["p"] = pair_sum(grad, got[0], half_index, "pair_sum")
            cm2 = _Comm()
            cm2.exchange(last["p"])
            return cm2

        dx, dxb, dnorm, got, got2 = bwd_dh(dys, ws, x_in, gain, dx, tag, cm, after)
        partials[name][l], slots[name][l] = last["p"], got2[0]
        return dx, dxb, dnorm, got

    def ffn_backward(sv, dx, dxb, norm, tag, l, final=False):
        x_in, h, g, u, a = sv
        names = [f"{tag}_w_down", f"{tag}_w_gate", f"{tag}_w_up"]
        wd, wg, wu = (G[n, l] for n in names)
        FS = wg.shape[-1]
        cm0, taken = carried_comm()
        (dg, du), got0 = ffn_bwd_act(dxb, wd, g, u, f"{tag}_bwd_act", cm0)
        settle(taken, got0)
        g_d, _ = wgrad(a, dxb, True, False, 0.5, FS, _tile_for(D), f"{tag}_dwd")
        g_g, (got_d,) = wgrad(h, dg, False, True, 1.0, _tile_for(D), FS, f"{tag}_dwg", halves(g_d))
        g_u, (got_g,) = wgrad(h, du, False, True, 1.0, _tile_for(D), FS, f"{tag}_dwu", halves(g_g))
        p_d = pair_sum(g_d, got_d, half_index, "pair_sum")
        p_g = pair_sum(g_g, got_g, half_index, "pair_sum")
        cm = halves(g_u)
        cm.exchange(p_d)
        cm.exchange(p_g)
        dx, dxb, dnorm, got = finish_sublayer(names[2], l, g_u, [dg, du], [wg, wu], x_in, norm[l], dx, f"{tag}_bwd_dh", cm, final)
        settle([(names[0], l, p_d), (names[1], l, p_g)], got[1:])
        return dx, dxb, dnorm

    for l in reversed(range(depth)):
        s1, sm, s2 = saved[l]
        dx, dxb, small_grads["ffn2_norm"][l] = ffn_backward(s2, dx, dxb, ffn2_norm, "ffn2", l)

        x1, hm, z, y, lse_tot, ycat = sm
        cm0, taken = carried_comm()
        dycat, got0 = nt_col_mm(dxb, G["w_out", l], "mix_out_bwd", cm0)
        settle(taken, got0)
        g_wout, _ = wgrad(ycat, dxb, True, False, 1.0, D_MIX // NSH, _tile_for(D, 2048), "mix_dwout")
        parts = [attn_bwd(z, dycat, y, lse_tot, d, f"attn_bwd_d{d}") for d in DILATIONS]
        dcp, dcw, dpw, dps = conv_pool_bwd(z, dycat, conv_full[l], pool_b[l], pool_scale[l].reshape(1, D_POOL), "conv_pool_bwd")
        dz = assemble_dz([p[0] for p in parts], [p[1] for p in parts], [p[2] for p in parts], dcp, "assemble_dz")
        g_win, (got_out,) = wgrad(hm, dz, False, True, 1.0, _tile_for(D), D_IN // NSH, "mix_dwin", halves(g_wout))
        p_out = pair_sum(g_wout, got_out, half_index, "pair_sum")
        cm = halves(g_win)
        cm.exchange(p_out)
        dx, dxb, dnm, got = finish_sublayer("w_in", l, g_win, [dz], [G["w_in", l]], x1, mix_norm[l], dx, "mix_bwd_dh", cm, False)
        settle([("w_out", l, p_out)], got[1:])
        small_grads["mix_norm"][l], small_grads["conv_w"][l] = dnm, dcw
        small_grads["pool_w"][l], small_grads["pool_scale"][l] = dpw, dps

        dx, dxb, small_grads["ffn1_norm"][l] = ffn_backward(s1, dx, dxb, ffn1_norm, "ffn1", l, final=(l == 0))
    assert not carry

    mine = [chip_sum(partials[n], slots[n], chip_index, "chip_sum") for n in BIG]
    theirs = sibling_swap(mine)
    grads, delta, new_m, new_v = {}, {}, {}, {}
    for n, a, b in zip(BIG, mine, theirs):
        grads[n], delta[n], new_m[n], new_v[n] = adamw_halves(W[n], a, b, M[n], V[n], half_index, "adamw")

    small_full = {n: jnp.stack([a.reshape(W[n].shape[1:] if n != "conv_w" else (3, D_CONV)) for a in small_grads[n]])
                  for n in small_grads}
    small_full["final_norm"] = d_final.reshape(D)
    order = list(SMALL)
    packed = jnp.concatenate([small_full[n].reshape(-1) for n in order])
    pad = (-packed.shape[0]) % (8 * 128)
    packed = jnp.pad(packed, (0, pad)).reshape(-1, 128)
    summed = small_all_reduce(packed).reshape(-1)
    off = 0
    for n in order:
        size = small_full[n].size
        grads[n] = summed[off:off + size].reshape(small_full[n].shape)
        off += size
    grads["conv_w"] = lax.dynamic_slice_in_dim(grads["conv_w"], my_chip * (D_CONV // NSH), D_CONV // NSH, axis=2)

    def pack(src):
        flat_ = jnp.concatenate([src[n].reshape(-1) for n in order])
        return jnp.pad(flat_, (0, (-flat_.shape[0]) % (8 * 128))).reshape(-1, 128)

    ds, ms, vs = adamw(pack(W), pack(grads), pack(M), pack(V), "adamw_small")
    off = 0
    for n in order:
        size = W[n].size
        for dst, src in ((delta, ds), (new_m, ms), (new_v, vs)):
            dst[n] = src.reshape(-1)[off:off + size].reshape(W[n].shape)
        off += size

    loss = lax.psum(loss11[0, 0], ("x", "y", "c"))
    return (loss, dx.reshape(x.shape), *[grads[n] for n in WEIGHTS], *[delta[n] for n in WEIGHTS],
            *[new_m[n] for n in WEIGHTS], *[new_v[n] for n in WEIGHTS])
```

```python
import functools

import jax
import jax.numpy as jnp
from jax import lax
from jax.experimental import pallas as pl
from jax.experimental.pallas import tpu as pltpu

F32 = jnp.float32
BF16 = jnp.bfloat16
MESH = pl.DeviceIdType.MESH

RMS_EPS = 1e-6
NEG_INF = -1e30
HEAD_DIM = 64
BLK = 128
SPAN = 128
DILATIONS = (1, 4, 16)
D_ATTN = 1024
D_CONV = 512
D_POOL = 512
POOL_WINDOWS = (2, 4, 8, 16)
POOL_GROUP = 128
D_IN = 3 * D_ATTN + 3 * D_CONV + D_POOL
D_MIX = D_ATTN + D_CONV + D_POOL
N_PAIR = D_ATTN // 128
ATTN_SCALE = HEAD_DIM ** -0.5
NSH = 4
ADAM_LR, ADAM_B1, ADAM_B2, ADAM_EPS, ADAM_WD, ADAM_STEP = 0.001, 0.9, 0.999, 1e-08, 0.01, 10

VMEM_LIMIT = 56 * 2 ** 20


def _cp(n_axes):
    return pltpu.CompilerParams(dimension_semantics=("arbitrary",) * n_axes, vmem_limit_bytes=VMEM_LIMIT)


def _sds(shape, dtype):
    return jax.ShapeDtypeStruct(shape, dtype)


def _dot(a, b):
    return jnp.dot(a, b, preferred_element_type=F32)


def _dot_nt(a, b):
    return lax.dot_general(a, b, (((1,), (1,)), ((), ())), preferred_element_type=F32)


def _dot_tn(a, b):
    return lax.dot_general(a, b, (((0,), (0,)), ((), ())), preferred_element_type=F32)


def _row_tile(s):
    return min(512, s)


def _rms_bwd_tile(xv, gv, dh):
    r = lax.rsqrt(jnp.mean(xv * xv, axis=-1, keepdims=True) + RMS_EPS)
    xhat = xv * r
    dg = jnp.sum(dh * xhat, axis=0, keepdims=True)
    dxhat = dh * gv
    dx = r * (dxhat - xhat * jnp.mean(dxhat * xhat, axis=-1, keepdims=True))
    return dx, dg


def final_loss(x, g, target, name):
    S, D = x.shape
    tm = _row_tile(S)

    def body(x_ref, g_ref, t_ref, loss_ref, dx_ref, dxb_ref, dg_ref):
        i = pl.program_id(0)
        xv, gv = x_ref[...], g_ref[...]
        r = lax.rsqrt(jnp.mean(xv * xv, axis=-1, keepdims=True) + RMS_EPS)
        err = xv * r * gv - t_ref[...]
        part = 0.5 * jnp.sum(jnp.mean(err * err, axis=-1, keepdims=True), axis=0, keepdims=True)
        dx, dg = _rms_bwd_tile(xv, gv, err * (1.0 / D))

        @pl.when(i == 0)
        def _():
            loss_ref[...] = jnp.zeros_like(loss_ref)
            dg_ref[...] = jnp.zeros_like(dg_ref)

        loss_ref[...] += part
        dg_ref[...] += dg
        dx_ref[...] = dx
        dxb_ref[...] = dx.astype(BF16)

    row = pl.BlockSpec((tm, D), lambda i: (i, 0))
    vec = pl.BlockSpec((1, D), lambda i: (0, 0))
    return pl.pallas_call(
        body, grid=(S // tm,), in_specs=[row, vec, row],
        out_specs=[pl.BlockSpec((1, 1), lambda i: (0, 0)), row, row, vec],
        out_shape=[_sds((1, 1), F32), _sds((S, D), F32), _sds((S, D), BF16), _sds((1, D), F32)],
        name=name, compiler_params=_cp(1),
    )(x, g.reshape(1, D), target)


def _wspec(w, imap):
    _, a, b = w.shape
    return pl.BlockSpec((None, a, b), lambda *ids: (imap(*ids), 0, 0))


class _Comm:
    def __init__(self):
        self.ins, self.out_shapes, self.aliases, self.items = [], [], {}, []

    def _add(self, kind, operand, out_shape, alias):
        if alias:
            self.aliases[len(self.ins)] = len(self.out_shapes)
        self.items.append((kind, len(self.ins), len(self.out_shapes)))
        self.ins.append(operand)
        self.out_shapes.append(out_shape)

    def gather_first(self, src):
        self._add("first", src, _sds((NSH,) + src.shape, src.dtype), False)

    def gather_pass(self, buf):
        self._add("pass", buf, _sds(buf.shape, buf.dtype), True)

    def sibling_half(self, grad):
        self._add("half", grad, _sds((NSH, grad.shape[1] // 2, grad.shape[2]), grad.dtype), False)

    def exchange(self, partial):
        self._add("xchg", partial, _sds(partial.shape, partial.dtype), False)

    def run(self, cins, couts, send, recv, start):
        x, y, c = lax.axis_index("x"), lax.axis_index("y"), lax.axis_index("c")
        chips = [(1 - x, y), (x, 1 - y), (1 - x, 1 - y)]
        me, sib = 2 * x + y, (x, y, 1 - c)
        for it, (kind, i, o) in enumerate(self.items):
            def rc(s, src, dst, to, it=it):
                return pltpu.make_async_remote_copy(src_ref=src, dst_ref=dst, send_sem=send.at[it, s], recv_sem=recv.at[it, s],
                                                    device_id=to, device_id_type=MESH)
            src, buf = cins[i], couts[o]
            if kind == "first":
                ha = src.shape[0] // 2
                rows = pl.ds(c * ha, ha)
                cps = [rc(s, src.at[rows], buf.at[me, rows], (px, py, c)) for s, (px, py) in enumerate(chips)]
                cps.append(rc(3, src, buf.at[me], sib))
                landing = [buf.at[2 * px + py, rows] for px, py in chips] + [buf.at[me]]
            elif kind == "pass":
                ha = buf.shape[1] // 2
                rows, other = pl.ds(c * ha, ha), pl.ds((1 - c) * ha, ha)
                cps = [rc(s, buf.at[2 * px + py, rows], buf.at[2 * px + py, rows], sib) for s, (px, py) in enumerate(chips)]
                landing = [buf.at[2 * px + py, other] for px, py in chips]
            elif kind == "half":
                ha = src.shape[1] // 2
                cps = [rc(0, src.at[:, pl.ds((1 - c) * ha, ha), :], buf, sib)]
                landing = [buf]
            else:
                cps = [rc(s, src.at[2 * px + py], buf.at[me], (px, py, c)) for s, (px, py) in enumerate(chips)]
                landing = [buf.at[2 * px + py] for px, py in chips]
            if start:
                for cp in cps:
                    cp.start()
            else:
                for s, dst in enumerate(landing):
                    rc(s, dst, dst, sib).wait_recv()
                for cp in cps:
                    cp.wait_send()


def _call(body, *, grid, in_specs, out_specs, out_shape, name, args, scratch=(), comm=None):
    multi = isinstance(out_shape, (list, tuple))
    oshape = list(out_shape) if multi else [out_shape]
    ospecs = list(out_specs) if multi else [out_specs]
    if comm is None or not comm.items:
        res = pl.pallas_call(body, grid=grid, in_specs=list(in_specs), out_specs=ospecs, out_shape=oshape,
                             scratch_shapes=list(scratch), name=name, compiler_params=_cp(len(grid)))(*args)
        return (list(res) if multi else res[0]), []
    nin, nout, nci, nco, nscr = len(in_specs), len(oshape), len(comm.ins), len(comm.out_shapes), len(scratch)

    def full(*refs):
        ins, cins = refs[:nin], refs[nin:nin + nci]
        outs, couts = refs[nin + nci:nin + nci + nout], refs[nin + nci + nout:nin + nci + nout + nco]
        scr = refs[nin + nci + nout + nco:nin + nci + nout + nco + nscr]
        send, recv = refs[-2:]
        ids = [pl.program_id(a) for a in range(len(grid))]
        first = functools.reduce(jnp.logical_and, [i == 0 for i in ids])
        last = functools.reduce(jnp.logical_and, [i == g - 1 for i, g in zip(ids, grid)])

        @pl.when(first)
        def _():
            comm.run(cins, couts, send, recv, True)

        body(*ins, *outs, *scr)

        @pl.when(last)
        def _():
            comm.run(cins, couts, send, recv, False)

    sems = pltpu.SemaphoreType.DMA((len(comm.items), 4))
    res = pl.pallas_call(
        full, grid=grid, in_specs=list(in_specs) + [ANY] * nci, out_specs=ospecs + [ANY] * nco,
        out_shape=oshape + comm.out_shapes, scratch_shapes=list(scratch) + [sems, sems],
        input_output_aliases={nin + i: nout + o for i, o in comm.aliases.items()},
        name=name, compiler_params=_cp(len(grid)),
    )(*args, *comm.ins)
    main = list(res[:nout])
    return (main if multi else main[0]), list(res[nout:])


def rms_fwd(x, g, name):
    S, D = x.shape
    tm = _row_tile(S)

    def body(x_ref, g_ref, h_ref):
        xv = x_ref[...]
        r = lax.rsqrt(jnp.mean(xv * xv, axis=-1, keepdims=True) + RMS_EPS)
        h_ref[...] = (xv * r * g_ref[...]).astype(BF16)

    return pl.pallas_call(
        body, grid=(S // tm,),
        in_specs=[pl.BlockSpec((tm, D), lambda i: (i, 0)), pl.BlockSpec((1, D), lambda i: (0, 0))],
        out_specs=pl.BlockSpec((tm, D), lambda i: (i, 0)),
        out_shape=_sds((S, D), BF16), name=name, compiler_params=_cp(1),
    )(x, g.reshape(1, D))


def ffn_up(h, wg, wu, name, comm=None):
    S, D = h.shape
    FS = wg.shape[-1]
    tm = _row_tile(S)

    def body(h_ref, wg_ref, wu_ref, sg_ref, up_ref, a_ref):
        hv = h_ref[...]
        g = _dot(hv, wg_ref[...])
        u = _dot(hv, wu_ref[...])
        s = jax.nn.sigmoid(g)
        sg = g * s
        sg_ref[...] = sg.astype(BF16)
        up_ref[...] = (u * s * (1.0 + g * (1.0 - s))).astype(BF16)
        a_ref[...] = (sg * u).astype(BF16)

    out = pl.BlockSpec((tm, FS), lambda k, i: (i, k))
    shard = lambda k, i: k
    return _call(
        body, grid=(NSH, S // tm),
        in_specs=[pl.BlockSpec((tm, D), lambda k, i: (i, 0)), _wspec(wg, shard), _wspec(wu, shard)],
        out_specs=[out] * 3, out_shape=[_sds((S, NSH * FS), BF16)] * 3,
        name=name, args=(h, wg, wu), comm=comm)


def col_mm(h, w, name, comm=None):
    S, D = h.shape
    NS = w.shape[-1]
    tm = _row_tile(S)

    def body(h_ref, w_ref, z_ref):
        z_ref[...] = _dot(h_ref[...], w_ref[...])

    return _call(
        body, grid=(NSH, S // tm),
        in_specs=[pl.BlockSpec((tm, D), lambda k, i: (i, 0)), _wspec(w, lambda k, i: k)],
        out_specs=pl.BlockSpec((tm, NS), lambda k, i: (i, k)), out_shape=_sds((S, NSH * NS), F32),
        name=name, args=(h, w), comm=comm)


def resid_mm(a, w, x, scale, name, comm=None):
    S, K = a.shape
    D = w.shape[-1]
    tm = _row_tile(S)
    tn = D // 2 if D % 256 == 0 else D

    def body(a_ref, w_ref, x_ref, o_ref):
        o_ref[...] = x_ref[...] + scale * _dot(a_ref[...], w_ref[...])

    out = pl.BlockSpec((tm, tn), lambda j, i: (i, j))
    return _call(
        body, grid=(D // tn, S // tm),
        in_specs=[pl.BlockSpec((tm, K), lambda j, i: (i, 0)), pl.BlockSpec((K, tn), lambda j, i: (0, j)), out],
        out_specs=out, out_shape=_sds((S, D), F32), name=name, args=(a, w.reshape(K, D), x), comm=comm)


def ffn_bwd_act(dxb, wd, sg, up, name, comm=None):
    S, D = dxb.shape
    FS = wd.shape[-2]
    tm = _row_tile(S)

    def body(dx_ref, w_ref, sg_ref, up_ref, dg_ref, du_ref):
        da = 0.5 * _dot_nt(dx_ref[...], w_ref[...])
        du_ref[...] = (da * sg_ref[...].astype(F32)).astype(BF16)
        dg_ref[...] = (da * up_ref[...].astype(F32)).astype(BF16)

    act = pl.BlockSpec((tm, FS), lambda k, i: (i, k))
    return _call(
        body, grid=(NSH, S // tm),
        in_specs=[pl.BlockSpec((tm, D), lambda k, i: (i, 0)), _wspec(wd, lambda k, i: k), act, act],
        out_specs=[act, act], out_shape=[_sds((S, NSH * FS), BF16)] * 2,
        name=name, args=(dxb, wd, sg, up), comm=comm)


def nt_col_mm(dxb, w, name, comm=None):
    S, D = dxb.shape
    KS = w.shape[-2]
    tm = _row_tile(S)

    def body(dx_ref, w_ref, o_ref):
        o_ref[...] = _dot_nt(dx_ref[...], w_ref[...])

    return _call(
        body, grid=(NSH, S // tm),
        in_specs=[pl.BlockSpec((tm, D), lambda k, i: (i, 0)), _wspec(w, lambda k, i: k)],
        out_specs=pl.BlockSpec((tm, KS), lambda k, i: (i, k)), out_shape=_sds((S, NSH * KS), F32),
        name=name, args=(dxb, w), comm=comm)


def wgrad(lhs, rhs, lhs_sharded, rhs_sharded, scale, tr, tc, name, comm=None):
    S = lhs.shape[0]
    R = lhs.shape[1] // (NSH if lhs_sharded else 1)
    C = rhs.shape[1] // (NSH if rhs_sharded else 1)
    ts = min(2048, S)
    nr, nc = R // tr, C // tc

    def body(l_ref, r_ref, o_ref):
        @pl.when(pl.program_id(3) == 0)
        def _():
            o_ref[...] = jnp.zeros_like(o_ref)

        o_ref[...] += scale * _dot_tn(l_ref[...], r_ref[...])

    lmap = (lambda k, a, b, s: (s, k * nr + a)) if lhs_sharded else (lambda k, a, b, s: (s, a))
    rmap = (lambda k, a, b, s: (s, k * nc + b)) if rhs_sharded else (lambda k, a, b, s: (s, b))
    return _call(
        body, grid=(NSH, nr, nc, S // ts),
        in_specs=[pl.BlockSpec((ts, tr), lmap), pl.BlockSpec((ts, tc), rmap)],
        out_specs=pl.BlockSpec((None, tr, tc), lambda k, a, b, s: (k, a, b)),
        out_shape=_sds((NSH, R, C), F32), name=name, args=(lhs, rhs), comm=comm)


def bwd_dh(dys, ws, x, g, dxin, name, comm=None, after=None):
    S, D = x.shape
    NS = ws[0].shape[-1]
    tm = _row_tile(S)
    nj = len(dys)

    def mm_body(*refs):
        dy_refs, w_refs, dh_ref = refs[:nj], refs[nj:2 * nj], refs[2 * nj]

        @pl.when(pl.program_id(1) == 0)
        def _():
            dh_ref[...] = jnp.zeros_like(dh_ref)

        for dy_ref, w_ref in zip(dy_refs, w_refs):
            dh_ref[...] += _dot_nt(dy_ref[...], w_ref[...])

    dh, comm_out = _call(
        mm_body, grid=(S // tm, NSH),
        in_specs=[pl.BlockSpec((tm, NS), lambda i, k: (i, k))] * nj + [_wspec(w, lambda i, k: k) for w in ws],
        out_specs=pl.BlockSpec((tm, D), lambda i, k: (i, 0)), out_shape=_sds((S, D), F32),
        name=name + "_mm", args=(*dys, *ws), comm=comm)

    def norm_body(dh_ref, x_ref, g_ref, dxin_ref, dx_ref, dxb_ref, dg_ref):
        dx, dg = _rms_bwd_tile(x_ref[...], g_ref[...], dh_ref[...])
        tot = dxin_ref[...] + dx
        dx_ref[...] = tot
        dxb_ref[...] = tot.astype(BF16)

        @pl.when(pl.program_id(0) == 0)
        def _():
            dg_ref[...] = jnp.zeros_like(dg_ref)

        dg_ref[...] += dg

    row = pl.BlockSpec((tm, D), lambda i: (i, 0))
    vec = pl.BlockSpec((1, D), lambda i: (0, 0))
    comm2 = after(comm_out) if after is not None else None
    (dx, dxb, dgain), comm2_out = _call(
        norm_body, grid=(S // tm,), in_specs=[row, row, vec, row], out_specs=[row, row, vec],
        out_shape=[_sds((S, D), F32), _sds((S, D), BF16), _sds((1, D), F32)],
        name=name + "_norm", args=(dh, x, g.reshape(1, D), dxin), comm=comm2)
    return dx, dxb, dgain, comm_out, comm2_out


PAIRS_PER_STEP = {1: 8, 4: 1, 16: 1}
ZQ, ZK, ZV = 0, D_ATTN // 128, 2 * D_ATTN // 128


def _band_valid(n):
    qi = lax.broadcasted_iota(jnp.int32, (BLK, 2 * BLK), 0)
    kj = lax.broadcasted_iota(jnp.int32, (BLK, 2 * BLK), 1)
    dist = qi + BLK - kj
    return (dist >= 0) & (dist <= SPAN) & ((kj >= BLK) | (n > 0))


PERM_FROM = 8


def _residue_perm(d, transpose):
    q = 128 // d
    a = lax.broadcasted_iota(jnp.int32, (128, 128), 1 if transpose else 0)
    b = lax.broadcasted_iota(jnp.int32, (128, 128), 0 if transpose else 1)
    return (b == (a % q) * d + a // q).astype(BF16)


def _perm_apply(perm, x, n_terms):
    out, rest = None, x
    for t in range(n_terms):
        term = rest.astype(BF16)
        out = _dot(perm, term) if out is None else out + _dot(perm, term)
        if t + 1 < n_terms:
            rest = rest - term.astype(F32)
    return out


def _split_residues(src_ref, dst, d, pp, row0=0, n_terms=1):
    if d < PERM_FROM:
        for r in range(d):
            for p in range(pp):
                dst[r * pp + p, row0:row0 + BLK, :] = src_ref[pl.ds(r, BLK, stride=d), p * 128:(p + 1) * 128]
        return
    perm, q = _residue_perm(d, False), 128 // d
    for c in range(d):
        t = _perm_apply(perm, src_ref[c * 128:(c + 1) * 128, :], n_terms)
        for r in range(d):
            dst[r, row0 + c * q:row0 + (c + 1) * q, :] = t[r * q:(r + 1) * q]


def _shift_keys(ks, vs, n):
    @pl.when(n == 0)
    def _():
        ks[:, :BLK, :] = jnp.zeros((ks.shape[0], BLK, 128), ks.dtype)
        vs[:, :BLK, :] = jnp.zeros((vs.shape[0], BLK, 128), vs.dtype)

    @pl.when(n > 0)
    def _():
        ks[:, :BLK, :] = ks[:, BLK:, :]
        vs[:, :BLK, :] = vs[:, BLK:, :]


def _merge_residues(bufs, d, pp, c, n_terms=2):
    q = 128 // d
    t = jnp.concatenate([bufs[r, c * q:(c + 1) * q, :] for r in range(d)], axis=0)
    return _perm_apply(_residue_perm(d, True), t, n_terms)


def _lane_pick(stat, lane, idx):
    return jnp.sum(jnp.where(lane == idx, stat, 0.0), axis=-1, keepdims=True)


def _residue_view(dilation):
    return 1, dilation


def _view_shape(shape, dv):
    return (shape[0] // dv, dv, shape[1]) if dv > 1 else tuple(shape)


def _viewed(a, dv):
    return a.reshape(_view_shape(a.shape, dv))


def _view_spec(dv, rows, width, row_of, col_of):
    if dv > 1:
        return pl.BlockSpec((rows, None, width), lambda r, a, b: (row_of(a, b), r, col_of(a, b)))
    return pl.BlockSpec((rows, width), lambda r, a, b: (row_of(a, b), col_of(a, b)))


def attn_fwd(z, dilation, name):
    S = z.shape[0]
    dv, d = _residue_view(dilation)
    R = BLK * d
    nb = S // (BLK * dilation)
    pp = PAIRS_PER_STEP[d]
    G = d * pp
    W = 128 * pp

    def body(q_ref, kc_ref, vc_ref, o_ref, lse_ref, qs, ks, vs, os_, ls):
        hb, n = pl.program_id(1), pl.program_id(2)
        valid = _band_valid(n)
        lane = lax.broadcasted_iota(jnp.int32, (1, 128), 1)
        _shift_keys(ks, vs, n)
        _split_residues(q_ref, qs, d, pp)
        _split_residues(kc_ref, ks, d, pp, BLK)
        _split_residues(vc_ref, vs, d, pp, BLK)

        def step(j, carry):
            q2, k2, v2 = (qs[j] * ATTN_SCALE).astype(BF16), ks[j].astype(BF16), vs[j].astype(BF16)
            pair = hb * pp + j % pp
            o2 = jnp.zeros((BLK, 128), F32)
            stat = jnp.zeros((BLK, 128), F32)
            for h in range(2):
                hm = (lane < HEAD_DIM) if h == 0 else (lane >= HEAD_DIM)
                qm = jnp.where(hm, q2, jnp.zeros_like(q2))
                vm = jnp.where(hm, v2, jnp.zeros_like(v2))
                s = jnp.where(valid, _dot_nt(qm, k2), NEG_INF)
                m = jnp.max(s, axis=-1, keepdims=True)
                p = jnp.exp(s - m)
                lsum = jnp.sum(p, axis=-1, keepdims=True)
                o2 = o2 + _dot(p.astype(BF16), vm) / lsum
                stat = jnp.where(lane == 2 * pair + h, m + jnp.log(lsum), stat)
            os_[j] = o2
            ls[j] = stat
            return carry

        lax.fori_loop(0, G, step, 0, unroll=min(G, 8))

        if d < PERM_FROM:
            for r in range(d):
                rows = pl.ds(r, BLK, stride=d)
                acc = ls[r * pp]
                for p in range(pp):
                    o_ref[rows, p * 128:(p + 1) * 128] = os_[r * pp + p]
                    if p:
                        acc = acc + ls[r * pp + p]
                lse_ref[rows, :] = acc
        else:
            for c in range(d):
                rows = slice(c * 128, (c + 1) * 128)
                o_ref[rows, :] = _merge_residues(os_, d, pp, c)
                lse_ref[rows, :] = _merge_residues(ls, d, pp, c, n_terms=3)

    nhb = N_PAIR // pp
    cur = lambda c: _view_spec(dv, R, W, lambda hb, n: n, lambda hb, n: c // pp + hb)
    zv = _viewed(z, dv)
    o, lse = pl.pallas_call(
        body, grid=(dv, nhb, nb),
        in_specs=[cur(ZQ), cur(ZK), cur(ZV)],
        out_specs=[_view_spec(dv, R, W, lambda hb, n: n, lambda hb, n: hb), _view_spec(dv, R, 128, lambda hb, n: n, lambda hb, n: hb)],
        out_shape=[_sds(_view_shape((S, D_ATTN), dv), F32), _sds(_view_shape((S, nhb * 128), dv), F32)],
        scratch_shapes=[pltpu.VMEM((G, BLK, 128), F32), pltpu.VMEM((G, 2 * BLK, 128), F32), pltpu.VMEM((G, 2 * BLK, 128), F32),
                        pltpu.VMEM((G, BLK, 128), F32), pltpu.VMEM((G, BLK, 128), F32)],
        name=name, compiler_params=_cp(3),
    )(zv, zv, zv)
    return o.reshape(S, D_ATTN), lse.reshape(S, nhb * 128)


def _pair_weights(w, lane):
    return [jnp.where(lane < HEAD_DIM, w[:, 2 * hp:2 * hp + 1], w[:, 2 * hp + 1:2 * hp + 2]) for hp in range(N_PAIR)]


def attn_merge(os_, lses, name):
    S = os_[0].shape[0]
    tm = _row_tile(S)
    npat = len(os_)

    def body(*refs):
        o_refs, l_refs = refs[:npat], refs[npat:2 * npat]
        y_ref, yb_ref, lt_ref = refs[2 * npat:]
        lane = lax.broadcasted_iota(jnp.int32, (1, 128), 1)
        ls = [functools.reduce(jnp.add, [r[:, b * 128:(b + 1) * 128] for b in range(r.shape[1] // 128)]) for r in l_refs]
        mx = functools.reduce(jnp.maximum, ls)
        es = [jnp.exp(v - mx) for v in ls]
        den = functools.reduce(jnp.add, es)
        lt_ref[...] = mx + jnp.log(den)
        ws = [_pair_weights(e / den, lane) for e in es]
        for hp in range(N_PAIR):
            sl = slice(hp * 128, (hp + 1) * 128)
            y = ws[0][hp] * o_refs[0][:, sl]
            for p in range(1, npat):
                y = y + ws[p][hp] * o_refs[p][:, sl]
            y_ref[:, sl] = y
            yb_ref[:, sl] = y.astype(BF16)

    big = pl.BlockSpec((tm, D_ATTN), lambda i: (i, 0))
    st = pl.BlockSpec((tm, 128), lambda i: (i, 0))
    return pl.pallas_call(
        body, grid=(S // tm,), in_specs=[big] * npat + [pl.BlockSpec((tm, l.shape[1]), lambda i: (i, 0)) for l in lses],
        out_specs=[big, big, st],
        out_shape=[_sds((S, D_ATTN), F32), _sds((S, D_ATTN), BF16), _sds((S, 128), F32)],
        name=name, compiler_params=_cp(1),
    )(*os_, *lses)


def attn_bwd(z, dycat, y, lse_tot, dilation, name):
    S = z.shape[0]
    dv, d = _residue_view(dilation)
    R = BLK * d
    nb = S // (BLK * dilation)
    pp = PAIRS_PER_STEP[d]
    G = d * pp
    W = 128 * pp

    def body(q_ref, kc_ref, vc_ref, dy_ref, y_ref, l_ref, dq_ref, dk_ref, dv_ref,
             qs, ks, vs, dys, ys, ls, dqs, dks, dvs, ck, cv):
        hb, n = pl.program_id(1), pl.program_id(2)

        def store_rows(ref, buf):
            if d < PERM_FROM:
                for r in range(d):
                    for p in range(pp):
                        ref[pl.ds(r, BLK, stride=d), p * 128:(p + 1) * 128] = buf[r * pp + p]
            else:
                for c in range(d):
                    ref[c * 128:(c + 1) * 128, :] = _merge_residues(buf, d, pp, c)

        @pl.when(n < nb)
        def _():
            valid = _band_valid(n)
            first = n == 0
            lane = lax.broadcasted_iota(jnp.int32, (1, 128), 1)
            _shift_keys(ks, vs, n)
            _split_residues(q_ref, qs, d, pp)
            _split_residues(kc_ref, ks, d, pp, BLK)
            _split_residues(vc_ref, vs, d, pp, BLK)
            _split_residues(dy_ref, dys, d, pp, n_terms=2)
            _split_residues(y_ref, ys, d, pp, n_terms=2)
            _split_residues(l_ref, ls, d, 1, n_terms=3)

            def step(j, carry):
                q2, k2, v2, dy2, y2 = (qs[j] * ATTN_SCALE).astype(BF16), ks[j].astype(BF16), vs[j].astype(BF16), dys[j], ys[j]
                stat = ls[j // pp]
                pair = hb * pp + j % pp
                dq2 = jnp.zeros((BLK, 128), F32)
                dk2 = jnp.zeros((2 * BLK, 128), F32)
                dv2 = jnp.zeros((2 * BLK, 128), F32)
                for h in range(2):
                    hm = (lane < HEAD_DIM) if h == 0 else (lane >= HEAD_DIM)
                    qm = jnp.where(hm, q2, jnp.zeros_like(q2))
                    km = jnp.where(hm, k2, jnp.zeros_like(k2))
                    dym = jnp.where(hm, dy2, 0.0)
                    dymb = dym.astype(BF16)
                    s = _dot_nt(qm, k2)
                    p = jnp.where(valid, jnp.exp(s - _lane_pick(stat, lane, 2 * pair + h)), 0.0)
                    dp = _dot_nt(dymb, v2)
                    delta = jnp.sum(dym * y2, axis=-1, keepdims=True)
                    ds = (p * (dp - delta)).astype(BF16)
                    dq2 = dq2 + _dot(ds, km)
                    dk2 = dk2 + _dot_tn(ds, qm)
                    dv2 = dv2 + _dot_tn(p.astype(BF16), dymb)
                dqs[j] = dq2 * ATTN_SCALE
                dks[j] = jnp.where(first, 0.0, ck[j]) + dk2[:BLK]
                dvs[j] = jnp.where(first, 0.0, cv[j]) + dv2[:BLK]
                ck[j] = dk2[BLK:]
                cv[j] = dv2[BLK:]
                return carry

            lax.fori_loop(0, G, step, 0, unroll=min(G, 8))
            store_rows(dq_ref, dqs)
            store_rows(dk_ref, dks)
            store_rows(dv_ref, dvs)

        @pl.when(n == nb)
        def _():
            store_rows(dk_ref, ck)
            store_rows(dv_ref, cv)

    cur = lambda n: jnp.minimum(n, nb - 1)
    zb = lambda c: _view_spec(dv, R, W, lambda hb, n: cur(n), lambda hb, n: c // pp + hb)
    big_cur = _view_spec(dv, R, W, lambda hb, n: cur(n), lambda hb, n: hb)
    big_lag = _view_spec(dv, R, W, lambda hb, n: jnp.maximum(n - 1, 0), lambda hb, n: hb)
    buf = lambda rows, dt: pltpu.VMEM((G, rows, 128), dt)
    zv = _viewed(z, dv)
    outs = pl.pallas_call(
        body, grid=(dv, N_PAIR // pp, nb + 1),
        in_specs=[zb(ZQ), zb(ZK), zb(ZV), big_cur, big_cur, _view_spec(dv, R, 128, lambda hb, n: cur(n), lambda hb, n: 0)],
        out_specs=[big_cur, big_lag, big_lag],
        out_shape=[_sds(_view_shape((S, D_ATTN), dv), F32)] * 3,
        scratch_shapes=[buf(BLK, F32), buf(2 * BLK, F32), buf(2 * BLK, F32), buf(BLK, F32), buf(BLK, F32),
                        pltpu.VMEM((d, BLK, 128), F32), buf(BLK, F32), buf(BLK, F32), buf(BLK, F32), buf(BLK, F32), buf(BLK, F32)],
        name=name, compiler_params=_cp(3),
    )(zv, zv, zv, _viewed(dycat, dv), _viewed(y, dv), _viewed(lse_tot, dv))
    return tuple(o.reshape(S, D_ATTN) for o in outs)


ZC_GB, ZC_GC, ZC_CI, ZC_PI = 6, 7, 8, 9


def _pool_counts(i, tb):
    pos = lax.broadcasted_iota(jnp.int32, (tb, D_POOL), 0) + i * tb + 1
    grp = lax.broadcasted_iota(jnp.int32, (tb, D_POOL), 1) // POOL_GROUP
    win = jnp.where(grp == 0, POOL_WINDOWS[0], jnp.where(grp == 1, POOL_WINDOWS[1],
                    jnp.where(grp == 2, POOL_WINDOWS[2], POOL_WINDOWS[3])))
    return jnp.minimum(pos, win).astype(F32), grp


def _select_group(stages, grp):
    return jnp.where(grp == 0, stages[0], jnp.where(grp == 1, stages[1], jnp.where(grp == 2, stages[2], stages[3])))


def _causal_window_sums(x2):
    s1 = x2 + pltpu.roll(x2, 1, 0)
    s2 = s1 + pltpu.roll(s1, 2, 0)
    s3 = s2 + pltpu.roll(s2, 4, 0)
    s4 = s3 + pltpu.roll(s3, 8, 0)
    return [s1, s2, s3, s4]


def _anticausal_window_sums(x2):
    n = x2.shape[0]
    s1 = x2 + pltpu.roll(x2, n - 1, 0)
    s2 = s1 + pltpu.roll(s1, n - 2, 0)
    s3 = s2 + pltpu.roll(s2, n - 4, 0)
    s4 = s3 + pltpu.roll(s3, n - 8, 0)
    return [s1, s2, s3, s4]


def _pooled(p_prev, p_cur, i, tb):
    x2 = jnp.concatenate([jnp.where(i > 0, p_prev, 0.0), p_cur], axis=0)
    count, grp = _pool_counts(i, tb)
    win_sum = _select_group([s[tb:] for s in _causal_window_sums(x2)], grp)
    return win_sum / count - p_cur, count, grp


def _pool_mm(v, pw_ref, nt):
    outs = []
    for gi in range(len(POOL_WINDOWS)):
        sl = slice(gi * POOL_GROUP, (gi + 1) * POOL_GROUP)
        outs.append(_dot_nt(v[:, sl], pw_ref[gi]) if nt else _dot(v[:, sl], pw_ref[gi]))
    return jnp.concatenate(outs, axis=1)


def conv_pool_fwd(z, conv_w, pool_w, pool_scale, name):
    S = z.shape[0]
    tb = min(256, S)

    def body(gb_ref, gc_ref, gcp_ref, ci_ref, cip_ref, pi_ref, pip_ref, cw_ref, pw_ref, ps_ref, yc_ref, yp_ref):
        i = pl.program_id(0)
        u2 = jnp.concatenate([jnp.where(i > 0, gcp_ref[...] * cip_ref[...], 0.0), gc_ref[...] * ci_ref[...]], axis=0)
        conv = cw_ref[0:1, :] * pltpu.roll(u2, 2, 0) + cw_ref[1:2, :] * pltpu.roll(u2, 1, 0) + cw_ref[2:3, :] * u2
        yc_ref[...] = (gb_ref[...] * conv[tb:]).astype(BF16)
        pooled, _, _ = _pooled(pip_ref[...], pi_ref[...], i, tb)
        yp_ref[...] = (_pool_mm(pooled.astype(BF16), pw_ref, False) * ps_ref[...]).astype(BF16)

    cur = lambda c: pl.BlockSpec((tb, 512), lambda i: (i, c))
    prev = lambda c: pl.BlockSpec((tb, 512), lambda i: (jnp.maximum(i - 1, 0), c))
    full = lambda a: pl.BlockSpec(a.shape, lambda i: (0,) * a.ndim)
    out = pl.BlockSpec((tb, 512), lambda i: (i, 0))
    return pl.pallas_call(
        body, grid=(S // tb,),
        in_specs=[cur(ZC_GB), cur(ZC_GC), prev(ZC_GC), cur(ZC_CI), prev(ZC_CI), cur(ZC_PI), prev(ZC_PI),
                  full(conv_w), full(pool_w), full(pool_scale)],
        out_specs=[out, out], out_shape=[_sds((S, 512), BF16)] * 2, name=name, compiler_params=_cp(1),
    )(z, z, z, z, z, z, z, conv_w, pool_w, pool_scale)


def conv_pool_bwd(z, dycat, conv_w, pool_w, pool_scale, name):
    S = z.shape[0]
    tb = min(256, S)
    nblk = S // tb
    ng = len(POOL_WINDOWS)

    def body(gb_ref, gbn_ref, gc_ref, gcp_ref, ci_ref, cip_ref, pi_ref, pip_ref, dyc_ref, dycn_ref, dyp_ref, dypn_ref,
             cw_ref, pw_ref, ps_ref, dz_ref, dcw_ref, dpw_ref, dps_ref):
        i = pl.program_id(0)
        last = i == nblk - 1

        @pl.when(i == 0)
        def _():
            dcw_ref[...] = jnp.zeros_like(dcw_ref)
            dpw_ref[...] = jnp.zeros_like(dpw_ref)
            dps_ref[...] = jnp.zeros_like(dps_ref)

        gc, ci = gc_ref[...], ci_ref[...]
        u2 = jnp.concatenate([jnp.where(i > 0, gcp_ref[...] * cip_ref[...], 0.0), gc * ci], axis=0)
        um2, um1, u0 = pltpu.roll(u2, 2, 0)[tb:], pltpu.roll(u2, 1, 0)[tb:], u2[tb:]
        conv = cw_ref[0:1, :] * um2 + cw_ref[1:2, :] * um1 + cw_ref[2:3, :] * u0
        dyc = dyc_ref[...]
        dconv = dyc * gb_ref[...]
        dc2 = jnp.concatenate([dconv, jnp.where(last, 0.0, dycn_ref[...] * gbn_ref[...])], axis=0)
        du = (cw_ref[2:3, :] * dconv + cw_ref[1:2, :] * pltpu.roll(dc2, 2 * tb - 1, 0)[:tb]
              + cw_ref[0:1, :] * pltpu.roll(dc2, 2 * tb - 2, 0)[:tb])
        dz_ref[:, 0:512] = (dyc * conv).astype(BF16)
        dz_ref[:, 512:1024] = (du * ci).astype(BF16)
        dz_ref[:, 1024:1536] = (du * gc).astype(BF16)
        dcw_ref[0:1, :] += jnp.sum(dconv * um2, axis=0, keepdims=True)
        dcw_ref[1:2, :] += jnp.sum(dconv * um1, axis=0, keepdims=True)
        dcw_ref[2:3, :] += jnp.sum(dconv * u0, axis=0, keepdims=True)

        pooled, count, grp = _pooled(pip_ref[...], pi_ref[...], i, tb)
        pooled_b = pooled.astype(BF16)
        t = _pool_mm(pooled_b, pw_ref, False)
        dyp, ps = dyp_ref[...], ps_ref[...]
        dps_ref[...] += jnp.sum(dyp * t, axis=0, keepdims=True)
        dt_b = (dyp * ps).astype(BF16)
        for gi in range(ng):
            sl = slice(gi * POOL_GROUP, (gi + 1) * POOL_GROUP)
            dpw_ref[gi] += _dot_tn(pooled_b[:, sl], dt_b[:, sl])
        dpooled = _pool_mm(dt_b, pw_ref, True)
        dpooled_n = _pool_mm((dypn_ref[...] * ps).astype(BF16), pw_ref, True)
        count_n, _ = _pool_counts(i + 1, tb)
        dq2 = jnp.concatenate([dpooled / count, jnp.where(last, 0.0, dpooled_n / count_n)], axis=0)
        lead = _select_group([s[:tb] for s in _anticausal_window_sums(dq2)], grp)
        dz_ref[:, 1536:2048] = (lead - dpooled).astype(BF16)

    cur = lambda c: pl.BlockSpec((tb, 512), lambda i: (i, c))
    prev = lambda c: pl.BlockSpec((tb, 512), lambda i: (jnp.maximum(i - 1, 0), c))
    nxt = lambda c: pl.BlockSpec((tb, 512), lambda i: (jnp.minimum(i + 1, nblk - 1), c))
    full = lambda a: pl.BlockSpec(a.shape, lambda i: (0,) * a.ndim)
    yc_c, yp_c = D_ATTN // 512, D_ATTN // 512 + 1
    return pl.pallas_call(
        body, grid=(nblk,),
        in_specs=[cur(ZC_GB), nxt(ZC_GB), cur(ZC_GC), prev(ZC_GC), cur(ZC_CI), prev(ZC_CI), cur(ZC_PI), prev(ZC_PI),
                  cur(yc_c), nxt(yc_c), cur(yp_c), nxt(yp_c), full(conv_w), full(pool_w), full(pool_scale)],
        out_specs=[pl.BlockSpec((tb, 2048), lambda i: (i, 0)), pl.BlockSpec((3, 512), lambda i: (0, 0)),
                   pl.BlockSpec((ng, POOL_GROUP, POOL_GROUP), lambda i: (0, 0, 0)), pl.BlockSpec((1, 512), lambda i: (0, 0))],
        out_shape=[_sds((S, 2048), BF16), _sds((3, 512), F32), _sds((ng, POOL_GROUP, POOL_GROUP), F32), _sds((1, 512), F32)],
        name=name, compiler_params=_cp(1),
    )(z, z, z, z, z, z, z, z, dycat, dycat, dycat, dycat, conv_w, pool_w, pool_scale)


def assemble_dz(parts_q, parts_k, parts_v, dcp, name):
    S = dcp.shape[0]
    tm = _row_tile(S)
    npat = len(parts_q)

    def body(*refs):
        dz_ref = refs[-1]
        dcp_ref = refs[-2]
        for j in range(3):
            acc = refs[j * npat][...]
            for p in range(1, npat):
                acc = acc + refs[j * npat + p][...]
            dz_ref[:, j * D_ATTN:(j + 1) * D_ATTN] = acc.astype(BF16)
        dz_ref[:, 3 * D_ATTN:] = dcp_ref[...]

    big = pl.BlockSpec((tm, D_ATTN), lambda i: (i, 0))
    return pl.pallas_call(
        body, grid=(S // tm,), in_specs=[big] * (3 * npat) + [pl.BlockSpec((tm, D_IN - 3 * D_ATTN), lambda i: (i, 0))],
        out_specs=pl.BlockSpec((tm, D_IN), lambda i: (i, 0)), out_shape=_sds((S, D_IN), BF16),
        name=name, compiler_params=_cp(1),
    )(*parts_q, *parts_k, *parts_v, dcp)


def adamw(w, g, m, v, name):
    shape = w.shape
    cols = shape[-1]
    rows = w.size // cols
    tr = rows
    for cand in (256, 128, 64, 32, 16, 8):
        if rows % cand == 0:
            tr = cand
            break
    c1 = 1.0 - ADAM_B1 ** ADAM_STEP
    c2 = 1.0 - ADAM_B2 ** ADAM_STEP

    def body(w_ref, g_ref, m_ref, v_ref, d_ref, mo_ref, vo_ref):
        gv = g_ref[...]
        mn = ADAM_B1 * m_ref[...] + (1.0 - ADAM_B1) * gv
        vn = ADAM_B2 * v_ref[...] + (1.0 - ADAM_B2) * (gv * gv)
        d_ref[...] = -ADAM_LR * ((mn / c1) / (jnp.sqrt(vn / c2) + ADAM_EPS) + ADAM_WD * w_ref[...])
        mo_ref[...] = mn
        vo_ref[...] = vn

    blk = pl.BlockSpec((tr, cols), lambda i: (i, 0))
    outs = pl.pallas_call(
        body, grid=(rows // tr,), in_specs=[blk] * 4, out_specs=[blk] * 3,
        out_shape=[_sds((rows, cols), F32)] * 3, name=name, compiler_params=_cp(1),
    )(*(a.reshape(rows, cols) for a in (w, g, m, v)))
    return tuple(o.reshape(shape) for o in outs)


ANY = pl.BlockSpec(memory_space=pl.ANY)


def _place():
    x, y, c = lax.axis_index("x"), lax.axis_index("y"), lax.axis_index("c")
    chips = [(1 - x, y), (x, 1 - y), (1 - x, 1 - y)]
    return x, y, c, chips


def gather_weights(ws):
    nw = len(ws)
    split = [w.ndim == 2 and w.shape[0] % 32 == 0 for w in ws]

    def body(*refs):
        ins, outs = refs[:nw], refs[nw:2 * nw]
        send, recv = refs[2 * nw:]
        x, y, c, chips = _place()
        me, sib = 2 * x + y, (x, y, 1 - c)

        def half(j, k, hc):
            if not split[j]:
                return outs[j].at[k]
            ha = ws[j].shape[0] // 2
            return outs[j].at[k, pl.ds(hc * ha, ha), :]

        def rcopy(j, s, src, dst, to):
            return pltpu.make_async_remote_copy(src_ref=src, dst_ref=dst, send_sem=send.at[j, s], recv_sem=recv.at[j, s],
                                                device_id=to, device_id_type=MESH)

        first = [rcopy(j, 6, ins[j], outs[j].at[me], sib) for j in range(nw)]
        for j in range(nw):
            ha = ws[j].shape[0] // 2
            mine = ins[j].at[pl.ds(c * ha, ha), :] if split[j] else ins[j]
            for s, (px, py) in enumerate(chips):
                first.append(rcopy(j, s, mine, half(j, me, c), (px, py, c)))
        for cp in first:
            cp.start()
        passed = []
        for j in range(nw):
            for s, (px, py) in enumerate(chips):
                blk = half(j, 2 * px + py, c)
                rcopy(j, s, blk, blk, (px, py, c)).wait_recv()
                if split[j]:
                    fwd = rcopy(j, 3 + s, blk, blk, sib)
                    fwd.start()
                    passed.append(fwd)
        for j in range(nw):
            for s, (px, py) in enumerate(chips):
                if split[j]:
                    blk = half(j, 2 * px + py, 1 - c)
                    rcopy(j, 3 + s, blk, blk, sib).wait_recv()
        for j in range(nw):
            rcopy(j, 6, ins[j], outs[j].at[me], sib).wait_recv()
        for cp in first + passed:
            cp.wait_send()

    return pl.pallas_call(
        body, in_specs=[ANY] * nw, out_specs=[ANY] * nw,
        out_shape=[_sds((NSH,) + w.shape, w.dtype) for w in ws],
        scratch_shapes=[pltpu.SemaphoreType.DMA((nw, 7)), pltpu.SemaphoreType.DMA((nw, 7))],
        name="gather_weights",
    )(*ws)


def pair_sum(g, got, half_index, name):
    _, A, B = g.shape
    ha = A // 2
    tr = ha
    for cand in (512, 256, 128, 64):
        if ha % cand == 0:
            tr = cand
            break
    nt = ha // tr

    def body(c_ref, g_ref, r_ref, o_ref):
        del c_ref
        o_ref[...] = (g_ref[...] + r_ref[...]).astype(BF16)

    return pl.pallas_call(
        body,
        grid_spec=pltpu.PrefetchScalarGridSpec(
            num_scalar_prefetch=1, grid=(NSH, nt),
            in_specs=[pl.BlockSpec((None, tr, B), lambda k, t, c: (k, c[0] * nt + t, 0)),
                      pl.BlockSpec((None, tr, B), lambda k, t, c: (k, t, 0))],
            out_specs=pl.BlockSpec((None, tr, B), lambda k, t, c: (k, t, 0))),
        out_shape=_sds((NSH, ha, B), BF16), name=name, compiler_params=_cp(2),
    )(half_index, g, got)


def _half_tile(ha):
    for cand in (512, 256, 128, 64):
        if ha % cand == 0:
            return cand
    return ha


def chip_sum(partials, slots, chip_index, name):
    depth = len(partials)
    _, ha, B = partials[0].shape
    tr = _half_tile(ha)
    nt = ha // tr

    def body(me_ref, *refs):
        p_refs, s_refs, o_ref = refs[:depth], refs[depth:depth + depth * NSH], refs[depth + depth * NSH]
        l = pl.program_id(0)
        for ll in range(depth):
            @pl.when(l == ll)
            def _(ll=ll):
                own = p_refs[ll][...].astype(F32)
                acc = jnp.where(me_ref[0] == 0, own, s_refs[ll * NSH][...].astype(F32))
                for k in range(1, NSH):
                    acc = acc + jnp.where(me_ref[0] == k, own, s_refs[ll * NSH + k][...].astype(F32))
                o_ref[...] = acc

    def frozen(ll):
        return lambda l, t: jnp.where(l == ll, t, jnp.where(l < ll, 0, nt - 1))

    def slot(k):
        return lambda me: jnp.where(me[0] == k, (k + 1) % NSH, k)

    in_specs = [pl.BlockSpec((None, tr, B), lambda l, t, me, ll=ll: (me[0], frozen(ll)(l, t), 0)) for ll in range(depth)]
    in_specs += [pl.BlockSpec((None, tr, B), lambda l, t, me, ll=ll, k=k: (slot(k)(me), frozen(ll)(l, t), 0))
                 for ll in range(depth) for k in range(NSH)]
    return pl.pallas_call(
        body,
        grid_spec=pltpu.PrefetchScalarGridSpec(
            num_scalar_prefetch=1, grid=(depth, nt), in_specs=in_specs,
            out_specs=pl.BlockSpec((None, tr, B), lambda l, t, me: (l, t, 0))),
        out_shape=_sds((depth, ha, B), F32), name=name, compiler_params=_cp(2),
    )(chip_index, *partials, *[s for s in slots for _ in range(NSH)])


def sibling_swap(mine):
    n = len(mine)

    def body(*refs):
        ins, outs = refs[:n], refs[n:2 * n]
        send, recv = refs[2 * n:]
        x, y, c, _ = _place()
        cps = [pltpu.make_async_remote_copy(src_ref=ins[j], dst_ref=outs[j], send_sem=send.at[j], recv_sem=recv.at[j],
                                            device_id=(x, y, 1 - c), device_id_type=MESH) for j in range(n)]
        for cp in cps:
            cp.start()
        for cp in cps:
            cp.wait()

    return pl.pallas_call(
        body, in_specs=[ANY] * n, out_specs=[ANY] * n, out_shape=[_sds(m.shape, m.dtype) for m in mine],
        scratch_shapes=[pltpu.SemaphoreType.DMA((n,)), pltpu.SemaphoreType.DMA((n,))], name="sibling_swap",
    )(*mine)


def adamw_halves(w, mine, theirs, m, v, core_index, name):
    depth, A, B = w.shape
    ha = A // 2
    tr = _half_tile(ha)
    while tr * B * 4 > 2 ** 20 and tr % 16 == 0:
        tr //= 2
    nt = ha // tr
    c1 = 1.0 - ADAM_B1 ** ADAM_STEP
    c2 = 1.0 - ADAM_B2 ** ADAM_STEP

    def body(c_ref, w_ref, a_ref, b_ref, m_ref, v_ref, g_ref, d_ref, mo_ref, vo_ref):
        gv = jnp.where(pl.program_id(1) == c_ref[0], a_ref[...], b_ref[...])
        mn = ADAM_B1 * m_ref[...] + (1.0 - ADAM_B1) * gv
        vn = ADAM_B2 * v_ref[...] + (1.0 - ADAM_B2) * (gv * gv)
        g_ref[...] = gv
        d_ref[...] = -ADAM_LR * ((mn / c1) / (jnp.sqrt(vn / c2) + ADAM_EPS) + ADAM_WD * w_ref[...])
        mo_ref[...] = mn
        vo_ref[...] = vn

    full = pl.BlockSpec((None, tr, B), lambda l, h, t, c: (l, h * nt + t, 0))
    a_spec = pl.BlockSpec((None, tr, B), lambda l, h, t, c: (l, jnp.where(h == c[0], t, 0), 0))
    b_spec = pl.BlockSpec((None, tr, B), lambda l, h, t, c: (l, jnp.where(h == c[0], 0, t), 0))
    return pl.pallas_call(
        body,
        grid_spec=pltpu.PrefetchScalarGridSpec(
            num_scalar_prefetch=1, grid=(depth, 2, nt), in_specs=[full, a_spec, b_spec, full, full], out_specs=[full] * 4),
        out_shape=[_sds(w.shape, F32)] * 4, name=name, compiler_params=_cp(3),
    )(core_index, w, mine, theirs, m, v)


def small_all_reduce(v):
    R = v.shape[0]

    def body(v_ref, o_ref, slots, send, recv):
        x, y, c, _ = _place()
        me = 4 * x + 2 * y + c
        slots[me] = v_ref[...]
        cps = []
        for m in range(1, 8):
            mx, my, mc = (m >> 2) & 1, (m >> 1) & 1, m & 1
            cps.append(pltpu.make_async_remote_copy(
                src_ref=v_ref, dst_ref=slots.at[me], send_sem=send.at[m - 1], recv_sem=recv.at[m - 1],
                device_id=(x ^ mx, y ^ my, c ^ mc), device_id_type=MESH))
        for cp in cps:
            cp.start()
        for cp in cps:
            cp.wait()
        acc = slots[0]
        for d in range(1, 8):
            acc = acc + slots[d]
        o_ref[...] = acc

    vm = pl.BlockSpec(memory_space=pltpu.VMEM)
    return pl.pallas_call(
        body, in_specs=[vm], out_specs=vm, out_shape=_sds((R, 128), F32),
        scratch_shapes=[pltpu.VMEM((8, R, 128), F32), pltpu.SemaphoreType.DMA((7,)), pltpu.SemaphoreType.DMA((7,))],
        name="small_all_reduce",
    )(v)


BIG = ("ffn1_w_gate", "ffn1_w_up", "ffn1_w_down", "w_in", "w_out", "ffn2_w_gate", "ffn2_w_up", "ffn2_w_down")
SMALL = ("ffn1_norm", "mix_norm", "conv_w", "pool_w", "pool_scale", "ffn2_norm", "final_norm")
WEIGHTS = ("ffn1_norm", "ffn1_w_gate", "ffn1_w_up", "ffn1_w_down", "mix_norm", "w_in", "conv_w", "pool_w", "pool_scale",
           "w_out", "ffn2_norm", "ffn2_w_gate", "ffn2_w_up", "ffn2_w_down", "final_norm")


def _tile_for(n, cap=1024):
    best = 128
    for t in range(128, min(n, cap) + 1, 128):
        if n % t == 0:
            best = t
    return n if n <= cap else best


GATHER_BEHIND = {
    (0, "ffn1_up"): [("w_in", 0), ("w_out", 0), ("ffn2_w_gate", 0), ("ffn2_w_up", 0)],
    (0, "ffn1_down"): [("ffn2_w_down", 0)],
    (0, "mix_in"): [("ffn1_w_gate", 1)],
    (0, "mix_out"): [("ffn1_w_up", 1)],
    (0, "ffn2_up"): [("ffn1_w_down", 1), ("w_in", 1), ("w_out", 1)],
    (0, "ffn2_down"): [("ffn2_w_gate", 1)],
    (1, "ffn1_up"): [("ffn2_w_up", 1), ("ffn2_w_down", 1)],
}


class _GatherPlan:
    def __init__(self, local):
        self.local, self.pending, self.ready = local, [], {}

    def comm(self, firsts):
        cm = _Comm()
        self._passing, self._firsts = list(self.pending), list(firsts)
        for _, buf in self._passing:
            cm.gather_pass(buf)
        for key in self._firsts:
            cm.gather_first(self.local[key])
        return cm

    def done(self, couts):
        npass = len(self._passing)
        for (key, _), buf in zip(self._passing, couts[:npass]):
            self.ready[key] = buf
        self.pending = list(zip(self._firsts, couts[npass:]))


def kernel(x, ffn1_norm, ffn1_w_gate, ffn1_w_up, ffn1_w_down, mix_norm, w_in, conv_w, pool_w, pool_scale, w_out, ffn2_norm, ffn2_w_gate, ffn2_w_up, ffn2_w_down, final_norm, loss_target, m_ffn1_norm, m_ffn1_w_gate, m_ffn1_w_up, m_ffn1_w_down, m_mix_norm, m_w_in, m_conv_w, m_pool_w, m_pool_scale, m_w_out, m_ffn2_norm, m_ffn2_w_gate, m_ffn2_w_up, m_ffn2_w_down, m_final_norm, v_ffn1_norm, v_ffn1_w_gate, v_ffn1_w_up, v_ffn1_w_down, v_mix_norm, v_w_in, v_conv_w, v_pool_w, v_pool_scale, v_w_out, v_ffn2_norm, v_ffn2_w_gate, v_ffn2_w_up, v_ffn2_w_down, v_final_norm):
    given = dict(locals())
    W = {n: given[n] for n in WEIGHTS}
    M = {n: given["m_" + n] for n in WEIGHTS}
    V = {n: given["v_" + n] for n in WEIGHTS}
    depth = ffn1_norm.shape[0]
    D = x.shape[-1]
    xs = x[0]
    my_chip = 2 * lax.axis_index("x") + lax.axis_index("y")
    my_core = lax.axis_index("c")

    half_index = my_core.astype(jnp.int32).reshape(1)
    chip_index = my_chip.astype(jnp.int32).reshape(1)
    plan = _GatherPlan({(n, l): W[n][l].astype(BF16) for n in BIG for l in range(depth)})
    head = [("ffn1_w_gate", 0), ("ffn1_w_up", 0), ("ffn1_w_down", 0)]
    gathered = gather_weights([plan.local[key] for key in head] + [conv_w])
    plan.ready.update(zip(head, gathered[:-1]))
    conv_full = jnp.moveaxis(gathered[-1], 0, 2).reshape(depth, 3, D_CONV)
    pool_b = pool_w.astype(BF16)
    G = plan.ready

    def behind(l, stage):
        return plan.comm([(n, ll) for n, ll in GATHER_BEHIND.get((l, stage), []) if ll < depth])

    saved = []
    cur = xs
    for l in range(depth):
        x0 = cur
        h = rms_fwd(x0, ffn1_norm[l], "ffn1_norm")
        (g, u, a), got = ffn_up(h, G["ffn1_w_gate", l], G["ffn1_w_up", l], "ffn1_up", behind(l, "ffn1_up"))
        plan.done(got)
        cur, got = resid_mm(a, G["ffn1_w_down", l], x0, 0.5, "ffn1_down", behind(l, "ffn1_down"))
        plan.done(got)
        s1 = (x0, h, g, u, a)

        x1 = cur
        hm = rms_fwd(x1, mix_norm[l], "mix_norm")
        z, got = col_mm(hm, G["w_in", l], "mix_in", behind(l, "mix_in"))
        plan.done(got)
        pats = [attn_fwd(z, d, f"attn_fwd_d{d}") for d in DILATIONS]
        y, yb, lse_tot = attn_merge([p[0] for p in pats], [p[1] for p in pats], "attn_merge")
        y_conv, y_pool = conv_pool_fwd(z, conv_full[l], pool_b[l], pool_scale[l].reshape(1, D_POOL), "conv_pool_fwd")
        ycat = jnp.concatenate([yb, y_conv, y_pool], axis=1)
        cur, got = resid_mm(ycat, G["w_out", l], x1, 1.0, "mix_out", behind(l, "mix_out"))
        plan.done(got)
        sm = (x1, hm, z, y, lse_tot, ycat)

        x2 = cur
        h = rms_fwd(x2, ffn2_norm[l], "ffn2_norm")
        (g, u, a), got = ffn_up(h, G["ffn2_w_gate", l], G["ffn2_w_up", l], "ffn2_up", behind(l, "ffn2_up"))
        plan.done(got)
        cur, got = resid_mm(a, G["ffn2_w_down", l], x2, 0.5, "ffn2_down", behind(l, "ffn2_down"))
        plan.done(got)
        saved.append((s1, sm, (x2, h, g, u, a)))
    assert not plan.pending and len(G) == len(BIG) * depth

    loss11, dx, dxb, d_final = final_loss(cur, final_norm, loss_target[0], "final_loss")

    small_grads = {n: [None] * depth for n in SMALL if n != "final_norm"}
    partials = {n: [None] * depth for n in BIG}
    slots = {n: [None] * depth for n in BIG}

    def halves(grad):
        cm = _Comm()
        cm.sibling_half(grad)
        return cm

    carry = []

    def carried_comm():
        cm = _Comm()
        taken = list(carry)
        carry.clear()
        for _, _, p in taken:
            cm.exchange(p)
        return cm, taken

    def settle(taken, got):
        for (n, ll, p), s in zip(taken, got):
            partials[n][ll], slots[n][ll] = p, s

    def finish_sublayer(name, l, grad, dys, ws, x_in, gain, dx, tag, cm, final):
        if not final:
            dx, dxb, dnorm, got, _ = bwd_dh(dys, ws, x_in, gain, dx, tag, cm)
            carry.append((name, l, pair_sum(grad, got[0], half_index, "pair_sum")))
            return dx, dxb, dnorm, got
        last = {}

        def after(got):
            last["p"] = pair_sum(grad, got[0], half_index, "pair_sum")
            cm2 = _Comm()
            cm2.exchange(last["p"])
            return cm2

        dx, dxb, dnorm, got, got2 = bwd_dh(dys, ws, x_in, gain, dx, tag, cm, after)
        partials[name][l], slots[name][l] = last["p"], got2[0]
        return dx, dxb, dnorm, got

    def ffn_backward(sv, dx, dxb, norm, tag, l, final=False):
        x_in, h, g, u, a = sv
        names = [f"{tag}_w_down", f"{tag}_w_gate", f"{tag}_w_up"]
        wd, wg, wu = (G[n, l] for n in names)
        FS = wg.shape[-1]
        cm0, taken = carried_comm()
        (dg, du), got0 = ffn_bwd_act(dxb, wd, g, u, f"{tag}_bwd_act", cm0)
        settle(taken, got0)
        g_d, _ = wgrad(a, dxb, True, False, 0.5, FS, _tile_for(D), f"{tag}_dwd")
        g_g, (got_d,) = wgrad(h, dg, False, True, 1.0, _tile_for(D), FS, f"{tag}_dwg", halves(g_d))
        g_u, (got_g,) = wgrad(h, du, False, True, 1.0, _tile_for(D), FS, f"{tag}_dwu", halves(g_g))
        p_d = pair_sum(g_d, got_d, half_index, "pair_sum")
        p_g = pair_sum(g_g, got_g, half_index, "pair_sum")
        cm = halves(g_u)
        cm.exchange(p_d)
        cm.exchange(p_g)
        dx, dxb, dnorm, got = finish_sublayer(names[2], l, g_u, [dg, du], [wg, wu], x_in, norm[l], dx, f"{tag}_bwd_dh", cm, final)
        settle([(names[0], l, p_d), (names[1], l, p_g)], got[1:])
        return dx, dxb, dnorm

    for l in reversed(range(depth)):
        s1, sm, s2 = saved[l]
        dx, dxb, small_grads["ffn2_norm"][l] = ffn_backward(s2, dx, dxb, ffn2_norm, "ffn2", l)

        x1, hm, z, y, lse_tot, ycat = sm
        cm0, taken = carried_comm()
        dycat, got0 = nt_col_mm(dxb, G["w_out", l], "mix_out_bwd", cm0)
        settle(taken, got0)
        g_wout, _ = wgrad(ycat, dxb, True, False, 1.0, D_MIX // NSH, _tile_for(D, 2048), "mix_dwout")
        parts = [attn_bwd(z, dycat, y, lse_tot, d, f"attn_bwd_d{d}") for d in DILATIONS]
        dcp, dcw, dpw, dps = conv_pool_bwd(z, dycat, conv_full[l], pool_b[l], pool_scale[l].reshape(1, D_POOL), "conv_pool_bwd")
        dz = assemble_dz([p[0] for p in parts], [p[1] for p in parts], [p[2] for p in parts], dcp, "assemble_dz")
        g_win, (got_out,) = wgrad(hm, dz, False, True, 1.0, _tile_for(D), D_IN // NSH, "mix_dwin", halves(g_wout))
        p_out = pair_sum(g_wout, got_out, half_index, "pair_sum")
        cm = halves(g_win)
        cm.exchange(p_out)
        dx, dxb, dnm, got = finish_sublayer("w_in", l, g_win, [dz], [G["w_in", l]], x1, mix_norm[l], dx, "mix_bwd_dh", cm, False)
        settle([("w_out", l, p_out)], got[1:])
        small_grads["mix_norm"][l], small_grads["conv_w"][l] = dnm, dcw
        small_grads["pool_w"][l], small_grads["pool_scale"][l] = dpw, dps

        dx, dxb, small_grads["ffn1_norm"][l] = ffn_backward(s1, dx, dxb, ffn1_norm, "ffn1", l, final=(l == 0))
    assert not carry

    mine = [chip_sum(partials[n], slots[n], chip_index, "chip_sum") for n in BIG]
    theirs = sibling_swap(mine)
    grads, delta, new_m, new_v = {}, {}, {}, {}
    for n, a, b in zip(BIG, mine, theirs):
        grads[n], delta[n], new_m[n], new_v[n] = adamw_halves(W[n], a, b, M[n], V[n], half_index, "adamw")

    small_full = {n: jnp.stack([a.reshape(W[n].shape[1:] if n != "conv_w" else (3, D_CONV)) for a in small_grads[n]])
                  for n in small_grads}
    small_full["final_norm"] = d_final.reshape(D)
    order = list(SMALL)
    packed = jnp.concatenate([small_full[n].reshape(-1) for n in order])
    pad = (-packed.shape[0]) % (8 * 128)
    packed = jnp.pad(packed, (0, pad)).reshape(-1, 128)
    summed = small_all_reduce(packed).reshape(-1)
    off = 0
    for n in order:
        size = small_full[n].size
        grads[n] = summed[off:off + size].reshape(small_full[n].shape)
        off += size
    grads["conv_w"] = lax.dynamic_slice_in_dim(grads["conv_w"], my_chip * (D_CONV // NSH), D_CONV // NSH, axis=2)

    def pack(src):
        flat_ = jnp.concatenate([src[n].reshape(-1) for n in order])
        return jnp.pad(flat_, (0, (-flat_.shape[0]) % (8 * 128))).reshape(-1, 128)

    ds, ms, vs = adamw(pack(W), pack(grads), pack(M), pack(V), "adamw_small")
    off = 0
    for n in order:
        size = W[n].size
        for dst, src in ((delta, ds), (new_m, ms), (new_v, vs)):
            dst[n] = src.reshape(-1)[off:off + size].reshape(W[n].shape)
        off += size

    loss = lax.psum(loss11[0, 0], ("x", "y", "c"))
    return (loss, dx.reshape(x.shape), *[grads[n] for n in WEIGHTS], *[delta[n] for n in WEIGHTS],
            *[new_m[n] for n in WEIGHTS], *[new_v[n] for n in WEIGHTS])
```

```python
import functools

import jax
import jax.numpy as jnp
from jax import lax
from jax.experimental import pallas as pl
from jax.experimental.pallas import tpu as pltpu

F32 = jnp.float32
BF16 = jnp.bfloat16
MESH = pl.DeviceIdType.MESH

RMS_EPS = 1e-6
NEG_INF = -1e30
HEAD_DIM = 64
BLK = 128
SPAN = 128
DILATIONS = (1, 4, 16)
D_ATTN = 1024
D_CONV = 512
D_POOL = 512
POOL_WINDOWS = (2, 4, 8, 16)
POOL_GROUP = 128
D_IN = 3 * D_ATTN + 3 * D_CONV + D_POOL
D_MIX = D_ATTN + D_CONV + D_POOL
N_PAIR = D_ATTN // 128
ATTN_SCALE = HEAD_DIM ** -0.5
NSH = 4
ADAM_LR, ADAM_B1, ADAM_B2, ADAM_EPS, ADAM_WD, ADAM_STEP = 0.001, 0.9, 0.999, 1e-08, 0.01, 10

VMEM_LIMIT = 56 * 2 ** 20


def _cp(n_axes):
    return pltpu.CompilerParams(dimension_semantics=("arbitrary",) * n_axes, vmem_limit_bytes=VMEM_LIMIT)


def _sds(shape, dtype):
    return jax.ShapeDtypeStruct(shape, dtype)


def _dot(a, b):
    return jnp.dot(a, b, preferred_element_type=F32)


def _dot_nt(a, b):
    return lax.dot_general(a, b, (((1,), (1,)), ((), ())), preferred_element_type=F32)


def _dot_tn(a, b):
    return lax.dot_general(a, b, (((0,), (0,)), ((), ())), preferred_element_type=F32)


def _row_tile(s):
    return min(512, s)


def _rms_bwd_tile(xv, gv, dh):
    r = lax.rsqrt(jnp.mean(xv * xv, axis=-1, keepdims=True) + RMS_EPS)
    xhat = xv * r
    dg = jnp.sum(dh * xhat, axis=0, keepdims=True)
    dxhat = dh * gv
    dx = r * (dxhat - xhat * jnp.mean(dxhat * xhat, axis=-1, keepdims=True))
    return dx, dg


def final_loss(x, g, target, name):
    S, D = x.shape
    tm = _row_tile(S)

    def body(x_ref, g_ref, t_ref, loss_ref, dx_ref, dxb_ref, dg_ref):
        i = pl.program_id(0)
        xv, gv = x_ref[...], g_ref[...]
        r = lax.rsqrt(jnp.mean(xv * xv, axis=-1, keepdims=True) + RMS_EPS)
        err = xv * r * gv - t_ref[...]
        part = 0.5 * jnp.sum(jnp.mean(err * err, axis=-1, keepdims=True), axis=0, keepdims=True)
        dx, dg = _rms_bwd_tile(xv, gv, err * (1.0 / D))

        @pl.when(i == 0)
        def _():
            loss_ref[...] = jnp.zeros_like(loss_ref)
            dg_ref[...] = jnp.zeros_like(dg_ref)

        loss_ref[...] += part
        dg_ref[...] += dg
        dx_ref[...] = dx
        dxb_ref[...] = dx.astype(BF16)

    row = pl.BlockSpec((tm, D), lambda i: (i, 0))
    vec = pl.BlockSpec((1, D), lambda i: (0, 0))
    return pl.pallas_call(
        body, grid=(S // tm,), in_specs=[row, vec, row],
        out_specs=[pl.BlockSpec((1, 1), lambda i: (0, 0)), row, row, vec],
        out_shape=[_sds((1, 1), F32), _sds((S, D), F32), _sds((S, D), BF16), _sds((1, D), F32)],
        name=name, compiler_params=_cp(1),
    )(x, g.reshape(1, D), target)


def _wspec(w, imap):
    _, a, b = w.shape
    return pl.BlockSpec((None, a, b), lambda *ids: (imap(*ids), 0, 0))


class _Comm:
    def __init__(self):
        self.ins, self.out_shapes, self.aliases, self.items = [], [], {}, []

    def _add(self, kind, operand, out_shape, alias):
        if alias:
            self.aliases[len(self.ins)] = len(self.out_shapes)
        self.items.append((kind, len(self.ins), len(self.out_shapes)))
        self.ins.append(operand)
        self.out_shapes.append(out_shape)

    def gather_first(self, src):
        self._add("first", src, _sds((NSH,) + src.shape, src.dtype), False)

    def gather_pass(self, buf):
        self._add("pass", buf, _sds(buf.shape, buf.dtype), True)

    def sibling_half(self, grad):
        self._add("half", grad, _sds((NSH, grad.shape[1] // 2, grad.shape[2]), grad.dtype), False)

    def sibling_whole(self, arr):
        self._add("whole", arr, _sds(arr.shape, arr.dtype), False)

    def exchange(self, partial):
        self._add("xchg", partial, _sds(partial.shape, partial.dtype), False)

    def run(self, cins, couts, send, recv, start):
        x, y, c = lax.axis_index("x"), lax.axis_index("y"), lax.axis_index("c")
        chips = [(1 - x, y), (x, 1 - y), (1 - x, 1 - y)]
        me, sib = 2 * x + y, (x, y, 1 - c)
        for it, (kind, i, o) in enumerate(self.items):
            def rc(s, src, dst, to, it=it):
                return pltpu.make_async_remote_copy(src_ref=src, dst_ref=dst, send_sem=send.at[it, s], recv_sem=recv.at[it, s],
                                                    device_id=to, device_id_type=MESH)
            src, buf = cins[i], couts[o]
            if kind == "first":
                ha = src.shape[0] // 2
                rows = pl.ds(c * ha, ha)
                cps = [rc(s, src.at[rows], buf.at[me, rows], (px, py, c)) for s, (px, py) in enumerate(chips)]
                cps.append(rc(3, src, buf.at[me], sib))
                landing = [buf.at[2 * px + py, rows] for px, py in chips] + [buf.at[me]]
            elif kind == "pass":
                ha = buf.shape[1] // 2
                rows, other = pl.ds(c * ha, ha), pl.ds((1 - c) * ha, ha)
                cps = [rc(s, buf.at[2 * px + py, rows], buf.at[2 * px + py, rows], sib) for s, (px, py) in enumerate(chips)]
                landing = [buf.at[2 * px + py, other] for px, py in chips]
            elif kind == "whole":
                cps = [rc(0, src, buf, sib)]
                landing = [buf]
            elif kind == "half":
                ha = src.shape[1] // 2
                cps = [rc(0, src.at[:, pl.ds((1 - c) * ha, ha), :], buf, sib)]
                landing = [buf]
            else:
                cps = [rc(s, src.at[2 * px + py], buf.at[me], (px, py, c)) for s, (px, py) in enumerate(chips)]
                landing = [buf.at[2 * px + py] for px, py in chips]
            if start:
                for cp in cps:
                    cp.start()
            else:
                for s, dst in enumerate(landing):
                    rc(s, dst, dst, sib).wait_recv()
                for cp in cps:
                    cp.wait_send()


def _call(body, *, grid, in_specs, out_specs, out_shape, name, args, scratch=(), comm=None, prefetch=()):
    multi = isinstance(out_shape, (list, tuple))
    oshape = list(out_shape) if multi else [out_shape]
    ospecs = list(out_specs) if multi else [out_specs]
    npf = len(prefetch)

    def launch(kernel_fn, ispecs, ospecs_, oshape_, scratch_, aliases, operands):
        if not npf:
            return pl.pallas_call(kernel_fn, grid=grid, in_specs=ispecs, out_specs=ospecs_, out_shape=oshape_,
                                  scratch_shapes=scratch_, input_output_aliases=aliases, name=name,
                                  compiler_params=_cp(len(grid)))(*operands)
        spec = pltpu.PrefetchScalarGridSpec(num_scalar_prefetch=npf, grid=grid, in_specs=ispecs, out_specs=ospecs_,
                                            scratch_shapes=scratch_)
        return pl.pallas_call(kernel_fn, grid_spec=spec, out_shape=oshape_, input_output_aliases=aliases, name=name,
                              compiler_params=_cp(len(grid)))(*prefetch, *operands)

    if comm is None or not comm.items:
        res = launch(body, list(in_specs), ospecs, oshape, list(scratch), {}, args)
        return (list(res) if multi else res[0]), []
    nin, nout, nci, nco, nscr = len(in_specs), len(oshape), len(comm.ins), len(comm.out_shapes), len(scratch)

    def full(*refs):
        pf, refs = refs[:npf], refs[npf:]
        ins, cins = refs[:nin], refs[nin:nin + nci]
        outs, couts = refs[nin + nci:nin + nci + nout], refs[nin + nci + nout:nin + nci + nout + nco]
        scr = refs[nin + nci + nout + nco:nin + nci + nout + nco + nscr]
        send, recv = refs[-2:]
        ids = [pl.program_id(a) for a in range(len(grid))]
        first = functools.reduce(jnp.logical_and, [i == 0 for i in ids])
        last = functools.reduce(jnp.logical_and, [i == g - 1 for i, g in zip(ids, grid)])

        @pl.when(first)
        def _():
            comm.run(cins, couts, send, recv, True)

        body(*pf, *ins, *outs, *scr)

        @pl.when(last)
        def _():
            comm.run(cins, couts, send, recv, False)

    sems = pltpu.SemaphoreType.DMA((len(comm.items), 4))
    res = launch(full, list(in_specs) + [ANY] * nci, ospecs + [ANY] * nco, oshape + comm.out_shapes,
                 list(scratch) + [sems, sems], {npf + nin + i: nout + o for i, o in comm.aliases.items()},
                 (*args, *comm.ins))
    main = list(res[:nout])
    return (main if multi else main[0]), list(res[nout:])


def rms_fwd(x, g, name):
    S, D = x.shape
    tm = _row_tile(S)

    def body(x_ref, g_ref, h_ref):
        xv = x_ref[...]
        r = lax.rsqrt(jnp.mean(xv * xv, axis=-1, keepdims=True) + RMS_EPS)
        h_ref[...] = (xv * r * g_ref[...]).astype(BF16)

    return pl.pallas_call(
        body, grid=(S // tm,),
        in_specs=[pl.BlockSpec((tm, D), lambda i: (i, 0)), pl.BlockSpec((1, D), lambda i: (0, 0))],
        out_specs=pl.BlockSpec((tm, D), lambda i: (i, 0)),
        out_shape=_sds((S, D), BF16), name=name, compiler_params=_cp(1),
    )(x, g.reshape(1, D))


def ffn_up(h, wg, wu, name, comm=None):
    S, D = h.shape
    FS = wg.shape[-1]
    tm = _row_tile(S)

    def body(h_ref, wg_ref, wu_ref, sg_ref, up_ref, a_ref):
        hv = h_ref[...]
        g = _dot(hv, wg_ref[...])
        u = _dot(hv, wu_ref[...])
        s = jax.nn.sigmoid(g)
        sg = g * s
        sg_ref[...] = sg.astype(BF16)
        up_ref[...] = (u * s * (1.0 + g * (1.0 - s))).astype(BF16)
        a_ref[...] = (sg * u).astype(BF16)

    out = pl.BlockSpec((tm, FS), lambda k, i: (i, k))
    shard = lambda k, i: k
    return _call(
        body, grid=(NSH, S // tm),
        in_specs=[pl.BlockSpec((tm, D), lambda k, i: (i, 0)), _wspec(wg, shard), _wspec(wu, shard)],
        out_specs=[out] * 3, out_shape=[_sds((S, NSH * FS), BF16)] * 3,
        name=name, args=(h, wg, wu), comm=comm)


def col_mm(h, w, name, comm=None):
    S, D = h.shape
    NS = w.shape[-1]
    tm = _row_tile(S)

    def body(h_ref, w_ref, z_ref):
        z_ref[...] = _dot(h_ref[...], w_ref[...])

    return _call(
        body, grid=(NSH, S // tm),
        in_specs=[pl.BlockSpec((tm, D), lambda k, i: (i, 0)), _wspec(w, lambda k, i: k)],
        out_specs=pl.BlockSpec((tm, NS), lambda k, i: (i, k)), out_shape=_sds((S, NSH * NS), F32),
        name=name, args=(h, w), comm=comm)


def resid_mm(a, w, x, scale, name, comm=None):
    S, K = a.shape
    D = w.shape[-1]
    tm = _row_tile(S)
    tn = D // 2 if D % 256 == 0 else D

    def body(a_ref, w_ref, x_ref, o_ref):
        o_ref[...] = x_ref[...] + scale * _dot(a_ref[...], w_ref[...])

    out = pl.BlockSpec((tm, tn), lambda j, i: (i, j))
    return _call(
        body, grid=(D // tn, S // tm),
        in_specs=[pl.BlockSpec((tm, K), lambda j, i: (i, 0)), pl.BlockSpec((K, tn), lambda j, i: (0, j)), out],
        out_specs=out, out_shape=_sds((S, D), F32), name=name, args=(a, w.reshape(K, D), x), comm=comm)


def ffn_bwd_act(dxb, wd, sg, up, name, comm=None):
    S, D = dxb.shape
    FS = wd.shape[-2]
    tm = _row_tile(S)

    def body(dx_ref, w_ref, sg_ref, up_ref, dg_ref, du_ref):
        da = 0.5 * _dot_nt(dx_ref[...], w_ref[...])
        du_ref[...] = (da * sg_ref[...].astype(F32)).astype(BF16)
        dg_ref[...] = (da * up_ref[...].astype(F32)).astype(BF16)

    act = pl.BlockSpec((tm, FS), lambda k, i: (i, k))
    return _call(
        body, grid=(NSH, S // tm),
        in_specs=[pl.BlockSpec((tm, D), lambda k, i: (i, 0)), _wspec(wd, lambda k, i: k), act, act],
        out_specs=[act, act], out_shape=[_sds((S, NSH * FS), BF16)] * 2,
        name=name, args=(dxb, wd, sg, up), comm=comm)


def nt_col_mm(dxb, w, name, comm=None):
    S, D = dxb.shape
    KS = w.shape[-2]
    tm = _row_tile(S)

    def body(dx_ref, w_ref, o_ref):
        o_ref[...] = _dot_nt(dx_ref[...], w_ref[...])

    return _call(
        body, grid=(NSH, S // tm),
        in_specs=[pl.BlockSpec((tm, D), lambda k, i: (i, 0)), _wspec(w, lambda k, i: k)],
        out_specs=pl.BlockSpec((tm, KS), lambda k, i: (i, k)), out_shape=_sds((S, NSH * KS), F32),
        name=name, args=(dxb, w), comm=comm)


def wgrad(lhs, rhs, lhs_sharded, rhs_sharded, scale, tr, tc, name, comm=None):
    S = lhs.shape[0]
    R = lhs.shape[1] // (NSH if lhs_sharded else 1)
    C = rhs.shape[1] // (NSH if rhs_sharded else 1)
    ts = min(2048, S)
    nr, nc = R // tr, C // tc

    def body(l_ref, r_ref, o_ref):
        @pl.when(pl.program_id(3) == 0)
        def _():
            o_ref[...] = jnp.zeros_like(o_ref)

        o_ref[...] += scale * _dot_tn(l_ref[...], r_ref[...])

    lmap = (lambda k, a, b, s: (s, k * nr + a)) if lhs_sharded else (lambda k, a, b, s: (s, a))
    rmap = (lambda k, a, b, s: (s, k * nc + b)) if rhs_sharded else (lambda k, a, b, s: (s, b))
    return _call(
        body, grid=(NSH, nr, nc, S // ts),
        in_specs=[pl.BlockSpec((ts, tr), lmap), pl.BlockSpec((ts, tc), rmap)],
        out_specs=pl.BlockSpec((None, tr, tc), lambda k, a, b, s: (k, a, b)),
        out_shape=_sds((NSH, R, C), F32), name=name, args=(lhs, rhs), comm=comm)


def bwd_dh(dys, ws, x, g, dxin, name, comm=None, after=None):
    S, D = x.shape
    NS = ws[0].shape[-1]
    tm = _row_tile(S)
    nj = len(dys)

    def mm_body(*refs):
        dy_refs, w_refs, dh_ref = refs[:nj], refs[nj:2 * nj], refs[2 * nj]

        @pl.when(pl.program_id(1) == 0)
        def _():
            dh_ref[...] = jnp.zeros_like(dh_ref)

        for dy_ref, w_ref in zip(dy_refs, w_refs):
            dh_ref[...] += _dot_nt(dy_ref[...], w_ref[...])

    dh, comm_out = _call(
        mm_body, grid=(S // tm, NSH),
        in_specs=[pl.BlockSpec((tm, NS), lambda i, k: (i, k))] * nj + [_wspec(w, lambda i, k: k) for w in ws],
        out_specs=pl.BlockSpec((tm, D), lambda i, k: (i, 0)), out_shape=_sds((S, D), F32),
        name=name + "_mm", args=(*dys, *ws), comm=comm)

    def norm_body(dh_ref, x_ref, g_ref, dxin_ref, dx_ref, dxb_ref, dg_ref):
        dx, dg = _rms_bwd_tile(x_ref[...], g_ref[...], dh_ref[...])
        tot = dxin_ref[...] + dx
        dx_ref[...] = tot
        dxb_ref[...] = tot.astype(BF16)

        @pl.when(pl.program_id(0) == 0)
        def _():
            dg_ref[...] = jnp.zeros_like(dg_ref)

        dg_ref[...] += dg

    row = pl.BlockSpec((tm, D), lambda i: (i, 0))
    vec = pl.BlockSpec((1, D), lambda i: (0, 0))
    comm2 = after(comm_out) if after is not None else None
    (dx, dxb, dgain), comm2_out = _call(
        norm_body, grid=(S // tm,), in_specs=[row, row, vec, row], out_specs=[row, row, vec],
        out_shape=[_sds((S, D), F32), _sds((S, D), BF16), _sds((1, D), F32)],
        name=name + "_norm", args=(dh, x, g.reshape(1, D), dxin), comm=comm2)
    return dx, dxb, dgain, comm_out, comm2_out


PAIRS_PER_STEP = {1: 8, 4: 1, 16: 1}
ZQ, ZK, ZV = 0, D_ATTN // 128, 2 * D_ATTN // 128


def _band_valid(n):
    qi = lax.broadcasted_iota(jnp.int32, (BLK, 2 * BLK), 0)
    kj = lax.broadcasted_iota(jnp.int32, (BLK, 2 * BLK), 1)
    dist = qi + BLK - kj
    return (dist >= 0) & (dist <= SPAN) & ((kj >= BLK) | (n > 0))


PERM_FROM = 8


def _residue_perm(d, transpose):
    q = 128 // d
    a = lax.broadcasted_iota(jnp.int32, (128, 128), 1 if transpose else 0)
    b = lax.broadcasted_iota(jnp.int32, (128, 128), 0 if transpose else 1)
    return (b == (a % q) * d + a // q).astype(BF16)


def _perm_apply(perm, x, n_terms):
    out, rest = None, x
    for t in range(n_terms):
        term = rest.astype(BF16)
        out = _dot(perm, term) if out is None else out + _dot(perm, term)
        if t + 1 < n_terms:
            rest = rest - term.astype(F32)
    return out


def _split_residues(src_ref, dst, d, pp, row0=0, n_terms=1):
    if d < PERM_FROM:
        for r in range(d):
            for p in range(pp):
                dst[r * pp + p, row0:row0 + BLK, :] = src_ref[pl.ds(r, BLK, stride=d), p * 128:(p + 1) * 128]
        return
    perm, q = _residue_perm(d, False), 128 // d
    for c in range(d):
        t = _perm_apply(perm, src_ref[c * 128:(c + 1) * 128, :], n_terms)
        for r in range(d):
            dst[r, row0 + c * q:row0 + (c + 1) * q, :] = t[r * q:(r + 1) * q]


def _shift_keys(ks, vs, n):
    @pl.when(n == 0)
    def _():
        ks[:, :BLK, :] = jnp.zeros((ks.shape[0], BLK, 128), ks.dtype)
        vs[:, :BLK, :] = jnp.zeros((vs.shape[0], BLK, 128), vs.dtype)

    @pl.when(n > 0)
    def _():
        ks[:, :BLK, :] = ks[:, BLK:, :]
        vs[:, :BLK, :] = vs[:, BLK:, :]


def _merge_residues(bufs, d, pp, c, n_terms=2):
    q = 128 // d
    t = jnp.concatenate([bufs[r, c * q:(c + 1) * q, :] for r in range(d)], axis=0)
    return _perm_apply(_residue_perm(d, True), t, n_terms)


def _lane_pick(stat, lane, idx):
    return jnp.sum(jnp.where(lane == idx, stat, 0.0), axis=-1, keepdims=True)


def _residue_view(dilation):
    return 1, dilation


def _view_shape(shape, dv):
    return (shape[0] // dv, dv, shape[1]) if dv > 1 else tuple(shape)


def _viewed(a, dv):
    return a.reshape(_view_shape(a.shape, dv))


def _view_spec(dv, rows, width, row_of, col_of):
    if dv > 1:
        return pl.BlockSpec((rows, None, width), lambda r, a, b: (row_of(a, b), r, col_of(a, b)))
    return pl.BlockSpec((rows, width), lambda r, a, b: (row_of(a, b), col_of(a, b)))


def attn_fwd(z, dilation, name):
    S = z.shape[0]
    dv, d = _residue_view(dilation)
    R = BLK * d
    nb = S // (BLK * dilation)
    pp = PAIRS_PER_STEP[d]
    G = d * pp
    W = 128 * pp

    def body(q_ref, kc_ref, vc_ref, o_ref, lse_ref, qs, ks, vs, os_, ls):
        hb, n = pl.program_id(1), pl.program_id(2)
        valid = _band_valid(n)
        lane = lax.broadcasted_iota(jnp.int32, (1, 128), 1)
        _shift_keys(ks, vs, n)
        _split_residues(q_ref, qs, d, pp)
        _split_residues(kc_ref, ks, d, pp, BLK)
        _split_residues(vc_ref, vs, d, pp, BLK)

        def step(j, carry):
            q2, k2, v2 = (qs[j] * ATTN_SCALE).astype(BF16), ks[j].astype(BF16), vs[j].astype(BF16)
            pair = hb * pp + j % pp
            o2 = jnp.zeros((BLK, 128), F32)
            stat = jnp.zeros((BLK, 128), F32)
            for h in range(2):
                hm = (lane < HEAD_DIM) if h == 0 else (lane >= HEAD_DIM)
                qm = jnp.where(hm, q2, jnp.zeros_like(q2))
                vm = jnp.where(hm, v2, jnp.zeros_like(v2))
                s = jnp.where(valid, _dot_nt(qm, k2), NEG_INF)
                m = jnp.max(s, axis=-1, keepdims=True)
                p = jnp.exp(s - m)
                lsum = jnp.sum(p, axis=-1, keepdims=True)
                o2 = o2 + _dot(p.astype(BF16), vm) / lsum
                stat = jnp.where(lane == 2 * pair + h, m + jnp.log(lsum), stat)
            os_[j] = o2
            ls[j] = stat
            return carry

        lax.fori_loop(0, G, step, 0, unroll=min(G, 8))

        if d < PERM_FROM:
            for r in range(d):
                rows = pl.ds(r, BLK, stride=d)
                acc = ls[r * pp]
                for p in range(pp):
                    o_ref[rows, p * 128:(p + 1) * 128] = os_[r * pp + p]
                    if p:
                        acc = acc + ls[r * pp + p]
                lse_ref[rows, :] = acc
        else:
            for c in range(d):
                rows = slice(c * 128, (c + 1) * 128)
                o_ref[rows, :] = _merge_residues(os_, d, pp, c)
                lse_ref[rows, :] = _merge_residues(ls, d, pp, c, n_terms=3)

    nhb = N_PAIR // pp
    cur = lambda c: _view_spec(dv, R, W, lambda hb, n: n, lambda hb, n: c // pp + hb)
    zv = _viewed(z, dv)
    o, lse = pl.pallas_call(
        body, grid=(dv, nhb, nb),
        in_specs=[cur(ZQ), cur(ZK), cur(ZV)],
        out_specs=[_view_spec(dv, R, W, lambda hb, n: n, lambda hb, n: hb), _view_spec(dv, R, 128, lambda hb, n: n, lambda hb, n: hb)],
        out_shape=[_sds(_view_shape((S, D_ATTN), dv), F32), _sds(_view_shape((S, nhb * 128), dv), F32)],
        scratch_shapes=[pltpu.VMEM((G, BLK, 128), F32), pltpu.VMEM((G, 2 * BLK, 128), F32), pltpu.VMEM((G, 2 * BLK, 128), F32),
                        pltpu.VMEM((G, BLK, 128), F32), pltpu.VMEM((G, BLK, 128), F32)],
        name=name, compiler_params=_cp(3),
    )(zv, zv, zv)
    return o.reshape(S, D_ATTN), lse.reshape(S, nhb * 128)


def _pair_weights(w, lane):
    return [jnp.where(lane < HEAD_DIM, w[:, 2 * hp:2 * hp + 1], w[:, 2 * hp + 1:2 * hp + 2]) for hp in range(N_PAIR)]


def attn_merge(os_, lses, name):
    S = os_[0].shape[0]
    tm = _row_tile(S)
    npat = len(os_)

    def body(*refs):
        o_refs, l_refs = refs[:npat], refs[npat:2 * npat]
        y_ref, yb_ref, lt_ref = refs[2 * npat:]
        lane = lax.broadcasted_iota(jnp.int32, (1, 128), 1)
        ls = [functools.reduce(jnp.add, [r[:, b * 128:(b + 1) * 128] for b in range(r.shape[1] // 128)]) for r in l_refs]
        mx = functools.reduce(jnp.maximum, ls)
        es = [jnp.exp(v - mx) for v in ls]
        den = functools.reduce(jnp.add, es)
        lt_ref[...] = mx + jnp.log(den)
        ws = [_pair_weights(e / den, lane) for e in es]
        for hp in range(N_PAIR):
            sl = slice(hp * 128, (hp + 1) * 128)
            y = ws[0][hp] * o_refs[0][:, sl]
            for p in range(1, npat):
                y = y + ws[p][hp] * o_refs[p][:, sl]
            y_ref[:, sl] = y
            yb_ref[:, sl] = y.astype(BF16)

    big = pl.BlockSpec((tm, D_ATTN), lambda i: (i, 0))
    st = pl.BlockSpec((tm, 128), lambda i: (i, 0))
    return pl.pallas_call(
        body, grid=(S // tm,), in_specs=[big] * npat + [pl.BlockSpec((tm, l.shape[1]), lambda i: (i, 0)) for l in lses],
        out_specs=[big, big, st],
        out_shape=[_sds((S, D_ATTN), F32), _sds((S, D_ATTN), BF16), _sds((S, 128), F32)],
        name=name, compiler_params=_cp(1),
    )(*os_, *lses)


def attn_bwd(z, dycat, y, lse_tot, dilation, name):
    S = z.shape[0]
    dv, d = _residue_view(dilation)
    R = BLK * d
    nb = S // (BLK * dilation)
    pp = PAIRS_PER_STEP[d]
    G = d * pp
    W = 128 * pp

    def body(q_ref, kc_ref, vc_ref, dy_ref, y_ref, l_ref, dq_ref, dk_ref, dv_ref,
             qs, ks, vs, dys, ys, ls, dqs, dks, dvs, ck, cv):
        hb, n = pl.program_id(1), pl.program_id(2)

        def store_rows(ref, buf):
            if d < PERM_FROM:
                for r in range(d):
                    for p in range(pp):
                        ref[pl.ds(r, BLK, stride=d), p * 128:(p + 1) * 128] = buf[r * pp + p]
            else:
                for c in range(d):
                    ref[c * 128:(c + 1) * 128, :] = _merge_residues(buf, d, pp, c)

        @pl.when(n < nb)
        def _():
            valid = _band_valid(n)
            first = n == 0
            lane = lax.broadcasted_iota(jnp.int32, (1, 128), 1)
            _shift_keys(ks, vs, n)
            _split_residues(q_ref, qs, d, pp)
            _split_residues(kc_ref, ks, d, pp, BLK)
            _split_residues(vc_ref, vs, d, pp, BLK)
            _split_residues(dy_ref, dys, d, pp, n_terms=2)
            _split_residues(y_ref, ys, d, pp, n_terms=2)
            _split_residues(l_ref, ls, d, 1, n_terms=3)

            def step(j, carry):
                q2, k2, v2, dy2, y2 = (qs[j] * ATTN_SCALE).astype(BF16), ks[j].astype(BF16), vs[j].astype(BF16), dys[j], ys[j]
                stat = ls[j // pp]
                pair = hb * pp + j % pp
                dq2 = jnp.zeros((BLK, 128), F32)
                dk2 = jnp.zeros((2 * BLK, 128), F32)
                dv2 = jnp.zeros((2 * BLK, 128), F32)
                for h in range(2):
                    hm = (lane < HEAD_DIM) if h == 0 else (lane >= HEAD_DIM)
                    qm = jnp.where(hm, q2, jnp.zeros_like(q2))
                    km = jnp.where(hm, k2, jnp.zeros_like(k2))
                    dym = jnp.where(hm, dy2, 0.0)
                    dymb = dym.astype(BF16)
                    s = _dot_nt(qm, k2)
                    p = jnp.where(valid, jnp.exp(s - _lane_pick(stat, lane, 2 * pair + h)), 0.0)
                    dp = _dot_nt(dymb, v2)
                    delta = jnp.sum(dym * y2, axis=-1, keepdims=True)
                    ds = (p * (dp - delta)).astype(BF16)
                    dq2 = dq2 + _dot(ds, km)
                    dk2 = dk2 + _dot_tn(ds, qm)
                    dv2 = dv2 + _dot_tn(p.astype(BF16), dymb)
                dqs[j] = dq2 * ATTN_SCALE
                dks[j] = jnp.where(first, 0.0, ck[j]) + dk2[:BLK]
                dvs[j] = jnp.where(first, 0.0, cv[j]) + dv2[:BLK]
                ck[j] = dk2[BLK:]
                cv[j] = dv2[BLK:]
                return carry

            lax.fori_loop(0, G, step, 0, unroll=min(G, 8))
            store_rows(dq_ref, dqs)
            store_rows(dk_ref, dks)
            store_rows(dv_ref, dvs)

        @pl.when(n == nb)
        def _():
            store_rows(dk_ref, ck)
            store_rows(dv_ref, cv)

    cur = lambda n: jnp.minimum(n, nb - 1)
    zb = lambda c: _view_spec(dv, R, W, lambda hb, n: cur(n), lambda hb, n: c // pp + hb)
    big_cur = _view_spec(dv, R, W, lambda hb, n: cur(n), lambda hb, n: hb)
    big_lag = _view_spec(dv, R, W, lambda hb, n: jnp.maximum(n - 1, 0), lambda hb, n: hb)
    buf = lambda rows, dt: pltpu.VMEM((G, rows, 128), dt)
    zv = _viewed(z, dv)
    outs = pl.pallas_call(
        body, grid=(dv, N_PAIR // pp, nb + 1),
        in_specs=[zb(ZQ), zb(ZK), zb(ZV), big_cur, big_cur, _view_spec(dv, R, 128, lambda hb, n: cur(n), lambda hb, n: 0)],
        out_specs=[big_cur, big_lag, big_lag],
        out_shape=[_sds(_view_shape((S, D_ATTN), dv), F32)] * 3,
        scratch_shapes=[buf(BLK, F32), buf(2 * BLK, F32), buf(2 * BLK, F32), buf(BLK, F32), buf(BLK, F32),
                        pltpu.VMEM((d, BLK, 128), F32), buf(BLK, F32), buf(BLK, F32), buf(BLK, F32), buf(BLK, F32), buf(BLK, F32)],
        name=name, compiler_params=_cp(3),
    )(zv, zv, zv, _viewed(dycat, dv), _viewed(y, dv), _viewed(lse_tot, dv))
    return tuple(o.reshape(S, D_ATTN) for o in outs)


ZC_GB, ZC_GC, ZC_CI, ZC_PI = 6, 7, 8, 9


def _pool_counts(i, tb):
    pos = lax.broadcasted_iota(jnp.int32, (tb, D_POOL), 0) + i * tb + 1
    grp = lax.broadcasted_iota(jnp.int32, (tb, D_POOL), 1) // POOL_GROUP
    win = jnp.where(grp == 0, POOL_WINDOWS[0], jnp.where(grp == 1, POOL_WINDOWS[1],
                    jnp.where(grp == 2, POOL_WINDOWS[2], POOL_WINDOWS[3])))
    return jnp.minimum(pos, win).astype(F32), grp


def _select_group(stages, grp):
    return jnp.where(grp == 0, stages[0], jnp.where(grp == 1, stages[1], jnp.where(grp == 2, stages[2], stages[3])))


def _causal_window_sums(x2):
    s1 = x2 + pltpu.roll(x2, 1, 0)
    s2 = s1 + pltpu.roll(s1, 2, 0)
    s3 = s2 + pltpu.roll(s2, 4, 0)
    s4 = s3 + pltpu.roll(s3, 8, 0)
    return [s1, s2, s3, s4]


def _anticausal_window_sums(x2):
    n = x2.shape[0]
    s1 = x2 + pltpu.roll(x2, n - 1, 0)
    s2 = s1 + pltpu.roll(s1, n - 2, 0)
    s3 = s2 + pltpu.roll(s2, n - 4, 0)
    s4 = s3 + pltpu.roll(s3, n - 8, 0)
    return [s1, s2, s3, s4]


def _pooled(p_prev, p_cur, i, tb):
    x2 = jnp.concatenate([jnp.where(i > 0, p_prev, 0.0), p_cur], axis=0)
    count, grp = _pool_counts(i, tb)
    win_sum = _select_group([s[tb:] for s in _causal_window_sums(x2)], grp)
    return win_sum / count - p_cur, count, grp


def _pool_mm(v, pw_ref, nt):
    outs = []
    for gi in range(len(POOL_WINDOWS)):
        sl = slice(gi * POOL_GROUP, (gi + 1) * POOL_GROUP)
        outs.append(_dot_nt(v[:, sl], pw_ref[gi]) if nt else _dot(v[:, sl], pw_ref[gi]))
    return jnp.concatenate(outs, axis=1)


def conv_pool_fwd(z, conv_w, pool_w, pool_scale, name):
    S = z.shape[0]
    tb = min(256, S)

    def body(gb_ref, gc_ref, gcp_ref, ci_ref, cip_ref, pi_ref, pip_ref, cw_ref, pw_ref, ps_ref, yc_ref, yp_ref):
        i = pl.program_id(0)
        u2 = jnp.concatenate([jnp.where(i > 0, gcp_ref[...] * cip_ref[...], 0.0), gc_ref[...] * ci_ref[...]], axis=0)
        conv = cw_ref[0:1, :] * pltpu.roll(u2, 2, 0) + cw_ref[1:2, :] * pltpu.roll(u2, 1, 0) + cw_ref[2:3, :] * u2
        yc_ref[...] = (gb_ref[...] * conv[tb:]).astype(BF16)
        pooled, _, _ = _pooled(pip_ref[...], pi_ref[...], i, tb)
        yp_ref[...] = (_pool_mm(pooled.astype(BF16), pw_ref, False) * ps_ref[...]).astype(BF16)

    cur = lambda c: pl.BlockSpec((tb, 512), lambda i: (i, c))
    prev = lambda c: pl.BlockSpec((tb, 512), lambda i: (jnp.maximum(i - 1, 0), c))
    full = lambda a: pl.BlockSpec(a.shape, lambda i: (0,) * a.ndim)
    out = pl.BlockSpec((tb, 512), lambda i: (i, 0))
    return pl.pallas_call(
        body, grid=(S // tb,),
        in_specs=[cur(ZC_GB), cur(ZC_GC), prev(ZC_GC), cur(ZC_CI), prev(ZC_CI), cur(ZC_PI), prev(ZC_PI),
                  full(conv_w), full(pool_w), full(pool_scale)],
        out_specs=[out, out], out_shape=[_sds((S, 512), BF16)] * 2, name=name, compiler_params=_cp(1),
    )(z, z, z, z, z, z, z, conv_w, pool_w, pool_scale)


def conv_pool_bwd(z, dycat, conv_w, pool_w, pool_scale, name):
    S = z.shape[0]
    tb = min(256, S)
    nblk = S // tb
    ng = len(POOL_WINDOWS)

    def body(gb_ref, gbn_ref, gc_ref, gcp_ref, ci_ref, cip_ref, pi_ref, pip_ref, dyc_ref, dycn_ref, dyp_ref, dypn_ref,
             cw_ref, pw_ref, ps_ref, dz_ref, dcw_ref, dpw_ref, dps_ref):
        i = pl.program_id(0)
        last = i == nblk - 1

        @pl.when(i == 0)
        def _():
            dcw_ref[...] = jnp.zeros_like(dcw_ref)
            dpw_ref[...] = jnp.zeros_like(dpw_ref)
            dps_ref[...] = jnp.zeros_like(dps_ref)

        gc, ci = gc_ref[...], ci_ref[...]
        u2 = jnp.concatenate([jnp.where(i > 0, gcp_ref[...] * cip_ref[...], 0.0), gc * ci], axis=0)
        um2, um1, u0 = pltpu.roll(u2, 2, 0)[tb:], pltpu.roll(u2, 1, 0)[tb:], u2[tb:]
        conv = cw_ref[0:1, :] * um2 + cw_ref[1:2, :] * um1 + cw_ref[2:3, :] * u0
        dyc = dyc_ref[...]
        dconv = dyc * gb_ref[...]
        dc2 = jnp.concatenate([dconv, jnp.where(last, 0.0, dycn_ref[...] * gbn_ref[...])], axis=0)
        du = (cw_ref[2:3, :] * dconv + cw_ref[1:2, :] * pltpu.roll(dc2, 2 * tb - 1, 0)[:tb]
              + cw_ref[0:1, :] * pltpu.roll(dc2, 2 * tb - 2, 0)[:tb])
        dz_ref[:, 0:512] = (dyc * conv).astype(BF16)
        dz_ref[:, 512:1024] = (du * ci).astype(BF16)
        dz_ref[:, 1024:1536] = (du * gc).astype(BF16)
        dcw_ref[0:1, :] += jnp.sum(dconv * um2, axis=0, keepdims=True)
        dcw_ref[1:2, :] += jnp.sum(dconv * um1, axis=0, keepdims=True)
        dcw_ref[2:3, :] += jnp.sum(dconv * u0, axis=0, keepdims=True)

        pooled, count, grp = _pooled(pip_ref[...], pi_ref[...], i, tb)
        pooled_b = pooled.astype(BF16)
        t = _pool_mm(pooled_b, pw_ref, False)
        dyp, ps = dyp_ref[...], ps_ref[...]
        dps_ref[...] += jnp.sum(dyp * t, axis=0, keepdims=True)
        dt_b = (dyp * ps).astype(BF16)
        for gi in range(ng):
            sl = slice(gi * POOL_GROUP, (gi + 1) * POOL_GROUP)
            dpw_ref[gi] += _dot_tn(pooled_b[:, sl], dt_b[:, sl])
        dpooled = _pool_mm(dt_b, pw_ref, True)
        dpooled_n = _pool_mm((dypn_ref[...] * ps).astype(BF16), pw_ref, True)
        count_n, _ = _pool_counts(i + 1, tb)
        dq2 = jnp.concatenate([dpooled / count, jnp.where(last, 0.0, dpooled_n / count_n)], axis=0)
        lead = _select_group([s[:tb] for s in _anticausal_window_sums(dq2)], grp)
        dz_ref[:, 1536:2048] = (lead - dpooled).astype(BF16)

    cur = lambda c: pl.BlockSpec((tb, 512), lambda i: (i, c))
    prev = lambda c: pl.BlockSpec((tb, 512), lambda i: (jnp.maximum(i - 1, 0), c))
    nxt = lambda c: pl.BlockSpec((tb, 512), lambda i: (jnp.minimum(i + 1, nblk - 1), c))
    full = lambda a: pl.BlockSpec(a.shape, lambda i: (0,) * a.ndim)
    yc_c, yp_c = D_ATTN // 512, D_ATTN // 512 + 1
    return pl.pallas_call(
        body, grid=(nblk,),
        in_specs=[cur(ZC_GB), nxt(ZC_GB), cur(ZC_GC), prev(ZC_GC), cur(ZC_CI), prev(ZC_CI), cur(ZC_PI), prev(ZC_PI),
                  cur(yc_c), nxt(yc_c), cur(yp_c), nxt(yp_c), full(conv_w), full(pool_w), full(pool_scale)],
        out_specs=[pl.BlockSpec((tb, 2048), lambda i: (i, 0)), pl.BlockSpec((3, 512), lambda i: (0, 0)),
                   pl.BlockSpec((ng, POOL_GROUP, POOL_GROUP), lambda i: (0, 0, 0)), pl.BlockSpec((1, 512), lambda i: (0, 0))],
        out_shape=[_sds((S, 2048), BF16), _sds((3, 512), F32), _sds((ng, POOL_GROUP, POOL_GROUP), F32), _sds((1, 512), F32)],
        name=name, compiler_params=_cp(1),
    )(z, z, z, z, z, z, z, z, dycat, dycat, dycat, dycat, conv_w, pool_w, pool_scale)


def assemble_dz(parts_q, parts_k, parts_v, dcp, name):
    S = dcp.shape[0]
    tm = _row_tile(S)
    npat = len(parts_q)

    def body(*refs):
        dz_ref = refs[-1]
        dcp_ref = refs[-2]
        for j in range(3):
            acc = refs[j * npat][...]
            for p in range(1, npat):
                acc = acc + refs[j * npat + p][...]
            dz_ref[:, j * D_ATTN:(j + 1) * D_ATTN] = acc.astype(BF16)
        dz_ref[:, 3 * D_ATTN:] = dcp_ref[...]

    big = pl.BlockSpec((tm, D_ATTN), lambda i: (i, 0))
    return pl.pallas_call(
        body, grid=(S // tm,), in_specs=[big] * (3 * npat) + [pl.BlockSpec((tm, D_IN - 3 * D_ATTN), lambda i: (i, 0))],
        out_specs=pl.BlockSpec((tm, D_IN), lambda i: (i, 0)), out_shape=_sds((S, D_IN), BF16),
        name=name, compiler_params=_cp(1),
    )(*parts_q, *parts_k, *parts_v, dcp)


def adamw(w, g, m, v, name):
    shape = w.shape
    cols = shape[-1]
    rows = w.size // cols
    tr = rows
    for cand in (256, 128, 64, 32, 16, 8):
        if rows % cand == 0:
            tr = cand
            break
    c1 = 1.0 - ADAM_B1 ** ADAM_STEP
    c2 = 1.0 - ADAM_B2 ** ADAM_STEP

    def body(w_ref, g_ref, m_ref, v_ref, d_ref, mo_ref, vo_ref):
        gv = g_ref[...]
        mn = ADAM_B1 * m_ref[...] + (1.0 - ADAM_B1) * gv
        vn = ADAM_B2 * v_ref[...] + (1.0 - ADAM_B2) * (gv * gv)
        d_ref[...] = -ADAM_LR * ((mn / c1) / (jnp.sqrt(vn / c2) + ADAM_EPS) + ADAM_WD * w_ref[...])
        mo_ref[...] = mn
        vo_ref[...] = vn

    blk = pl.BlockSpec((tr, cols), lambda i: (i, 0))
    outs = pl.pallas_call(
        body, grid=(rows // tr,), in_specs=[blk] * 4, out_specs=[blk] * 3,
        out_shape=[_sds((rows, cols), F32)] * 3, name=name, compiler_params=_cp(1),
    )(*(a.reshape(rows, cols) for a in (w, g, m, v)))
    return tuple(o.reshape(shape) for o in outs)


ANY = pl.BlockSpec(memory_space=pl.ANY)


def _place():
    x, y, c = lax.axis_index("x"), lax.axis_index("y"), lax.axis_index("c")
    chips = [(1 - x, y), (x, 1 - y), (1 - x, 1 - y)]
    return x, y, c, chips


def gather_weights(ws):
    nw = len(ws)
    split = [w.ndim == 2 and w.shape[0] % 32 == 0 for w in ws]

    def body(*refs):
        ins, outs = refs[:nw], refs[nw:2 * nw]
        send, recv = refs[2 * nw:]
        x, y, c, chips = _place()
        me, sib = 2 * x + y, (x, y, 1 - c)

        def half(j, k, hc):
            if not split[j]:
                return outs[j].at[k]
            ha = ws[j].shape[0] // 2
            return outs[j].at[k, pl.ds(hc * ha, ha), :]

        def rcopy(j, s, src, dst, to):
            return pltpu.make_async_remote_copy(src_ref=src, dst_ref=dst, send_sem=send.at[j, s], recv_sem=recv.at[j, s],
                                                device_id=to, device_id_type=MESH)

        first = [rcopy(j, 6, ins[j], outs[j].at[me], sib) for j in range(nw)]
        for j in range(nw):
            ha = ws[j].shape[0] // 2
            mine = ins[j].at[pl.ds(c * ha, ha), :] if split[j] else ins[j]
            for s, (px, py) in enumerate(chips):
                first.append(rcopy(j, s, mine, half(j, me, c), (px, py, c)))
        for cp in first:
            cp.start()
        passed = []
        for j in range(nw):
            for s, (px, py) in enumerate(chips):
                blk = half(j, 2 * px + py, c)
                rcopy(j, s, blk, blk, (px, py, c)).wait_recv()
                if split[j]:
                    fwd = rcopy(j, 3 + s, blk, blk, sib)
                    fwd.start()
                    passed.append(fwd)
        for j in range(nw):
            for s, (px, py) in enumerate(chips):
                if split[j]:
                    blk = half(j, 2 * px + py, 1 - c)
                    rcopy(j, 3 + s, blk, blk, sib).wait_recv()
        for j in range(nw):
            rcopy(j, 6, ins[j], outs[j].at[me], sib).wait_recv()
        for cp in first + passed:
            cp.wait_send()

    return pl.pallas_call(
        body, in_specs=[ANY] * nw, out_specs=[ANY] * nw,
        out_shape=[_sds((NSH,) + w.shape, w.dtype) for w in ws],
        scratch_shapes=[pltpu.SemaphoreType.DMA((nw, 7)), pltpu.SemaphoreType.DMA((nw, 7))],
        name="gather_weights",
    )(*ws)


def pair_sum(g, got, half_index, name):
    _, A, B = g.shape
    ha = A // 2
    tr = ha
    for cand in (512, 256, 128, 64):
        if ha % cand == 0:
            tr = cand
            break
    nt = ha // tr

    def body(c_ref, g_ref, r_ref, o_ref):
        del c_ref
        o_ref[...] = (g_ref[...] + r_ref[...]).astype(BF16)

    return pl.pallas_call(
        body,
        grid_spec=pltpu.PrefetchScalarGridSpec(
            num_scalar_prefetch=1, grid=(NSH, nt),
            in_specs=[pl.BlockSpec((None, tr, B), lambda k, t, c: (k, c[0] * nt + t, 0)),
                      pl.BlockSpec((None, tr, B), lambda k, t, c: (k, t, 0))],
            out_specs=pl.BlockSpec((None, tr, B), lambda k, t, c: (k, t, 0))),
        out_shape=_sds((NSH, ha, B), BF16), name=name, compiler_params=_cp(2),
    )(half_index, g, got)


def _half_tile(ha):
    for cand in (512, 256, 128, 64):
        if ha % cand == 0:
            return cand
    return ha


def chip_sum(partials, slots, chip_index, name):
    depth = len(partials)
    _, ha, B = partials[0].shape
    tr = _half_tile(ha)
    nt = ha // tr

    def body(me_ref, *refs):
        p_refs, s_refs, o_ref = refs[:depth], refs[depth:depth + depth * NSH], refs[depth + depth * NSH]
        l = pl.program_id(0)
        for ll in range(depth):
            @pl.when(l == ll)
            def _(ll=ll):
                own = p_refs[ll][...].astype(F32)
                acc = jnp.where(me_ref[0] == 0, own, s_refs[ll * NSH][...].astype(F32))
                for k in range(1, NSH):
                    acc = acc + jnp.where(me_ref[0] == k, own, s_refs[ll * NSH + k][...].astype(F32))
                o_ref[...] = acc

    def frozen(ll):
        return lambda l, t: jnp.where(l == ll, t, jnp.where(l < ll, 0, nt - 1))

    def slot(k):
        return lambda me: jnp.where(me[0] == k, (k + 1) % NSH, k)

    in_specs = [pl.BlockSpec((None, tr, B), lambda l, t, me, ll=ll: (me[0], frozen(ll)(l, t), 0)) for ll in range(depth)]
    in_specs += [pl.BlockSpec((None, tr, B), lambda l, t, me, ll=ll, k=k: (slot(k)(me), frozen(ll)(l, t), 0))
                 for ll in range(depth) for k in range(NSH)]
    return pl.pallas_call(
        body,
        grid_spec=pltpu.PrefetchScalarGridSpec(
            num_scalar_prefetch=1, grid=(depth, nt), in_specs=in_specs,
            out_specs=pl.BlockSpec((None, tr, B), lambda l, t, me: (l, t, 0))),
        out_shape=_sds((depth, ha, B), F32), name=name, compiler_params=_cp(2),
    )(chip_index, *partials, *[s for s in slots for _ in range(NSH)])


def sibling_swap(mine):
    n = len(mine)

    def body(*refs):
        ins, outs = refs[:n], refs[n:2 * n]
        send, recv = refs[2 * n:]
        x, y, c, _ = _place()
        cps = [pltpu.make_async_remote_copy(src_ref=ins[j], dst_ref=outs[j], send_sem=send.at[j], recv_sem=recv.at[j],
                                            device_id=(x, y, 1 - c), device_id_type=MESH) for j in range(n)]
        for cp in cps:
            cp.start()
        for cp in cps:
            cp.wait()

    return pl.pallas_call(
        body, in_specs=[ANY] * n, out_specs=[ANY] * n, out_shape=[_sds(m.shape, m.dtype) for m in mine],
        scratch_shapes=[pltpu.SemaphoreType.DMA((n,)), pltpu.SemaphoreType.DMA((n,))], name="sibling_swap",
    )(*mine)


def adamw_halves(w, mine, theirs, m, v, core_index, name, comm=None):
    depth, A, B = w.shape
    ha = A // 2
    tr = _half_tile(ha)
    while tr * B * 4 > 2 ** 20 and tr % 16 == 0:
        tr //= 2
    nt = ha // tr
    c1 = 1.0 - ADAM_B1 ** ADAM_STEP
    c2 = 1.0 - ADAM_B2 ** ADAM_STEP

    def body(c_ref, w_ref, a_ref, b_ref, m_ref, v_ref, g_ref, d_ref, mo_ref, vo_ref):
        gv = jnp.where(pl.program_id(1) == c_ref[0], a_ref[...], b_ref[...])
        mn = ADAM_B1 * m_ref[...] + (1.0 - ADAM_B1) * gv
        vn = ADAM_B2 * v_ref[...] + (1.0 - ADAM_B2) * (gv * gv)
        g_ref[...] = gv
        d_ref[...] = -ADAM_LR * ((mn / c1) / (jnp.sqrt(vn / c2) + ADAM_EPS) + ADAM_WD * w_ref[...])
        mo_ref[...] = mn
        vo_ref[...] = vn

    full = pl.BlockSpec((None, tr, B), lambda l, h, t, c: (l, h * nt + t, 0))
    a_spec = pl.BlockSpec((None, tr, B), lambda l, h, t, c: (l, jnp.where(h == c[0], t, 0), 0))
    b_spec = pl.BlockSpec((None, tr, B), lambda l, h, t, c: (l, jnp.where(h == c[0], 0, t), 0))
    return _call(
        body, grid=(depth, 2, nt), in_specs=[full, a_spec, b_spec, full, full], out_specs=[full] * 4,
        out_shape=[_sds(w.shape, F32)] * 4, name=name, args=(w, mine, theirs, m, v), comm=comm, prefetch=(core_index,))


def small_all_reduce(v):
    R = v.shape[0]

    def body(v_ref, o_ref, slots, send, recv):
        x, y, c, _ = _place()
        me = 4 * x + 2 * y + c
        slots[me] = v_ref[...]
        cps = []
        for m in range(1, 8):
            mx, my, mc = (m >> 2) & 1, (m >> 1) & 1, m & 1
            cps.append(pltpu.make_async_remote_copy(
                src_ref=v_ref, dst_ref=slots.at[me], send_sem=send.at[m - 1], recv_sem=recv.at[m - 1],
                device_id=(x ^ mx, y ^ my, c ^ mc), device_id_type=MESH))
        for cp in cps:
            cp.start()
        for cp in cps:
            cp.wait()
        acc = slots[0]
        for d in range(1, 8):
            acc = acc + slots[d]
        o_ref[...] = acc

    vm = pl.BlockSpec(memory_space=pltpu.VMEM)
    return pl.pallas_call(
        body, in_specs=[vm], out_specs=vm, out_shape=_sds((R, 128), F32),
        scratch_shapes=[pltpu.VMEM((8, R, 128), F32), pltpu.SemaphoreType.DMA((7,)), pltpu.SemaphoreType.DMA((7,))],
        name="small_all_reduce",
    )(v)


BIG = ("ffn1_w_gate", "ffn1_w_up", "ffn1_w_down", "w_in", "w_out", "ffn2_w_gate", "ffn2_w_up", "ffn2_w_down")
SMALL = ("ffn1_norm", "mix_norm", "conv_w", "pool_w", "pool_scale", "ffn2_norm", "final_norm")
WEIGHTS = ("ffn1_norm", "ffn1_w_gate", "ffn1_w_up", "ffn1_w_down", "mix_norm", "w_in", "conv_w", "pool_w", "pool_scale",
           "w_out", "ffn2_norm", "ffn2_w_gate", "ffn2_w_up", "ffn2_w_down", "final_norm")


def _tile_for(n, cap=1024):
    best = 128
    for t in range(128, min(n, cap) + 1, 128):
        if n % t == 0:
            best = t
    return n if n <= cap else best


GATHER_BEHIND = {
    (0, "ffn1_up"): [("w_in", 0), ("w_out", 0), ("ffn2_w_gate", 0), ("ffn2_w_up", 0)],
    (0, "ffn1_down"): [("ffn2_w_down", 0)],
    (0, "mix_in"): [("ffn1_w_gate", 1)],
    (0, "mix_out"): [("ffn1_w_up", 1)],
    (0, "ffn2_up"): [("ffn1_w_down", 1), ("w_in", 1), ("w_out", 1)],
    (0, "ffn2_down"): [("ffn2_w_gate", 1)],
    (1, "ffn1_up"): [("ffn2_w_up", 1), ("ffn2_w_down", 1)],
}


class _GatherPlan:
    def __init__(self, local):
        self.local, self.pending, self.ready = local, [], {}

    def comm(self, firsts):
        cm = _Comm()
        self._passing, self._firsts = list(self.pending), list(firsts)
        for _, buf in self._passing:
            cm.gather_pass(buf)
        for key in self._firsts:
            cm.gather_first(self.local[key])
        return cm

    def done(self, couts):
        npass = len(self._passing)
        for (key, _), buf in zip(self._passing, couts[:npass]):
            self.ready[key] = buf
        self.pending = list(zip(self._firsts, couts[npass:]))


def kernel(x, ffn1_norm, ffn1_w_gate, ffn1_w_up, ffn1_w_down, mix_norm, w_in, conv_w, pool_w, pool_scale, w_out, ffn2_norm, ffn2_w_gate, ffn2_w_up, ffn2_w_down, final_norm, loss_target, m_ffn1_norm, m_ffn1_w_gate, m_ffn1_w_up, m_ffn1_w_down, m_mix_norm, m_w_in, m_conv_w, m_pool_w, m_pool_scale, m_w_out, m_ffn2_norm, m_ffn2_w_gate, m_ffn2_w_up, m_ffn2_w_down, m_final_norm, v_ffn1_norm, v_ffn1_w_gate, v_ffn1_w_up, v_ffn1_w_down, v_mix_norm, v_w_in, v_conv_w, v_pool_w, v_pool_scale, v_w_out, v_ffn2_norm, v_ffn2_w_gate, v_ffn2_w_up, v_ffn2_w_down, v_final_norm):
    given = dict(locals())
    W = {n: given[n] for n in WEIGHTS}
    M = {n: given["m_" + n] for n in WEIGHTS}
    V = {n: given["v_" + n] for n in WEIGHTS}
    depth = ffn1_norm.shape[0]
    D = x.shape[-1]
    xs = x[0]
    my_chip = 2 * lax.axis_index("x") + lax.axis_index("y")
    my_core = lax.axis_index("c")

    half_index = my_core.astype(jnp.int32).reshape(1)
    chip_index = my_chip.astype(jnp.int32).reshape(1)
    plan = _GatherPlan({(n, l): W[n][l].astype(BF16) for n in BIG for l in range(depth)})
    head = [("ffn1_w_gate", 0), ("ffn1_w_up", 0), ("ffn1_w_down", 0)]
    gathered = gather_weights([plan.local[key] for key in head] + [conv_w])
    plan.ready.update(zip(head, gathered[:-1]))
    conv_full = jnp.moveaxis(gathered[-1], 0, 2).reshape(depth, 3, D_CONV)
    pool_b = pool_w.astype(BF16)
    G = plan.ready

    def behind(l, stage):
        return plan.comm([(n, ll) for n, ll in GATHER_BEHIND.get((l, stage), []) if ll < depth])

    saved = []
    cur = xs
    for l in range(depth):
        x0 = cur
        h = rms_fwd(x0, ffn1_norm[l], "ffn1_norm")
        (g, u, a), got = ffn_up(h, G["ffn1_w_gate", l], G["ffn1_w_up", l], "ffn1_up", behind(l, "ffn1_up"))
        plan.done(got)
        cur, got = resid_mm(a, G["ffn1_w_down", l], x0, 0.5, "ffn1_down", behind(l, "ffn1_down"))
        plan.done(got)
        s1 = (x0, h, g, u, a)

        x1 = cur
        hm = rms_fwd(x1, mix_norm[l], "mix_norm")
        z, got = col_mm(hm, G["w_in", l], "mix_in", behind(l, "mix_in"))
        plan.done(got)
        pats = [attn_fwd(z, d, f"attn_fwd_d{d}") for d in DILATIONS]
        y, yb, lse_tot = attn_merge([p[0] for p in pats], [p[1] for p in pats], "attn_merge")
        y_conv, y_pool = conv_pool_fwd(z, conv_full[l], pool_b[l], pool_scale[l].reshape(1, D_POOL), "conv_pool_fwd")
        ycat = jnp.concatenate([yb, y_conv, y_pool], axis=1)
        cur, got = resid_mm(ycat, G["w_out", l], x1, 1.0, "mix_out", behind(l, "mix_out"))
        plan.done(got)
        sm = (x1, hm, z, y, lse_tot, ycat)

        x2 = cur
        h = rms_fwd(x2, ffn2_norm[l], "ffn2_norm")
        (g, u, a), got = ffn_up(h, G["ffn2_w_gate", l], G["ffn2_w_up", l], "ffn2_up", behind(l, "ffn2_up"))
        plan.done(got)
        cur, got = resid_mm(a, G["ffn2_w_down", l], x2, 0.5, "ffn2_down", behind(l, "ffn2_down"))
        plan.done(got)
        saved.append((s1, sm, (x2, h, g, u, a)))
    assert not plan.pending and len(G) == len(BIG) * depth

    loss11, dx, dxb, d_final = final_loss(cur, final_norm, loss_target[0], "final_loss")

    small_grads = {n: [None] * depth for n in SMALL if n != "final_norm"}
    partials = {n: [None] * depth for n in BIG}
    slots = {n: [None] * depth for n in BIG}

    def halves(grad):
        cm = _Comm()
        cm.sibling_half(grad)
        return cm

    carry = []

    def carried_comm():
        cm = _Comm()
        taken = list(carry)
        carry.clear()
        for _, _, p in taken:
            cm.exchange(p)
        return cm, taken

    def settle(taken, got):
        for (n, ll, p), s in zip(taken, got):
            partials[n][ll], slots[n][ll] = p, s

    def finish_sublayer(name, l, grad, dys, ws, x_in, gain, dx, tag, cm, final):
        if not final:
            dx, dxb, dnorm, got, _ = bwd_dh(dys, ws, x_in, gain, dx, tag, cm)
            carry.append((name, l, pair_sum(grad, got[0], half_index, "pair_sum")))
            return dx, dxb, dnorm, got
        last = {}

        def after(got):
            last["p"] = pair_sum(grad, got[0], half_index, "pair_sum")
            cm2 = _Comm()
            cm2.exchange(last["p"])
            return cm2

        dx, dxb, dnorm, got, got2 = bwd_dh(dys, ws, x_in, gain, dx, tag, cm, after)
        partials[name][l], slots[name][l] = last["p"], got2[0]
        return dx, dxb, dnorm, got

    def ffn_backward(sv, dx, dxb, norm, tag, l, final=False):
        x_in, h, g, u, a = sv
        names = [f"{tag}_w_down", f"{tag}_w_gate", f"{tag}_w_up"]
        wd, wg, wu = (G[n, l] for n in names)
        FS = wg.shape[-1]
        cm0, taken = carried_comm()
        (dg, du), got0 = ffn_bwd_act(dxb, wd, g, u, f"{tag}_bwd_act", cm0)
        settle(taken, got0)
        g_d, _ = wgrad(a, dxb, True, False, 0.5, FS, _tile_for(D), f"{tag}_dwd")
        g_g, (got_d,) = wgrad(h, dg, False, True, 1.0, _tile_for(D), FS, f"{tag}_dwg", halves(g_d))
        g_u, (got_g,) = wgrad(h, du, False, True, 1.0, _tile_for(D), FS, f"{tag}_dwu", halves(g_g))
        p_d = pair_sum(g_d, got_d, half_index, "pair_sum")
        p_g = pair_sum(g_g, got_g, half_index, "pair_sum")
        cm = halves(g_u)
        cm.exchange(p_d)
        cm.exchange(p_g)
        dx, dxb, dnorm, got = finish_sublayer(names[2], l, g_u, [dg, du], [wg, wu], x_in, norm[l], dx, f"{tag}_bwd_dh", cm, final)
        settle([(names[0], l, p_d), (names[1], l, p_g)], got[1:])
        return dx, dxb, dnorm

    for l in reversed(range(depth)):
        s1, sm, s2 = saved[l]
        dx, dxb, small_grads["ffn2_norm"][l] = ffn_backward(s2, dx, dxb, ffn2_norm, "ffn2", l)

        x1, hm, z, y, lse_tot, ycat = sm
        cm0, taken = carried_comm()
        dycat, got0 = nt_col_mm(dxb, G["w_out", l], "mix_out_bwd", cm0)
        settle(taken, got0)
        g_wout, _ = wgrad(ycat, dxb, True, False, 1.0, D_MIX // NSH, _tile_for(D, 2048), "mix_dwout")
        parts = [attn_bwd(z, dycat, y, lse_tot, d, f"attn_bwd_d{d}") for d in DILATIONS]
        dcp, dcw, dpw, dps = conv_pool_bwd(z, dycat, conv_full[l], pool_b[l], pool_scale[l].reshape(1, D_POOL), "conv_pool_bwd")
        dz = assemble_dz([p[0] for p in parts], [p[1] for p in parts], [p[2] for p in parts], dcp, "assemble_dz")
        g_win, (got_out,) = wgrad(hm, dz, False, True, 1.0, _tile_for(D), D_IN // NSH, "mix_dwin", halves(g_wout))
        p_out = pair_sum(g_wout, got_out, half_index, "pair_sum")
        cm = halves(g_win)
        cm.exchange(p_out)
        dx, dxb, dnm, got = finish_sublayer("w_in", l, g_win, [dz], [G["w_in", l]], x1, mix_norm[l], dx, "mix_bwd_dh", cm, False)
        settle([("w_out", l, p_out)], got[1:])
        small_grads["mix_norm"][l], small_grads["conv_w"][l] = dnm, dcw
        small_grads["pool_w"][l], small_grads["pool_scale"][l] = dpw, dps

        dx, dxb, small_grads["ffn1_norm"][l] = ffn_backward(s1, dx, dxb, ffn1_norm, "ffn1", l, final=(l == 0))
    assert not carry

    mine = [chip_sum(partials[n], slots[n], chip_index, "chip_sum") for n in BIG]
    theirs = sibling_swap(mine[:1])[0]
    grads, delta, new_m, new_v = {}, {}, {}, {}
    for i, n in enumerate(BIG):
        cm = _Comm()
        if i + 1 < len(BIG):
            cm.sibling_whole(mine[i + 1])
        (grads[n], delta[n], new_m[n], new_v[n]), got = adamw_halves(W[n], mine[i], theirs, M[n], V[n], half_index, "adamw", cm)
        theirs = got[0] if got else None

    small_full = {n: jnp.stack([a.reshape(W[n].shape[1:] if n != "conv_w" else (3, D_CONV)) for a in small_grads[n]])
                  for n in small_grads}
    small_full["final_norm"] = d_final.reshape(D)
    order = list(SMALL)
    packed = jnp.concatenate([small_full[n].reshape(-1) for n in order])
    pad = (-packed.shape[0]) % (8 * 128)
    packed = jnp.pad(packed, (0, pad)).reshape(-1, 128)
    summed = small_all_reduce(packed).reshape(-1)
    off = 0
    for n in order:
        size = small_full[n].size
        grads[n] = summed[off:off + size].reshape(small_full[n].shape)
        off += size
    grads["conv_w"] = lax.dynamic_slice_in_dim(grads["conv_w"], my_chip * (D_CONV // NSH), D_CONV // NSH, axis=2)

    def pack(src):
        flat_ = jnp.concatenate([src[n].reshape(-1) for n in order])
        return jnp.pad(flat_, (0, (-flat_.shape[0]) % (8 * 128))).reshape(-1, 128)

    ds, ms, vs = adamw(pack(W), pack(grads), pack(M), pack(V), "adamw_small")
    off = 0
    for n in order:
        size = W[n].size
        for dst, src in ((delta, ds), (new_m, ms), (new_v, vs)):
            dst[n] = src.reshape(-1)[off:off + size].reshape(W[n].shape)
        off += size

    loss = lax.psum(loss11[0, 0], ("x", "y", "c"))
    return (loss, dx.reshape(x.shape), *[grads[n] for n in WEIGHTS], *[delta[n] for n in WEIGHTS],
            *[new_m[n] for n in WEIGHTS], *[new_v[n] for n in WEIGHTS])
```
